```python
import jax
import jax.numpy as jnp
from jax import lax
import numpy as np

D_MODEL = 1024
BATCH = 1
SEQ = 16384
DEPTH = 2
DEC_BATCH = 32
DEC_SEQ = 16
PAST_LEN = 4096

CHUNK = 64
N_MEM = 256
HEAD_DIM = 64
A_HEADS = 8
A_WIDTH = A_HEADS * HEAD_DIM
A_PREV_CHUNKS = 8
A_WINDOW = A_PREV_CHUNKS * CHUNK
A_BAND = (A_PREV_CHUNKS + 1) * CHUNK
REL_CLIP = 128
POOL_WINDOWS = (2, 4, 8, 16)
B_WIDTH = D_MODEL - A_WIDTH
B_GROUP = B_WIDTH // len(POOL_WINDOWS)
POOL_STATE = max(POOL_WINDOWS) - 1
C_HEADS = D_MODEL // HEAD_DIM
C_WIDTH = C_HEADS * HEAD_DIM
Q_BLOCK = 128
FORGET_BIAS = 4.0
X_HEADS = 4
X_HEAD_DIM = D_MODEL // X_HEADS
D_FF = 2816
N_EXPERTS = 8
TOP_K = 2
N_PAIRS = DEPTH // 2
ALPHA = (2.0 * DEPTH) ** 0.25
BETA = (8.0 * DEPTH) ** -0.25
LN_EPS = 1e-5
NEG_INF = -1e30

kernel_name = 'streaming_hybrid_band_pool_fox_step'


def layer_norm(x, g, b):
    xf = x.astype(jnp.float32)
    mu = jnp.mean(xf, axis=-1, keepdims=True)
    var = jnp.mean(jnp.square(xf - mu), axis=-1, keepdims=True)
    return ((xf - mu) * lax.rsqrt(var + LN_EPS)).astype(x.dtype) * g + b


def deepnorm_residual(x, sub, g, b):
    return layer_norm(ALPHA * x + sub, g, b)


def rel_index(d):
    return jnp.clip(d, -REL_CLIP, REL_CLIP) + REL_CLIP


def band_attention_prompt(q, k, v, rel_table):
    B, S, H, Dh = q.shape
    nc = S // CHUNK
    qc = q.reshape(B, nc, CHUNK, H, Dh)
    pad = jnp.zeros((B, A_WINDOW, H, Dh), k.dtype)
    kp = jnp.concatenate([pad, k], 1).reshape(B, nc + A_PREV_CHUNKS, CHUNK, H, Dh)
    vp = jnp.concatenate([pad, v], 1).reshape(B, nc + A_PREV_CHUNKS, CHUNK, H, Dh)
    kb = jnp.concatenate([kp[:, j:j + nc] for j in range(A_PREV_CHUNKS + 1)], axis=2)
    vb = jnp.concatenate([vp[:, j:j + nc] for j in range(A_PREV_CHUNKS + 1)], axis=2)
    q_off = jnp.arange(CHUNK)
    k_off = jnp.arange(A_BAND) - A_WINDOW
    bias = rel_table[:, rel_index(q_off[:, None] - k_off[None, :])].astype(jnp.float32)
    valid = (jnp.arange(nc)[:, None] * CHUNK + k_off[None, :]) >= 0
    s = jnp.einsum('bcqhd,bckhd->bchqk', qc, kb).astype(jnp.float32) * (Dh ** -0.5)
    s = jnp.where(valid[None, :, None, None, :], s + bias, NEG_INF)
    p = jax.nn.softmax(s, axis=-1).astype(v.dtype)
    o = jnp.einsum('bchqk,bckhd->bcqhd', p, vb)
    return o.reshape(B, S, H * Dh)


def band_attention_sample(q, k, v, k_cache, v_cache, rel_table, past_len):
    B, T, H, Dh = q.shape
    W = k_cache.shape[1]
    kk = jnp.concatenate([k_cache, k], 1)
    vv = jnp.concatenate([v_cache, v], 1)
    q_pos = past_len + jnp.arange(T)
    k_pos = past_len - W + jnp.arange(W + T)
    bias = rel_table[:, rel_index(q_pos[:, None] - k_pos[None, :])].astype(jnp.float32)
    s = jnp.einsum('bqhd,bkhd->bhqk', q, kk).astype(jnp.float32) * (Dh ** -0.5) + bias
    p = jax.nn.softmax(s, axis=-1).astype(vv.dtype)
    o = jnp.einsum('bhqk,bkhd->bqhd', p, vv)
    return o.reshape(B, T, H * Dh)


def multiscale_pool(u, u_hist, pos0, pool_w, pool_scale):
    B, T, _ = u.shape
    P = u_hist.shape[1]
    ue = jnp.concatenate([u_hist, u], 1).astype(jnp.float32)
    cs = jnp.pad(jnp.cumsum(ue, axis=1), ((0, 0), (POOL_STATE + 1, 0), (0, 0)))
    pos = pos0 + jnp.arange(T).astype(jnp.float32)
    off = POOL_STATE + 1 + P
    outs = []
    for g, w in enumerate(POOL_WINDOWS):
        sl = slice(g * B_GROUP, (g + 1) * B_GROUP)
        win = cs[:, off:off + T, sl] - cs[:, off - w:off - w + T, sl]
        cnt = jnp.minimum(float(w), pos + 1.0)
        outs.append(win / cnt[None, :, None] - ue[:, P:, sl])
    d = jnp.stack(outs, axis=2).astype(u.dtype)
    y = jnp.einsum('btgc,gce->btge', d, pool_w).reshape(B, T, B_WIDTH)
    return y * pool_scale


def forgetting_attention_prompt(q, k, v, logf):
    B, S, H, Dh = q.shape
    dcum = jnp.cumsum(logf, axis=1).transpose(0, 2, 1)
    k_pos = jnp.arange(S)

    def block(i):
        q0 = i * Q_BLOCK
        qb = lax.dynamic_slice_in_dim(q, q0, Q_BLOCK, axis=1)
        dq = lax.dynamic_slice_in_dim(dcum, q0, Q_BLOCK, axis=2)
        s = jnp.einsum('bqhd,bkhd->bhqk', qb, k).astype(jnp.float32) * (Dh ** -0.5)
        s = s + dq[..., :, None] - dcum[..., None, :]
        q_pos = q0 + jnp.arange(Q_BLOCK)
        s = jnp.where(k_pos[None, :] <= q_pos[:, None], s, NEG_INF)
        p = jax.nn.softmax(s, axis=-1).astype(v.dtype)
        return jnp.einsum('bhqk,bkhd->bqhd', p, v)

    o = lax.map(block, jnp.arange(S // Q_BLOCK))
    return o.transpose(1, 0, 2, 3, 4).reshape(B, S, H * Dh)


def forgetting_attention_sample(q, k, v, logf, k_cache, v_cache, lf_cache):
    B, T, H, Dh = q.shape
    P = k_cache.shape[1]
    kk = jnp.concatenate([k_cache, k], 1)
    vv = jnp.concatenate([v_cache, v], 1)
    dcum = jnp.cumsum(jnp.concatenate([lf_cache.astype(jnp.float32), logf], 1), axis=1).transpose(0, 2, 1)
    s = jnp.einsum('bqhd,bkhd->bhqk', q, kk).astype(jnp.float32) * (Dh ** -0.5)
    s = s + dcum[..., P:, None] - dcum[..., None, :]
    mask = jnp.arange(P + T)[None, :] <= (P + jnp.arange(T))[:, None]
    s = jnp.where(mask, s, NEG_INF)
    p = jax.nn.softmax(s, axis=-1).astype(vv.dtype)
    o = jnp.einsum('bhqk,bkhd->bqhd', p, vv)
    return o.reshape(B, T, H * Dh)


def mixer_ab(x, cache, past_len, w_in, rel_table, pool_w, pool_scale, w_out):
    B, T, _ = x.shape
    h = x @ w_in
    q = h[..., :A_WIDTH].reshape(B, T, A_HEADS, HEAD_DIM)
    k = h[..., A_WIDTH:2 * A_WIDTH].reshape(B, T, A_HEADS, HEAD_DIM)
    v = h[..., 2 * A_WIDTH:3 * A_WIDTH].reshape(B, T, A_HEADS, HEAD_DIM)
    u = h[..., 3 * A_WIDTH:]
    if cache is None:
        o_a = band_attention_prompt(q, k, v, rel_table)
        o_b = multiscale_pool(u, u[:, :0], 0, pool_w, pool_scale)
        keep = min(A_WINDOW, T)
        new = (k[:, T - keep:], v[:, T - keep:], u[:, T - POOL_STATE:])
    else:
        k_cache, v_cache, u_hist = cache
        o_a = band_attention_sample(q, k, v, k_cache, v_cache, rel_table, past_len)
        o_b = multiscale_pool(u, u_hist, past_len, pool_w, pool_scale)
        new = (k, v, jnp.concatenate([u_hist, u], 1)[:, -POOL_STATE:])
    y = jnp.concatenate([o_a, o_b], axis=-1) @ w_out
    return y, new


def mixer_c(x, cache, w_in, b_f, w_out):
    B, T, _ = x.shape
    h = x @ w_in
    q = h[..., :C_WIDTH].reshape(B, T, C_HEADS, HEAD_DIM)
    k = h[..., C_WIDTH:2 * C_WIDTH].reshape(B, T, C_HEADS, HEAD_DIM)
    v = h[..., 2 * C_WIDTH:3 * C_WIDTH].reshape(B, T, C_HEADS, HEAD_DIM)
    logf = jax.nn.log_sigmoid(h[..., 3 * C_WIDTH:].astype(jnp.float32) + b_f.astype(jnp.float32))
    if cache is None:
        o = forgetting_attention_prompt(q, k, v, logf)
    else:
        k_cache, v_cache, lf_cache = cache
        o = forgetting_attention_sample(q, k, v, logf, k_cache, v_cache, lf_cache)
    return o @ w_out, (k, v, logf)


def memory_cross_attention(x, mem_k, mem_v, w_q, w_o):
    B, T, _ = x.shape
    q = (x @ w_q).reshape(B, T, X_HEADS, X_HEAD_DIM)
    s = jnp.einsum('bqhd,bmhd->bhqm', q, mem_k).astype(jnp.float32) * (X_HEAD_DIM ** -0.5)
    p = jax.nn.softmax(s, axis=-1).astype(mem_v.dtype)
    o = jnp.einsum('bhqm,bmhd->bqhd', p, mem_v).reshape(B, T, D_MODEL)
    return o @ w_o


def swiglu(x, w1, w3, w2):
    return (jax.nn.silu(x @ w1) * (x @ w3)) @ w2


def moe_swiglu(x, w_router, b_router, w1, w3, w2):
    logits = (x @ w_router).astype(jnp.float32) + b_router.astype(jnp.float32)
    top_val, top_idx = lax.top_k(logits, TOP_K)
    gates = jax.nn.softmax(top_val, axis=-1)
    comb = jnp.einsum('btk,btke->bte', gates, jax.nn.one_hot(top_idx, N_EXPERTS, dtype=jnp.float32)).astype(x.dtype)
    y = jnp.zeros_like(x)
    for e in range(N_EXPERTS):
        y = y + comb[..., e:e + 1] * swiglu(x, w1[e], w3[e], w2[e])
    return y


def trunk(x, mem_k, mem_v, caches, past_len, w_in_ab, rel_bias_a, pool_w, pool_scale, w_out_ab,
          w_in_c, b_f, w_out_c, w_xq, w_xo, ln_g, ln_b, ffn_w1, ffn_w3, ffn_w2,
          w_router, b_router, moe_w1, moe_w3, moe_w2):
    a_k, a_v, pool_st, c_k, c_v, c_lf = [], [], [], [], [], []
    for layer in range(DEPTH):
        p = layer // 2
        if layer % 2 == 0:
            cache = None if caches is None else (caches[0][p], caches[1][p], caches[2][p])
            y, (nk, nv, nu) = mixer_ab(x, cache, past_len, w_in_ab[p], rel_bias_a[p], pool_w[p], pool_scale[p], w_out_ab[p])
            a_k.append(nk)
            a_v.append(nv)
            pool_st.append(nu)
        else:
            cache = None if caches is None else (caches[3][p], caches[4][p], caches[5][p])
            y, (nk, nv, nl) = mixer_c(x, cache, w_in_c[p], b_f[p], w_out_c[p])
            c_k.append(nk)
            c_v.append(nv)
            c_lf.append(nl)
        x = deepnorm_residual(x, y, ln_g[layer, 0], ln_b[layer, 0])
        x = deepnorm_residual(x, memory_cross_attention(x, mem_k[layer], mem_v[layer], w_xq[layer], w_xo[layer]),
                              ln_g[layer, 1], ln_b[layer, 1])
        if layer % 2 == 0:
            f = swiglu(x, ffn_w1[p], ffn_w3[p], ffn_w2[p])
        else:
            f = moe_swiglu(x, w_router[p], b_router[p], moe_w1[p], moe_w3[p], moe_w2[p])
        x = deepnorm_residual(x, f, ln_g[layer, 2], ln_b[layer, 2])
    states = (jnp.stack(a_k), jnp.stack(a_v), jnp.stack(pool_st), jnp.stack(c_k), jnp.stack(c_v), jnp.stack(c_lf))
    return x, states


def setup_inputs(seed: int = 0) -> dict:
    key = jax.random.key(seed)
    ks = iter(jax.random.split(key, 48))

    def nrm(shape, scale=1.0):
        return jax.random.normal(next(ks), shape, jnp.float32) * scale

    a_rows = min(A_WINDOW, PAST_LEN)
    d_in = D_MODEL ** -0.5
    return {
        'x_prompt': nrm((BATCH, SEQ, D_MODEL)),
        'x_sample': nrm((DEC_BATCH, DEC_SEQ, D_MODEL)),
        'cache_a_k': nrm((N_PAIRS, DEC_BATCH, a_rows, A_HEADS, HEAD_DIM)),
        'cache_a_v': nrm((N_PAIRS, DEC_BATCH, a_rows, A_HEADS, HEAD_DIM)),
        'state_pool': nrm((N_PAIRS, DEC_BATCH, POOL_STATE, B_WIDTH)),
        'cache_c_k': nrm((N_PAIRS, DEC_BATCH, PAST_LEN, C_HEADS, HEAD_DIM)),
        'cache_c_v': nrm((N_PAIRS, DEC_BATCH, PAST_LEN, C_HEADS, HEAD_DIM)),
        'cache_c_logf': jax.nn.log_sigmoid(FORGET_BIAS + nrm((N_PAIRS, DEC_BATCH, PAST_LEN, C_HEADS))),
        'cache_mem_k': nrm((DEPTH, DEC_BATCH, N_MEM, X_HEADS, X_HEAD_DIM)),
        'cache_mem_v': nrm((DEPTH, DEC_BATCH, N_MEM, X_HEADS, X_HEAD_DIM)),
        'mem_prompt': nrm((BATCH, N_MEM, D_MODEL)),
        'w_in_ab': nrm((N_PAIRS, D_MODEL, 3 * A_WIDTH + B_WIDTH), d_in),
        'rel_bias_a': nrm((N_PAIRS, A_HEADS, 2 * REL_CLIP + 1), 0.2),
        'pool_w': nrm((N_PAIRS, len(POOL_WINDOWS), B_GROUP, B_GROUP), B_GROUP ** -0.5),
        'pool_scale': 1.0 + nrm((N_PAIRS, B_WIDTH), 0.02),
        'w_out_ab': nrm((N_PAIRS, D_MODEL, D_MODEL), d_in * BETA),
        'w_in_c': nrm((N_PAIRS, D_MODEL, 3 * C_WIDTH + C_HEADS), d_in),
        'b_f': FORGET_BIAS + nrm((N_PAIRS, C_HEADS), 0.5),
        'w_out_c': nrm((N_PAIRS, C_WIDTH, D_MODEL), C_WIDTH ** -0.5 * BETA),
        'w_xq': nrm((DEPTH, D_MODEL, D_MODEL), d_in),
        'w_xk': nrm((DEPTH, D_MODEL, D_MODEL), d_in),
        'w_xv': nrm((DEPTH, D_MODEL, D_MODEL), d_in),
        'w_xo': nrm((DEPTH, D_MODEL, D_MODEL), d_in * BETA),
        'ln_g': 1.0 + nrm((DEPTH, 3, D_MODEL), 0.02),
        'ln_b': nrm((DEPTH, 3, D_MODEL), 0.02),
        'ffn_w1': nrm((N_PAIRS, D_MODEL, D_FF), d_in),
        'ffn_w3': nrm((N_PAIRS, D_MODEL, D_FF), d_in),
        'ffn_w2': nrm((N_PAIRS, D_FF, D_MODEL), D_FF ** -0.5 * BETA),
        'w_router': nrm((N_PAIRS, D_MODEL, N_EXPERTS), d_in),
        'b_router': nrm((N_PAIRS, N_EXPERTS), 0.01),
        'moe_w1': nrm((N_PAIRS, N_EXPERTS, D_MODEL, D_FF), d_in),
        'moe_w3': nrm((N_PAIRS, N_EXPERTS, D_MODEL, D_FF), d_in),
        'moe_w2': nrm((N_PAIRS, N_EXPERTS, D_FF, D_MODEL), D_FF ** -0.5 * BETA),
    }


def reference(x_prompt, x_sample, cache_a_k, cache_a_v, state_pool, cache_c_k, cache_c_v, cache_c_logf,
              cache_mem_k, cache_mem_v, mem_prompt, w_in_ab, rel_bias_a, pool_w, pool_scale, w_out_ab,
              w_in_c, b_f, w_out_c, w_xq, w_xk, w_xv, w_xo, ln_g, ln_b, ffn_w1, ffn_w3, ffn_w2,
              w_router, b_router, moe_w1, moe_w3, moe_w2):
    bp = mem_prompt.shape[0]
    p_mem_k = jnp.einsum('bmd,lde->lbme', mem_prompt, w_xk).reshape(DEPTH, bp, N_MEM, X_HEADS, X_HEAD_DIM)
    p_mem_v = jnp.einsum('bmd,lde->lbme', mem_prompt, w_xv).reshape(DEPTH, bp, N_MEM, X_HEADS, X_HEAD_DIM)

    y_prompt, (p_a_k, p_a_v, p_pool, p_c_k, p_c_v, p_c_logf) = trunk(
        x_prompt, p_mem_k, p_mem_v, None, 0, w_in_ab, rel_bias_a, pool_w, pool_scale, w_out_ab,
        w_in_c, b_f, w_out_c, w_xq, w_xo, ln_g, ln_b, ffn_w1, ffn_w3, ffn_w2,
        w_router, b_router, moe_w1, moe_w3, moe_w2)

    past_len = cache_c_k.shape[2]
    y_sample, (s_a_k, s_a_v, s_pool, s_c_k, s_c_v, s_c_logf) = trunk(
        x_sample, cache_mem_k, cache_mem_v,
        (cache_a_k, cache_a_v, state_pool, cache_c_k, cache_c_v, cache_c_logf), past_len,
        w_in_ab, rel_bias_a, pool_w, pool_scale, w_out_ab,
        w_in_c, b_f, w_out_c, w_xq, w_xo, ln_g, ln_b, ffn_w1, ffn_w3, ffn_w2,
        w_router, b_router, moe_w1, moe_w3, moe_w2)

    return (y_prompt, y_sample, p_a_k, p_a_v, p_pool, p_c_k, p_c_v, p_c_logf, p_mem_k, p_mem_v,
            s_a_k, s_a_v, s_pool, s_c_k, s_c_v, s_c_logf)
```

```python
import functools

import jax
import jax.numpy as jnp
from jax import lax
from jax.experimental import pallas as pl
from jax.experimental.pallas import tpu as pltpu

F32 = jnp.float32
BF16 = jnp.bfloat16

D_MODEL = 1024
DEPTH = 2
CHUNK = 64
N_MEM = 256
HEAD_DIM = 64
A_HEADS = 8
A_WIDTH = A_HEADS * HEAD_DIM
A_PREV_CHUNKS = 8
A_WINDOW = A_PREV_CHUNKS * CHUNK
A_BAND = (A_PREV_CHUNKS + 1) * CHUNK
REL_CLIP = 128
POOL_WINDOWS = (2, 4, 8, 16)
B_WIDTH = D_MODEL - A_WIDTH
B_GROUP = B_WIDTH // len(POOL_WINDOWS)
POOL_STATE = max(POOL_WINDOWS) - 1
C_HEADS = D_MODEL // HEAD_DIM
C_WIDTH = C_HEADS * HEAD_DIM
X_HEADS = 4
X_HEAD_DIM = D_MODEL // X_HEADS
D_FF = 2816
N_EXPERTS = 8
ALPHA = (2.0 * DEPTH) ** 0.25
LN_EPS = 1e-5
NEG_INF = -1e30

LANES = 128
PAIR = LANES // HEAD_DIM
VMEM_LIMIT = 56 * 1024 * 1024

NT_DIMS = (((1,), (1,)), ((), ()))


def _cparams(*sem):
    return pltpu.CompilerParams(dimension_semantics=sem, vmem_limit_bytes=VMEM_LIMIT)


def _deepnorm_ln(res, sub, g, b):
    z = ALPHA * res + sub
    mu = jnp.mean(z, axis=-1, keepdims=True)
    zc = z - mu
    var = jnp.mean(zc * zc, axis=-1, keepdims=True)
    return zc * lax.rsqrt(var + LN_EPS) * g + b


def _mm_kernel(x_ref, w_ref, *o_refs):
    y = jnp.dot(x_ref[...].astype(BF16), w_ref[...], preferred_element_type=F32)
    for o_ref in o_refs:
        o_ref[...] = y.astype(o_ref.dtype)


def matmul(x, w, out_dtypes, *, tm, tn, name):
    m, k = x.shape
    n = w.shape[1]
    outs = pl.pallas_call(
        _mm_kernel,
        grid=(n // tn, m // tm),
        in_specs=[pl.BlockSpec((tm, k), lambda j, i: (i, 0)),
                  pl.BlockSpec((k, tn), lambda j, i: (0, j))],
        out_specs=[pl.BlockSpec((tm, tn), lambda j, i: (i, j)) for _ in out_dtypes],
        out_shape=[jax.ShapeDtypeStruct((m, n), dt) for dt in out_dtypes],
        compiler_params=_cparams("arbitrary", "arbitrary"),
        name=name,
    )(x, w)
    return outs


def _mm_logsig_kernel(x_ref, w_ref, b_ref, o_ref):
    z = jnp.dot(x_ref[...].astype(BF16), w_ref[...], preferred_element_type=F32)
    o_ref[...] = jax.nn.log_sigmoid(z + b_ref[...])


def matmul_logsig(x, w, b, *, tm, name):
    m, k = x.shape
    n = w.shape[1]
    return pl.pallas_call(
        _mm_logsig_kernel,
        grid=(m // tm,),
        in_specs=[pl.BlockSpec((tm, k), lambda i: (i, 0)),
                  pl.BlockSpec((k, n), lambda i: (0, 0)),
                  pl.BlockSpec((1, n), lambda i: (0, 0))],
        out_specs=pl.BlockSpec((tm, n), lambda i: (i, 0)),
        out_shape=jax.ShapeDtypeStruct((m, n), F32),
        compiler_params=_cparams("arbitrary"),
        name=name,
    )(x, w, b)


def _mm_ln_kernel(x_ref, w_ref, res_ref, g_ref, b_ref, o_ref):
    y = jnp.dot(x_ref[...].astype(BF16), w_ref[...], preferred_element_type=F32)
    o_ref[...] = _deepnorm_ln(res_ref[...], y, g_ref[...], b_ref[...])


def matmul_ln(x, w, res, g, b, *, tm, name):
    m, k = x.shape
    n = w.shape[1]
    return pl.pallas_call(
        _mm_ln_kernel,
        grid=(m // tm,),
        in_specs=[pl.BlockSpec((tm, k), lambda i: (i, 0)),
                  pl.BlockSpec((k, n), lambda i: (0, 0)),
                  pl.BlockSpec((tm, n), lambda i: (i, 0)),
                  pl.BlockSpec((1, n), lambda i: (0, 0)),
                  pl.BlockSpec((1, n), lambda i: (0, 0))],
        out_specs=pl.BlockSpec((tm, n), lambda i: (i, 0)),
        out_shape=jax.ShapeDtypeStruct((m, n), F32),
        compiler_params=_cparams("arbitrary"),
        name=name,
    )(x, w, res, g.reshape(1, n), b.reshape(1, n))


def _moe_kernel(x_ref, comb_ref, w1_ref, w3_ref, w2_ref, g_ref, b_ref, o_ref, acc_ref, *, ne, nf):
    e = pl.program_id(1)
    f = pl.program_id(2)

    @pl.when((e == 0) & (f == 0))
    def _():
        acc_ref[...] = jnp.zeros_like(acc_ref)

    x = x_ref[...].astype(BF16)
    h1 = jnp.dot(x, w1_ref[0], preferred_element_type=F32)
    h3 = jnp.dot(x, w3_ref[0], preferred_element_type=F32)
    a = (jax.nn.silu(h1) * h3).astype(BF16)
    y = jnp.dot(a, w2_ref[0], preferred_element_type=F32)
    acc_ref[...] += comb_ref[0] * y

    @pl.when((e == ne - 1) & (f == nf - 1))
    def _():
        o_ref[...] = _deepnorm_ln(x_ref[...], acc_ref[...], g_ref[...], b_ref[...])


def moe_ln(x, comb, w1, w3, w2, g, b, *, tm, tf, name):
    m, d = x.shape
    ne, _, dff = w1.shape
    nf = dff // tf
    return pl.pallas_call(
        functools.partial(_moe_kernel, ne=ne, nf=nf),
        grid=(m // tm, ne, nf),
        in_specs=[pl.BlockSpec((tm, d), lambda i, e, f: (i, 0)),
                  pl.BlockSpec((1, tm, 1), lambda i, e, f: (e, i, 0)),
                  pl.BlockSpec((1, d, tf), lambda i, e, f: (e, 0, f)),
                  pl.BlockSpec((1, d, tf), lambda i, e, f: (e, 0, f)),
                  pl.BlockSpec((1, tf, d), lambda i, e, f: (e, f, 0)),
                  pl.BlockSpec((1, d), lambda i, e, f: (0, 0)),
                  pl.BlockSpec((1, d), lambda i, e, f: (0, 0))],
        out_specs=pl.BlockSpec((tm, d), lambda i, e, f: (i, 0)),
        out_shape=jax.ShapeDtypeStruct((m, d), F32),
        scratch_shapes=[pltpu.VMEM((tm, d), F32)],
        compiler_params=_cparams("arbitrary", "arbitrary", "arbitrary"),
        name=name,
    )(x, comb, w1, w3, w2, g.reshape(1, d), b.reshape(1, d))


def _router_kernel(x_ref, w_ref, b_ref, o_ref):
    logits = jnp.dot(x_ref[...], w_ref[...], preferred_element_type=F32,
                     precision=lax.Precision.HIGHEST) + b_ref[...]
    lane = lax.broadcasted_iota(jnp.int32, logits.shape, 1)
    big = jnp.int32(LANES)
    v1 = jnp.max(logits, axis=-1, keepdims=True)
    i1 = jnp.min(jnp.where(logits == v1, lane, big), axis=-1, keepdims=True)
    rest = jnp.where(lane == i1, NEG_INF, logits)
    v2 = jnp.max(rest, axis=-1, keepdims=True)
    i2 = jnp.min(jnp.where(rest == v2, lane, big), axis=-1, keepdims=True)
    e2 = jnp.exp(v2 - v1)
    den = 1.0 + e2
    o_ref[...] = jnp.where(lane == i1, 1.0 / den, 0.0) + jnp.where(lane == i2, e2 / den, 0.0)


def router(x, w_pad, b_pad, *, tm, name):
    m, d = x.shape
    return pl.pallas_call(
        _router_kernel,
        grid=(m // tm,),
        in_specs=[pl.BlockSpec((tm, d), lambda i: (i, 0)),
                  pl.BlockSpec((d, LANES), lambda i: (0, 0)),
                  pl.BlockSpec((1, LANES), lambda i: (0, 0))],
        out_specs=pl.BlockSpec((tm, LANES), lambda i: (i, 0)),
        out_shape=jax.ShapeDtypeStruct((m, LANES), F32),
        compiler_params=_cparams("arbitrary"),
        name=name,
    )(x, w_pad, b_pad)


def _xattn_kernel(q_ref, mk_ref, mv_ref, o_ref):
    scale = X_HEAD_DIM ** -0.5
    for h in range(X_HEADS):
        sl = slice(h * X_HEAD_DIM, (h + 1) * X_HEAD_DIM)
        q = q_ref[:, sl]
        k = mk_ref[0, :, sl].astype(BF16)
        v = mv_ref[0, :, sl].astype(BF16)
        s = lax.dot_general(q, k, NT_DIMS, preferred_element_type=F32) * scale
        m = jnp.max(s, axis=-1, keepdims=True)
        p = jnp.exp(s - m)
        l = jnp.sum(p, axis=-1, keepdims=True)
        o = jnp.dot(p.astype(BF16), v, preferred_element_type=F32) / l
        o_ref[:, sl] = o.astype(o_ref.dtype)


def cross_attention(q, mem_k, mem_v, *, row0, rows, tq, name):
    d = q.shape[1]
    nb = mem_k.shape[0]
    per_b = rows // nb
    nq = per_b // tq
    base = row0 // tq
    return pl.pallas_call(
        _xattn_kernel,
        grid=(nb, nq),
        in_specs=[pl.BlockSpec((tq, d), lambda b, i: (base + b * nq + i, 0)),
                  pl.BlockSpec((1, N_MEM, d), lambda b, i: (b, 0, 0)),
                  pl.BlockSpec((1, N_MEM, d), lambda b, i: (b, 0, 0))],
        out_specs=pl.BlockSpec((tq, d), lambda b, i: (b * nq + i, 0)),
        out_shape=jax.ShapeDtypeStruct((rows, d), BF16),
        compiler_params=_cparams("arbitrary", "arbitrary"),
        name=name,
    )(q, mem_k, mem_v)


A_BAND_PAD = A_BAND + CHUNK
A_BLOCK = A_WINDOW


def _head_lane_masks():
    lane = lax.broadcasted_iota(jnp.int32, (1, LANES), 1)
    return [(lane >= a * HEAD_DIM) & (lane < (a + 1) * HEAD_DIM) for a in range(PAIR)]


def _band_prompt_kernel(q_ref, kp_ref, kc_ref, vp_ref, vc_ref, bias_ref, o_ref):
    i = pl.program_id(1)
    scale = HEAD_DIM ** -0.5
    pad = jnp.zeros((CHUNK, LANES), BF16)
    kk = jnp.concatenate([kp_ref[...].astype(BF16), kc_ref[...].astype(BF16), pad], axis=0)
    vv = jnp.concatenate([vp_ref[...].astype(BF16), vc_ref[...].astype(BF16), pad], axis=0)
    masks = _head_lane_masks()
    col = lax.broadcasted_iota(jnp.int32, (1, A_BAND_PAD), 1)
    for c in range(A_BLOCK // CHUNK):
        q_c = q_ref[c * CHUNK:(c + 1) * CHUNK, :].astype(BF16)
        kwin = kk[c * CHUNK:c * CHUNK + A_BAND_PAD]
        vwin = vv[c * CHUNK:c * CHUNK + A_BAND_PAD]
        valid = (i * A_BLOCK + c * CHUNK - A_WINDOW + col) >= 0
        o_c = None
        for a in range(PAIR):
            qm = jnp.where(masks[a], q_c, jnp.zeros_like(q_c))
            s = lax.dot_general(qm, kwin, NT_DIMS, preferred_element_type=F32) * scale
            s = jnp.where(valid, s + bias_ref[a], NEG_INF)
            m = jnp.max(s, axis=-1, keepdims=True)
            p = jnp.exp(s - m)
            l = jnp.sum(p, axis=-1, keepdims=True)
            o = jnp.dot(p.astype(BF16), vwin, preferred_element_type=F32) / l
            o_c = o if o_c is None else jnp.where(masks[a], o, o_c)
        o_ref[c * CHUNK:(c + 1) * CHUNK, :] = o_c.astype(o_ref.dtype)


def band_attention_prompt(h, bias, *, seq, name):
    nblk = seq // A_BLOCK
    npair = A_WIDTH // LANES
    qcol, kcol, vcol = 0, npair, 2 * npair
    return pl.pallas_call(
        _band_prompt_kernel,
        grid=(npair, nblk),
        in_specs=[pl.BlockSpec((A_BLOCK, LANES), lambda p, i: (i, qcol + p)),
                  pl.BlockSpec((A_BLOCK, LANES), lambda p, i: (jnp.maximum(i - 1, 0), kcol + p)),
                  pl.BlockSpec((A_BLOCK, LANES), lambda p, i: (i, kcol + p)),
                  pl.BlockSpec((A_BLOCK, LANES), lambda p, i: (jnp.maximum(i - 1, 0), vcol + p)),
                  pl.BlockSpec((A_BLOCK, LANES), lambda p, i: (i, vcol + p)),
                  pl.BlockSpec((PAIR, CHUNK, A_BAND_PAD), lambda p, i: (p, 0, 0))],
        out_specs=pl.BlockSpec((A_BLOCK, LANES), lambda p, i: (i, p)),
        out_shape=jax.ShapeDtypeStruct((seq, A_WIDTH), BF16),
        compiler_params=_cparams("arbitrary", "arbitrary"),
        name=name,
    )(h, h, h, h, h, bias)


def _head_rows(x, nheads):
    t, w = x.shape
    x3 = jnp.broadcast_to(x[None], (nheads, t, w))
    hh = lax.broadcasted_iota(jnp.int32, (nheads, 1, w), 0)
    cc = lax.broadcasted_iota(jnp.int32, (nheads, 1, w), 2) // HEAD_DIM
    return jnp.where(hh == cc, x3, jnp.zeros_like(x3)).reshape(nheads * t, w)


def _head_diag(r, nheads, t):
    w = r.shape[1]
    r3 = r.reshape(nheads, t, w)
    hh = lax.broadcasted_iota(jnp.int32, (nheads, 1, w), 0)
    cc = lax.broadcasted_iota(jnp.int32, (nheads, 1, w), 2) // HEAD_DIM
    return jnp.sum(jnp.where(hh == cc, r3, 0.0), axis=0)


def _band_sample_kernel(q_ref, kn_ref, vn_ref, kc_ref, vc_ref, bc_ref, bn_ref, o_ref):
    scale = HEAD_DIM ** -0.5
    t = q_ref.shape[0]
    qr = _head_rows(q_ref[...], A_HEADS).astype(BF16)
    kc = kc_ref[0].astype(BF16)
    vc = vc_ref[0].astype(BF16)
    kn = kn_ref[...].astype(BF16)
    vn = vn_ref[...].astype(BF16)
    s_c = lax.dot_general(qr, kc, NT_DIMS, preferred_element_type=F32) * scale + bc_ref[...]
    s_n = lax.dot_general(qr, kn, NT_DIMS, preferred_element_type=F32) * scale + bn_ref[...]
    m = jnp.maximum(jnp.max(s_c, axis=-1, keepdims=True), jnp.max(s_n, axis=-1, keepdims=True))
    p_c = jnp.exp(s_c - m)
    p_n = jnp.exp(s_n - m)
    l = jnp.sum(p_c, axis=-1, keepdims=True) + jnp.sum(p_n, axis=-1, keepdims=True)
    r = (jnp.dot(p_c.astype(BF16), vc, preferred_element_type=F32)
         + jnp.dot(p_n.astype(BF16), vn, preferred_element_type=F32)) / l
    o_ref[...] = _head_diag(r, A_HEADS, t).astype(o_ref.dtype)


def band_attention_sample(h, k_cache, v_cache, bias_c, bias_n, *, row0, nb, t, name):
    base = row0 // t
    w = k_cache.shape[1]
    return pl.pallas_call(
        _band_sample_kernel,
        grid=(nb,),
        in_specs=[pl.BlockSpec((t, A_WIDTH), lambda b: (base + b, 0)),
                  pl.BlockSpec((t, A_WIDTH), lambda b: (base + b, 1)),
                  pl.BlockSpec((t, A_WIDTH), lambda b: (base + b, 2)),
                  pl.BlockSpec((1, w, A_WIDTH), lambda b: (b, 0, 0)),
                  pl.BlockSpec((1, w, A_WIDTH), lambda b: (b, 0, 0)),
                  pl.BlockSpec((A_HEADS * t, w), lambda b: (0, 0)),
                  pl.BlockSpec((A_HEADS * t, t), lambda b: (0, 0))],
        out_specs=pl.BlockSpec((t, A_WIDTH), lambda b: (b, 0)),
        out_shape=jax.ShapeDtypeStruct((nb * t, A_WIDTH), BF16),
        compiler_params=_cparams("arbitrary"),
        name=name,
    )(h, h, h, k_cache, v_cache, bias_c, bias_n)


POOL_HALO = POOL_STATE + 1


def _pool_core(ext_ref, rows, pos0, pw_ref, sc_ref):
    pos = (pos0 + lax.broadcasted_iota(jnp.int32, (rows, 1), 0)).astype(F32)
    outs = []
    for g, w in enumerate(POOL_WINDOWS):
        sl = slice(g * B_GROUP, (g + 1) * B_GROUP)
        cur = ext_ref[POOL_HALO:POOL_HALO + rows, sl]
        win = cur
        for r in range(1, w):
            win = win + ext_ref[POOL_HALO - r:POOL_HALO - r + rows, sl]
        cnt = jnp.minimum(float(w), pos + 1.0)
        d = (win / cnt - cur).astype(BF16)
        outs.append(jnp.dot(d, pw_ref[g], preferred_element_type=F32))
    return jnp.concatenate(outs, axis=-1) * sc_ref[...]


def _pool_prompt_kernel(up_ref, uc_ref, pw_ref, sc_ref, o_ref, ext_ref, *, tm):
    i = pl.program_id(0)
    ext_ref[0:POOL_HALO, :] = jnp.where(i > 0, up_ref[...], 0.0)
    ext_ref[POOL_HALO:POOL_HALO + tm, :] = uc_ref[...]
    o_ref[...] = _pool_core(ext_ref, tm, i * tm, pw_ref, sc_ref).astype(o_ref.dtype)


def pool_prompt(h, pool_w, pool_scale, *, seq, tm, name):
    ucol = 3 * A_WIDTH // B_WIDTH
    per = tm // POOL_HALO
    return pl.pallas_call(
        functools.partial(_pool_prompt_kernel, tm=tm),
        grid=(seq // tm,),
        in_specs=[pl.BlockSpec((POOL_HALO, B_WIDTH), lambda i: (jnp.maximum(i * per - 1, 0), ucol)),
                  pl.BlockSpec((tm, B_WIDTH), lambda i: (i, ucol)),
                  pl.BlockSpec((len(POOL_WINDOWS), B_GROUP, B_GROUP), lambda i: (0, 0, 0)),
                  pl.BlockSpec((1, B_WIDTH), lambda i: (0, 0))],
        out_specs=pl.BlockSpec((tm, B_WIDTH), lambda i: (i, 0)),
        out_shape=jax.ShapeDtypeStruct((seq, B_WIDTH), BF16),
        scratch_shapes=[pltpu.VMEM((POOL_HALO + tm, B_WIDTH), F32)],
        compiler_params=_cparams("arbitrary"),
        name=name,
    )(h, h, pool_w, pool_scale.reshape(1, B_WIDTH))


def _pool_sample_kernel(hist_ref, u_ref, pw_ref, sc_ref, o_ref, ext_ref, *, t, pos0):
    ext_ref[0:POOL_HALO, :] = hist_ref[0]
    ext_ref[POOL_HALO:POOL_HALO + t, :] = u_ref[...]
    o_ref[...] = _pool_core(ext_ref, t, pos0, pw_ref, sc_ref).astype(o_ref.dtype)


def pool_sample(h, hist, pool_w, pool_scale, *, row0, nb, t, pos0, name):
    ucol = 3 * A_WIDTH // B_WIDTH
    base = row0 // t
    return pl.pallas_call(
        functools.partial(_pool_sample_kernel, t=t, pos0=pos0),
        grid=(nb,),
        in_specs=[pl.BlockSpec((1, POOL_HALO, B_WIDTH), lambda b: (b, 0, 0)),
                  pl.BlockSpec((t, B_WIDTH), lambda b: (base + b, ucol)),
                  pl.BlockSpec((len(POOL_WINDOWS), B_GROUP, B_GROUP), lambda b: (0, 0, 0)),
                  pl.BlockSpec((1, B_WIDTH), lambda b: (0, 0))],
        out_specs=pl.BlockSpec((t, B_WIDTH), lambda b: (b, 0)),
        out_shape=jax.ShapeDtypeStruct((nb * t, B_WIDTH), BF16),
        scratch_shapes=[pltpu.VMEM((POOL_HALO + t, B_WIDTH), F32)],
        compiler_params=_cparams("arbitrary"),
        name=name,
    )(hist, h, pool_w, pool_scale.reshape(1, B_WIDTH))


CUMSUM_ALIGN = 8 * LANES


def _cumsum_kernel(x_ref, o_ref, *, nh, nblk):
    r = lax.broadcasted_iota(jnp.int32, (LANES, LANES), 0)
    c = lax.broadcasted_iota(jnp.int32, (LANES, LANES), 1)
    upper = (r <= c).astype(F32)
    rb = lax.broadcasted_iota(jnp.int32, (nblk, nblk), 0)
    cb = lax.broadcasted_iota(jnp.int32, (nblk, nblk), 1)
    before = (cb < rb).astype(F32)
    for h in range(nh):
        rows = slice(h * nblk, (h + 1) * nblk)
        local = jnp.dot(x_ref[0, rows, :], upper, preferred_element_type=F32, precision=lax.Precision.HIGHEST)
        tot = jnp.broadcast_to(local[:, LANES - 1:LANES], (nblk, LANES))
        o_ref[0, rows, :] = local + jnp.dot(before, tot, preferred_element_type=F32,
                                            precision=lax.Precision.HIGHEST)


def cumsum_lanes(x, *, name):
    nb, nh, ln = x.shape
    nblk = ln // LANES
    out = pl.pallas_call(
        functools.partial(_cumsum_kernel, nh=nh, nblk=nblk),
        grid=(nb,),
        in_specs=[pl.BlockSpec((1, nh * nblk, LANES), lambda b: (b, 0, 0))],
        out_specs=pl.BlockSpec((1, nh * nblk, LANES), lambda b: (b, 0, 0)),
        out_shape=jax.ShapeDtypeStruct((nb, nh * nblk, LANES), F32),
        compiler_params=_cparams("arbitrary"),
        name=name,
    )(x.reshape(nb, nh * nblk, LANES))
    return out.reshape(nb, nh, ln)


def _fa_prompt_kernel(q_ref, k_ref, v_ref, dk_ref, dq_ref, o_ref, *, tq):
    qi = pl.program_id(1)
    scale = HEAD_DIM ** -0.5
    masks = _head_lane_masks()
    q = q_ref[...]
    row = lax.broadcasted_iota(jnp.int32, (tq, tq), 0)
    colm = lax.broadcasted_iota(jnp.int32, (tq, tq), 1)
    causal = colm <= row
    out = None
    for a in range(PAIR):
        qm = jnp.where(masks[a], q, jnp.zeros_like(q))
        dref = dq_ref[a, :, 0:1]

        def scores(j):
            off = pl.multiple_of(j * tq, tq)
            k = k_ref[pl.ds(off, tq), :]
            v = v_ref[pl.ds(off, tq), :]
            kb = dref - dk_ref[a, :, pl.ds(off, tq)]
            s = lax.dot_general(qm, k, NT_DIMS, preferred_element_type=F32) * scale + kb
            return s, v

        def update(s, v, carry):
            m, l, acc = carry
            m_new = jnp.maximum(m, jnp.max(s, axis=-1, keepdims=True))
            alpha = jnp.exp(m - m_new)
            p = jnp.exp(s - m_new)
            l = alpha * l + jnp.sum(p, axis=-1, keepdims=True)
            acc = alpha * acc + jnp.dot(p.astype(BF16), v, preferred_element_type=F32)
            return m_new, l, acc

        def body(j, carry):
            s, v = scores(j)
            return update(s, v, carry)

        init = (jnp.full((tq, 1), NEG_INF, F32), jnp.zeros((tq, 1), F32), jnp.zeros((tq, LANES), F32))
        carry = lax.fori_loop(0, qi, body, init)
        s, v = scores(qi)
        s = jnp.where(causal, s, NEG_INF)
        m, l, acc = update(s, v, carry)
        o = acc / l
        out = o if out is None else jnp.where(masks[a], o, out)
    o_ref[...] = out.astype(o_ref.dtype)


def forgetting_attention_prompt(qkv, dcum, *, seq, tq, name):
    npair = C_WIDTH // LANES
    kcol, vcol = npair, 2 * npair
    return pl.pallas_call(
        functools.partial(_fa_prompt_kernel, tq=tq),
        grid=(npair, seq // tq),
        in_specs=[pl.BlockSpec((tq, LANES), lambda p, i: (i, p)),
                  pl.BlockSpec((seq, LANES), lambda p, i: (0, kcol + p)),
                  pl.BlockSpec((seq, LANES), lambda p, i: (0, vcol + p)),
                  pl.BlockSpec((PAIR, 1, seq), lambda p, i: (p, 0, 0)),
                  pl.BlockSpec((PAIR, 1, tq), lambda p, i: (p, 0, i))],
        out_specs=pl.BlockSpec((tq, LANES), lambda p, i: (i, p)),
        out_shape=jax.ShapeDtypeStruct((seq, C_WIDTH), BF16),
        compiler_params=_cparams("arbitrary", "arbitrary"),
        name=name,
    )(qkv, qkv, qkv, dcum, dcum)


def _fa_sample_kernel(q_ref, kn_ref, vn_ref, kc_ref, vc_ref, dc_ref, dp_ref, dn_ref, o_ref,
                      qr_ref, m_ref, l_ref, acc_ref, *, t, nkc):
    kc_i = pl.program_id(1)
    scale = HEAD_DIM ** -0.5
    rows = C_HEADS * t

    @pl.when(kc_i == 0)
    def _():
        qr_ref[...] = _head_rows(q_ref[...], C_HEADS)
        m_ref[...] = jnp.full(m_ref.shape, NEG_INF, F32)
        l_ref[...] = jnp.zeros(l_ref.shape, F32)
        acc_ref[...] = jnp.zeros(acc_ref.shape, F32)

    qr = qr_ref[...]
    d_tot = dp_ref[0][:, :, LANES - 1:LANES]

    def update(s, v):
        m = m_ref[...]
        m_new = jnp.maximum(m, jnp.max(s, axis=-1, keepdims=True))
        alpha = jnp.exp(m - m_new)
        p = jnp.exp(s - m_new)
        l_ref[...] = alpha * l_ref[...] + jnp.sum(p, axis=-1, keepdims=True)
        acc_ref[...] = alpha * acc_ref[...] + jnp.dot(p.astype(BF16), v, preferred_element_type=F32)
        m_ref[...] = m_new

    tk = kc_ref.shape[1]
    bias_c = jnp.broadcast_to(d_tot - dc_ref[0], (C_HEADS, t, tk)).reshape(rows, tk)
    s_c = lax.dot_general(qr, kc_ref[0].astype(BF16), NT_DIMS, preferred_element_type=F32) * scale + bias_c
    update(s_c, vc_ref[0].astype(BF16))

    @pl.when(kc_i == nkc - 1)
    def _():
        bias_n = jnp.broadcast_to(d_tot - dn_ref[0][:, :, 0:t], (C_HEADS, t, t)).reshape(rows, t)
        s_n = lax.dot_general(qr, kn_ref[...], NT_DIMS, preferred_element_type=F32) * scale + bias_n
        qpos = lax.broadcasted_iota(jnp.int32, (C_HEADS, t, t), 1).reshape(rows, t)
        kpos = lax.broadcasted_iota(jnp.int32, (C_HEADS, t, t), 2).reshape(rows, t)
        s_n = jnp.where(kpos <= qpos, s_n, NEG_INF)
        update(s_n, vn_ref[...])
        r = acc_ref[...] / l_ref[...]
        o_ref[...] = _head_diag(r, C_HEADS, t).astype(o_ref.dtype)


def forgetting_attention_sample(qkv, k_cache, v_cache, dall, *, row0, nb, t, tk, name):
    base = row0 // t
    past = k_cache.shape[1]
    nkc = past // tk
    rows = C_HEADS * t
    return pl.pallas_call(
        functools.partial(_fa_sample_kernel, t=t, nkc=nkc),
        grid=(nb, nkc),
        in_specs=[pl.BlockSpec((t, C_WIDTH), lambda b, c: (base + b, 0)),
                  pl.BlockSpec((t, C_WIDTH), lambda b, c: (base + b, 1)),
                  pl.BlockSpec((t, C_WIDTH), lambda b, c: (base + b, 2)),
                  pl.BlockSpec((1, tk, C_WIDTH), lambda b, c: (b, c, 0)),
                  pl.BlockSpec((1, tk, C_WIDTH), lambda b, c: (b, c, 0)),
                  pl.BlockSpec((1, C_HEADS, 1, tk), lambda b, c: (b, 0, 0, c)),
                  pl.BlockSpec((1, C_HEADS, 1, LANES), lambda b, c: (b, 0, 0, past // LANES - 1)),
                  pl.BlockSpec((1, C_HEADS, 1, LANES), lambda b, c: (b, 0, 0, past // LANES))],
        out_specs=pl.BlockSpec((t, C_WIDTH), lambda b, c: (b, 0)),
        out_shape=jax.ShapeDtypeStruct((nb * t, C_WIDTH), BF16),
        scratch_shapes=[pltpu.VMEM((rows, C_WIDTH), BF16),
                        pltpu.VMEM((rows, 1), F32),
                        pltpu.VMEM((rows, 1), F32),
                        pltpu.VMEM((rows, C_WIDTH), F32)],
        compiler_params=_cparams("arbitrary", "arbitrary"),
        name=name,
    )(qkv, qkv, qkv, k_cache, v_cache, dall, dall, dall)


def _rel_bias_matrix(rel_table, nq, nk):
    d = jnp.arange(nq)[:, None] - (jnp.arange(nk)[None, :] - A_WINDOW)
    return rel_table[:, jnp.clip(d, -REL_CLIP, REL_CLIP) + REL_CLIP].astype(F32)


def kernel(x_prompt, x_sample, cache_a_k, cache_a_v, state_pool, cache_c_k, cache_c_v, cache_c_logf,
           cache_mem_k, cache_mem_v, mem_prompt, w_in_ab, rel_bias_a, pool_w, pool_scale, w_out_ab,
           w_in_c, b_f, w_out_c, w_xq, w_xk, w_xv, w_xo, ln_g, ln_b, ffn_w1, ffn_w3, ffn_w2,
           w_router, b_router, moe_w1, moe_w3, moe_w2):
    bp, seq, d = x_prompt.shape
    nb, t, _ = x_sample.shape
    assert bp == 1 and d == D_MODEL
    ns = nb * t
    tot = seq + ns
    past = cache_c_k.shape[2]
    tm = 512
    assert tot % tm == 0 and seq % tm == 0

    x = jnp.concatenate([x_prompt.reshape(seq, d), x_sample.reshape(ns, d)], axis=0)

    mem = mem_prompt.reshape(N_MEM, d)
    p_mem_k, p_mem_v = [], []
    for layer in range(DEPTH):
        (mk,) = matmul(mem, w_xk[layer].astype(BF16), [F32], tm=N_MEM, tn=d, name=f"mem_k{layer}")
        (mv,) = matmul(mem, w_xv[layer].astype(BF16), [F32], tm=N_MEM, tn=d, name=f"mem_v{layer}")
        p_mem_k.append(mk)
        p_mem_v.append(mv)

    def cross_block(x, layer):
        (q,) = matmul(x, w_xq[layer].astype(BF16), [BF16], tm=tm, tn=d, name=f"xq{layer}")
        o_p = cross_attention(q, p_mem_k[layer].reshape(1, N_MEM, d), p_mem_v[layer].reshape(1, N_MEM, d),
                              row0=0, rows=seq, tq=tm, name=f"xattn_p{layer}")
        o_s = cross_attention(q, cache_mem_k[layer].reshape(nb, N_MEM, d), cache_mem_v[layer].reshape(nb, N_MEM, d),
                              row0=seq, rows=ns, tq=t, name=f"xattn_s{layer}")
        o = jnp.concatenate([o_p, o_s], axis=0)
        return matmul_ln(o, w_xo[layer].astype(BF16), x, ln_g[layer, 1], ln_b[layer, 1], tm=tm, name=f"xo{layer}")

    pr = 0
    (h,) = matmul(x, w_in_ab[pr].astype(BF16), [F32], tm=tm, tn=tm, name="in_ab")
    bias = _rel_bias_matrix(rel_bias_a[pr], CHUNK, A_BAND)
    bias_p = jnp.concatenate([bias, jnp.full((A_HEADS, CHUNK, A_BAND_PAD - A_BAND), NEG_INF, F32)], axis=-1)
    oa_p = band_attention_prompt(h, bias_p, seq=seq, name="band_p")
    bias_s = bias[:, :t, :A_WINDOW + t].reshape(A_HEADS * t, A_WINDOW + t)
    oa_s = band_attention_sample(h, cache_a_k[pr].reshape(nb, A_WINDOW, A_WIDTH),
                                 cache_a_v[pr].reshape(nb, A_WINDOW, A_WIDTH),
                                 bias_s[:, :A_WINDOW], bias_s[:, A_WINDOW:], row0=seq, nb=nb, t=t, name="band_s")
    pw = pool_w[pr].astype(BF16)
    ob_p = pool_prompt(h, pw, pool_scale[pr], seq=seq, tm=tm, name="pool_p")
    hist = jnp.concatenate([jnp.zeros((nb, 1, B_WIDTH), F32), state_pool[pr]], axis=1)
    ob_s = pool_sample(h, hist, pw, pool_scale[pr], row0=seq, nb=nb, t=t, pos0=past, name="pool_s")
    mix = jnp.concatenate([jnp.concatenate([oa_p, ob_p], axis=1), jnp.concatenate([oa_s, ob_s], axis=1)], axis=0)
    x = matmul_ln(mix, w_out_ab[pr].astype(BF16), x, ln_g[0, 0], ln_b[0, 0], tm=tm, name="out_ab")
    x = cross_block(x, 0)
    ones = jnp.ones((1, tot, 1), F32)
    x = moe_ln(x, ones, ffn_w1.astype(BF16), ffn_w3.astype(BF16), ffn_w2.astype(BF16),
               ln_g[0, 2], ln_b[0, 2], tm=tm, tf=D_FF // 2, name="ffn")

    k_a = h[:, A_WIDTH:2 * A_WIDTH]
    v_a = h[:, 2 * A_WIDTH:3 * A_WIDTH]
    u_b = h[:, 3 * A_WIDTH:]
    keep = min(A_WINDOW, seq)
    p_a_k = k_a[seq - keep:seq].reshape(1, 1, keep, A_HEADS, HEAD_DIM)
    p_a_v = v_a[seq - keep:seq].reshape(1, 1, keep, A_HEADS, HEAD_DIM)
    p_pool = u_b[seq - POOL_STATE:seq].reshape(1, 1, POOL_STATE, B_WIDTH)
    s_a_k = k_a[seq:].reshape(1, nb, t, A_HEADS, HEAD_DIM)
    s_a_v = v_a[seq:].reshape(1, nb, t, A_HEADS, HEAD_DIM)
    s_pool = jnp.concatenate([state_pool[pr], u_b[seq:].reshape(nb, t, B_WIDTH)], axis=1)[:, -POOL_STATE:][None]

    w_c = w_in_c[pr]
    qkv32, qkv16 = matmul(x, w_c[:, :3 * C_WIDTH].astype(BF16), [F32, BF16], tm=tm, tn=tm, name="in_c")
    w_f = jnp.pad(w_c[:, 3 * C_WIDTH:], ((0, 0), (0, LANES - C_HEADS))).astype(BF16)
    bf_pad = jnp.pad(b_f[pr].astype(F32), (0, LANES - C_HEADS)).reshape(1, LANES)
    logf = matmul_logsig(x, w_f, bf_pad, tm=tm, name="in_c_f")[:, :C_HEADS]
    lf_p = logf[:seq].T.reshape(1, C_HEADS, seq)
    dcum_p = cumsum_lanes(lf_p, name="cumsum_p").reshape(C_HEADS, 1, seq)
    oc_p = forgetting_attention_prompt(qkv16, dcum_p, seq=seq, tq=tm, name="fa_p")
    lf_s = logf[seq:].reshape(nb, t, C_HEADS)
    lf_all = jnp.concatenate([cache_c_logf[pr].astype(F32), lf_s,
                              jnp.zeros((nb, CUMSUM_ALIGN - t, C_HEADS), F32)], axis=1)
    dall = cumsum_lanes(jnp.swapaxes(lf_all, 1, 2), name="cumsum_s").reshape(nb, C_HEADS, 1, past + CUMSUM_ALIGN)
    oc_s = forgetting_attention_sample(qkv16, cache_c_k[pr].reshape(nb, past, C_WIDTH),
                                       cache_c_v[pr].reshape(nb, past, C_WIDTH), dall,
                                       row0=seq, nb=nb, t=t, tk=1024, name="fa_s")
    oc = jnp.concatenate([oc_p, oc_s], axis=0)
    x = matmul_ln(oc, w_out_c[pr].astype(BF16), x, ln_g[1, 0], ln_b[1, 0], tm=tm, name="out_c")
    x = cross_block(x, 1)
    w_r = jnp.pad(w_router[pr], ((0, 0), (0, LANES - N_EXPERTS)))
    b_r = jnp.concatenate([b_router[pr].astype(F32), jnp.full((LANES - N_EXPERTS,), NEG_INF, F32)]).reshape(1, LANES)
    comb = router(x, w_r, b_r, tm=tm, name="router")
    comb_e = comb[:, :N_EXPERTS].T.reshape(N_EXPERTS, tot, 1)
    x = moe_ln(x, comb_e, moe_w1[pr].astype(BF16), moe_w3[pr].astype(BF16), moe_w2[pr].astype(BF16),
               ln_g[1, 2], ln_b[1, 2], tm=tm, tf=D_FF // 2, name="moe")

    k_c = qkv32[:, C_WIDTH:2 * C_WIDTH]
    v_c = qkv32[:, 2 * C_WIDTH:]
    p_c_k = k_c[:seq].reshape(1, 1, seq, C_HEADS, HEAD_DIM)
    p_c_v = v_c[:seq].reshape(1, 1, seq, C_HEADS, HEAD_DIM)
    p_c_logf = logf[:seq].reshape(1, 1, seq, C_HEADS)
    s_c_k = k_c[seq:].reshape(1, nb, t, C_HEADS, HEAD_DIM)
    s_c_v = v_c[seq:].reshape(1, nb, t, C_HEADS, HEAD_DIM)
    s_c_logf = lf_s[None]

    y_prompt = x[:seq].reshape(1, seq, d)
    y_sample = x[seq:].reshape(nb, t, d)
    pmk = jnp.stack(p_mem_k).reshape(DEPTH, 1, N_MEM, X_HEADS, X_HEAD_DIM)
    pmv = jnp.stack(p_mem_v).reshape(DEPTH, 1, N_MEM, X_HEADS, X_HEAD_DIM)
    return (y_prompt, y_sample, p_a_k, p_a_v, p_pool, p_c_k, p_c_v, p_c_logf, pmk, pmv,
            s_a_k, s_a_v, s_pool, s_c_k, s_c_v, s_c_logf)
```

```python
import functools

import jax
import jax.numpy as jnp
from jax import lax
from jax.experimental import pallas as pl
from jax.experimental.pallas import tpu as pltpu

F32 = jnp.float32
BF16 = jnp.bfloat16

D_MODEL = 1024
DEPTH = 2
CHUNK = 64
N_MEM = 256
HEAD_DIM = 64
A_HEADS = 8
A_WIDTH = A_HEADS * HEAD_DIM
A_PREV_CHUNKS = 8
A_WINDOW = A_PREV_CHUNKS * CHUNK
A_BAND = (A_PREV_CHUNKS + 1) * CHUNK
REL_CLIP = 128
POOL_WINDOWS = (2, 4, 8, 16)
B_WIDTH = D_MODEL - A_WIDTH
B_GROUP = B_WIDTH // len(POOL_WINDOWS)
POOL_STATE = max(POOL_WINDOWS) - 1
C_HEADS = D_MODEL // HEAD_DIM
C_WIDTH = C_HEADS * HEAD_DIM
X_HEADS = 4
X_HEAD_DIM = D_MODEL // X_HEADS
D_FF = 2816
N_EXPERTS = 8
ALPHA = (2.0 * DEPTH) ** 0.25
LN_EPS = 1e-5
NEG_INF = -1e30

LANES = 128
PAIR = LANES // HEAD_DIM
VMEM_LIMIT = 56 * 1024 * 1024

NT_DIMS = (((1,), (1,)), ((), ()))


def _cparams(*sem):
    return pltpu.CompilerParams(dimension_semantics=sem, vmem_limit_bytes=VMEM_LIMIT)


def _deepnorm_ln(res, sub, g, b):
    z = ALPHA * res + sub
    mu = jnp.mean(z, axis=-1, keepdims=True)
    zc = z - mu
    var = jnp.mean(zc * zc, axis=-1, keepdims=True)
    return zc * lax.rsqrt(var + LN_EPS) * g + b


def _mm_kernel(x_ref, w_ref, *o_refs):
    y = jnp.dot(x_ref[...].astype(BF16), w_ref[...], preferred_element_type=F32)
    for o_ref in o_refs:
        o_ref[...] = y.astype(o_ref.dtype)


def matmul(x, w, out_dtypes, *, tm, tn, name):
    m, k = x.shape
    n = w.shape[1]
    outs = pl.pallas_call(
        _mm_kernel,
        grid=(n // tn, m // tm),
        in_specs=[pl.BlockSpec((tm, k), lambda j, i: (i, 0)),
                  pl.BlockSpec((k, tn), lambda j, i: (0, j))],
        out_specs=[pl.BlockSpec((tm, tn), lambda j, i: (i, j)) for _ in out_dtypes],
        out_shape=[jax.ShapeDtypeStruct((m, n), dt) for dt in out_dtypes],
        compiler_params=_cparams("arbitrary", "arbitrary"),
        name=name,
    )(x, w)
    return outs


def _mm_logsig_kernel(x_ref, w_ref, b_ref, o_ref):
    z = jnp.dot(x_ref[...].astype(BF16), w_ref[...], preferred_element_type=F32)
    o_ref[...] = jax.nn.log_sigmoid(z + b_ref[...])


def matmul_logsig(x, w, b, *, tm, name):
    m, k = x.shape
    n = w.shape[1]
    return pl.pallas_call(
        _mm_logsig_kernel,
        grid=(m // tm,),
        in_specs=[pl.BlockSpec((tm, k), lambda i: (i, 0)),
                  pl.BlockSpec((k, n), lambda i: (0, 0)),
                  pl.BlockSpec((1, n), lambda i: (0, 0))],
        out_specs=pl.BlockSpec((tm, n), lambda i: (i, 0)),
        out_shape=jax.ShapeDtypeStruct((m, n), F32),
        compiler_params=_cparams("arbitrary"),
        name=name,
    )(x, w, b)


def _mm_ln_kernel(x_ref, w_ref, res_ref, g_ref, b_ref, o_ref):
    y = jnp.dot(x_ref[...].astype(BF16), w_ref[...], preferred_element_type=F32)
    o_ref[...] = _deepnorm_ln(res_ref[...], y, g_ref[...], b_ref[...])


def matmul_ln(x, w, res, g, b, *, tm, name):
    m, k = x.shape
    n = w.shape[1]
    return pl.pallas_call(
        _mm_ln_kernel,
        grid=(m // tm,),
        in_specs=[pl.BlockSpec((tm, k), lambda i: (i, 0)),
                  pl.BlockSpec((k, n), lambda i: (0, 0)),
                  pl.BlockSpec((tm, n), lambda i: (i, 0)),
                  pl.BlockSpec((1, n), lambda i: (0, 0)),
                  pl.BlockSpec((1, n), lambda i: (0, 0))],
        out_specs=pl.BlockSpec((tm, n), lambda i: (i, 0)),
        out_shape=jax.ShapeDtypeStruct((m, n), F32),
        compiler_params=_cparams("arbitrary"),
        name=name,
    )(x, w, res, g.reshape(1, n), b.reshape(1, n))


def _moe_kernel(x_ref, comb_ref, w1_ref, w3_ref, w2_ref, g_ref, b_ref, o_ref, acc_ref, *, ne, nf):
    e = pl.program_id(1)
    f = pl.program_id(2)

    @pl.when((e == 0) & (f == 0))
    def _():
        acc_ref[...] = jnp.zeros_like(acc_ref)

    x = x_ref[...].astype(BF16)
    h1 = jnp.dot(x, w1_ref[0], preferred_element_type=F32)
    h3 = jnp.dot(x, w3_ref[0], preferred_element_type=F32)
    a = (jax.nn.silu(h1) * h3).astype(BF16)
    y = jnp.dot(a, w2_ref[0], preferred_element_type=F32)
    acc_ref[...] += comb_ref[0] * y

    @pl.when((e == ne - 1) & (f == nf - 1))
    def _():
        o_ref[...] = _deepnorm_ln(x_ref[...], acc_ref[...], g_ref[...], b_ref[...])


def moe_ln(x, comb, w1, w3, w2, g, b, *, tm, tf, name):
    m, d = x.shape
    ne, _, dff = w1.shape
    nf = dff // tf
    return pl.pallas_call(
        functools.partial(_moe_kernel, ne=ne, nf=nf),
        grid=(m // tm, ne, nf),
        in_specs=[pl.BlockSpec((tm, d), lambda i, e, f: (i, 0)),
                  pl.BlockSpec((1, tm, 1), lambda i, e, f: (e, i, 0)),
                  pl.BlockSpec((1, d, tf), lambda i, e, f: (e, 0, f)),
                  pl.BlockSpec((1, d, tf), lambda i, e, f: (e, 0, f)),
                  pl.BlockSpec((1, tf, d), lambda i, e, f: (e, f, 0)),
                  pl.BlockSpec((1, d), lambda i, e, f: (0, 0)),
                  pl.BlockSpec((1, d), lambda i, e, f: (0, 0))],
        out_specs=pl.BlockSpec((tm, d), lambda i, e, f: (i, 0)),
        out_shape=jax.ShapeDtypeStruct((m, d), F32),
        scratch_shapes=[pltpu.VMEM((tm, d), F32)],
        compiler_params=_cparams("arbitrary", "arbitrary", "arbitrary"),
        name=name,
    )(x, comb, w1, w3, w2, g.reshape(1, d), b.reshape(1, d))


ROUTE_I1, ROUTE_I2, ROUTE_G1, ROUTE_G2, ROUTE_R1, ROUTE_R2 = range(6)
EXPERT_TILE = 256


def _router_kernel(x_ref, w_ref, b_ref, o_ref, cnt_ref, carry_ref):
    i = pl.program_id(0)

    @pl.when(i == 0)
    def _():
        carry_ref[...] = jnp.zeros_like(carry_ref)

    tm = x_ref.shape[0]
    logits = jnp.dot(x_ref[...], w_ref[...], preferred_element_type=F32,
                     precision=lax.Precision.HIGHEST) + b_ref[...]
    lane = lax.broadcasted_iota(jnp.int32, logits.shape, 1)
    big = jnp.int32(LANES)
    v1 = jnp.max(logits, axis=-1, keepdims=True)
    i1 = jnp.min(jnp.where(logits == v1, lane, big), axis=-1, keepdims=True)
    rest = jnp.where(lane == i1, NEG_INF, logits)
    v2 = jnp.max(rest, axis=-1, keepdims=True)
    i2 = jnp.min(jnp.where(rest == v2, lane, big), axis=-1, keepdims=True)
    e2 = jnp.exp(v2 - v1)
    den = 1.0 + e2
    oh1 = lane == i1
    oh2 = lane == i2
    cnt = jnp.where(oh1 | oh2, 1.0, 0.0)
    r = lax.broadcasted_iota(jnp.int32, (tm, tm), 0)
    c = lax.broadcasted_iota(jnp.int32, (tm, tm), 1)
    earlier = jnp.where(c < r, 1.0, 0.0).astype(BF16)
    before = jnp.dot(earlier, cnt.astype(BF16), preferred_element_type=F32) + carry_ref[...]
    r1 = jnp.sum(jnp.where(oh1, before, 0.0), axis=-1, keepdims=True)
    r2 = jnp.sum(jnp.where(oh2, before, 0.0), axis=-1, keepdims=True)
    carry_ref[...] += jnp.sum(cnt, axis=0, keepdims=True)
    cnt_ref[...] = carry_ref[...]
    out = jnp.zeros(logits.shape, F32)
    for ln, val in ((ROUTE_I1, i1.astype(F32)), (ROUTE_I2, i2.astype(F32)), (ROUTE_G1, 1.0 / den),
                    (ROUTE_G2, e2 / den), (ROUTE_R1, r1), (ROUTE_R2, r2)):
        out = jnp.where(lane == ln, val, out)
    o_ref[...] = out


def router(x, w_pad, b_pad, *, tm, name):
    m, d = x.shape
    return pl.pallas_call(
        _router_kernel,
        grid=(m // tm,),
        in_specs=[pl.BlockSpec((tm, d), lambda i: (i, 0)),
                  pl.BlockSpec((d, LANES), lambda i: (0, 0)),
                  pl.BlockSpec((1, LANES), lambda i: (0, 0))],
        out_specs=[pl.BlockSpec((tm, LANES), lambda i: (i, 0)),
                   pl.BlockSpec((1, LANES), lambda i: (0, 0))],
        out_shape=[jax.ShapeDtypeStruct((m, LANES), F32), jax.ShapeDtypeStruct((1, LANES), F32)],
        scratch_shapes=[pltpu.VMEM((1, LANES), F32)],
        compiler_params=_cparams("arbitrary"),
        name=name,
    )(x, w_pad, b_pad)


def _gather_rows_start(src_hbm, idx_ref, n, dst, sem):
    def issue(r, carry):
        pltpu.make_async_copy(src_hbm.at[pl.ds(idx_ref[0, 0, r], 1)], dst.at[pl.ds(r, 1)], sem).start()
        return carry

    lax.fori_loop(0, n, issue, 0)


def _gather_rows_wait(src_hbm, n, dst, sem):
    pltpu.make_async_copy(src_hbm.at[pl.ds(0, n)], dst, sem).wait()


def _expert_kernel(te_ref, nt_ref, idx0_ref, idxn_ref, x_hbm, w1_ref, w3_ref, w2_ref, o_ref, xbuf, sem, *, tm, nf):
    m = pl.program_id(0)
    nt = nt_ref[0]
    slot = m % 2

    @pl.when(m == 0)
    def _():
        _gather_rows_start(x_hbm, idx0_ref, tm, xbuf.at[0], sem.at[0])

    @pl.when(m + 1 < nt)
    def _():
        _gather_rows_start(x_hbm, idxn_ref, tm, xbuf.at[1 - slot], sem.at[1 - slot])

    @pl.when(m < nt)
    def _():
        _gather_rows_wait(x_hbm, tm, xbuf.at[slot], sem.at[slot])
        x = xbuf[slot].astype(BF16)
        tf = w1_ref.shape[2] // nf
        y = None
        for f in range(nf):
            sl = slice(f * tf, (f + 1) * tf)
            h1 = jnp.dot(x, w1_ref[0, :, sl], preferred_element_type=F32)
            h3 = jnp.dot(x, w3_ref[0, :, sl], preferred_element_type=F32)
            a = (jax.nn.silu(h1) * h3).astype(BF16)
            part = jnp.dot(a, w2_ref[0, sl, :], preferred_element_type=F32)
            y = part if y is None else y + part
        o_ref[...] = y

    @pl.when(m >= nt)
    def _():
        o_ref[...] = jnp.zeros_like(o_ref)


def expert_tiles(x, w1, w3, w2, tile_expert, n_tiles, src_rows, *, tm, nf, name):
    m, d = x.shape
    ne, _, dff = w1.shape
    nt_max = src_rows.shape[0]

    def wmap(i, te, nt):
        return (te[jnp.minimum(i, nt[0] - 1)], 0, 0)

    grid_spec = pltpu.PrefetchScalarGridSpec(
        num_scalar_prefetch=2,
        grid=(nt_max,),
        in_specs=[pl.BlockSpec((1, 1, tm), lambda i, te, nt: (0, 0, 0), memory_space=pltpu.SMEM),
                  pl.BlockSpec((1, 1, tm), lambda i, te, nt: (jnp.minimum(i + 1, nt_max - 1), 0, 0),
                               memory_space=pltpu.SMEM),
                  pl.BlockSpec(memory_space=pl.ANY),
                  pl.BlockSpec((1, d, dff), wmap),
                  pl.BlockSpec((1, d, dff), wmap),
                  pl.BlockSpec((1, dff, d), wmap)],
        out_specs=pl.BlockSpec((tm, d), lambda i, te, nt: (i, 0)),
        scratch_shapes=[pltpu.VMEM((2, tm, d), F32), pltpu.SemaphoreType.DMA((2,))],
    )
    return pl.pallas_call(
        functools.partial(_expert_kernel, tm=tm, nf=nf),
        grid_spec=grid_spec,
        out_shape=jax.ShapeDtypeStruct((nt_max * tm, d), F32),
        compiler_params=_cparams("arbitrary"),
        name=name,
    )(tile_expert, n_tiles, src_rows, src_rows, x, w1, w3, w2)


def _combine_ln_kernel(idx0_ref, idxn_ref, y_hbm, x_ref, route_ref, g_ref, b_ref, o_ref, ybuf, sem, *, tm, nsteps):
    i = pl.program_id(0)
    slot = i % 2

    @pl.when(i == 0)
    def _():
        _gather_rows_start(y_hbm, idx0_ref, 2 * tm, ybuf.at[0], sem.at[0])

    @pl.when(i + 1 < nsteps)
    def _():
        _gather_rows_start(y_hbm, idxn_ref, 2 * tm, ybuf.at[1 - slot], sem.at[1 - slot])

    _gather_rows_wait(y_hbm, 2 * tm, ybuf.at[slot], sem.at[slot])
    route = route_ref[...]
    g1 = route[:, ROUTE_G1:ROUTE_G1 + 1]
    g2 = route[:, ROUTE_G2:ROUTE_G2 + 1]
    y = g1 * ybuf[slot, 0:tm, :] + g2 * ybuf[slot, tm:2 * tm, :]
    o_ref[...] = _deepnorm_ln(x_ref[...], y, g_ref[...], b_ref[...])


def combine_ln(x, route, y_sorted, pos_rows, g, b, *, tm, name):
    m, d = x.shape
    nsteps = m // tm
    return pl.pallas_call(
        functools.partial(_combine_ln_kernel, tm=tm, nsteps=nsteps),
        grid=(nsteps,),
        in_specs=[pl.BlockSpec((1, 1, 2 * tm), lambda i: (0, 0, 0), memory_space=pltpu.SMEM),
                  pl.BlockSpec((1, 1, 2 * tm), lambda i: (jnp.minimum(i + 1, nsteps - 1), 0, 0),
                               memory_space=pltpu.SMEM),
                  pl.BlockSpec(memory_space=pl.ANY),
                  pl.BlockSpec((tm, d), lambda i: (i, 0)),
                  pl.BlockSpec((tm, LANES), lambda i: (i, 0)),
                  pl.BlockSpec((1, d), lambda i: (0, 0)),
                  pl.BlockSpec((1, d), lambda i: (0, 0))],
        out_specs=pl.BlockSpec((tm, d), lambda i: (i, 0)),
        out_shape=jax.ShapeDtypeStruct((m, d), F32),
        scratch_shapes=[pltpu.VMEM((2, 2 * tm, d), F32), pltpu.SemaphoreType.DMA((2,))],
        compiler_params=_cparams("arbitrary"),
        name=name,
    )(pos_rows, pos_rows, y_sorted, x, route, g.reshape(1, d), b.reshape(1, d))


def moe_routed_ln(x, w_router_pad, b_router_pad, w1, w3, w2, g, b, *, tm, name):
    m, d = x.shape
    ne = w1.shape[0]
    te_rows = EXPERT_TILE
    route, counts = router(x, w_router_pad, b_router_pad, tm=tm, name=name + "_router")
    counts = counts[0, :ne].astype(jnp.int32)
    padded = (counts + te_rows - 1) // te_rows * te_rows
    ends = jnp.cumsum(padded)
    base = ends - padded
    i1 = route[:, ROUTE_I1].astype(jnp.int32)
    i2 = route[:, ROUTE_I2].astype(jnp.int32)
    pos1 = base[i1] + route[:, ROUTE_R1].astype(jnp.int32)
    pos2 = base[i2] + route[:, ROUTE_R2].astype(jnp.int32)
    nt_max = (2 * m + ne * (te_rows - 1)) // te_rows + 1
    tok = jnp.arange(m, dtype=jnp.int32)
    src = jnp.zeros((nt_max * te_rows,), jnp.int32).at[pos1].set(tok).at[pos2].set(tok)
    n_tiles = (ends[-1] // te_rows).astype(jnp.int32).reshape(1)
    tile_start = jnp.arange(nt_max, dtype=jnp.int32) * te_rows
    tile_expert = jnp.minimum(jnp.searchsorted(ends, tile_start, side="right"), ne - 1).astype(jnp.int32)
    y_sorted = expert_tiles(x, w1, w3, w2, tile_expert, n_tiles, src.reshape(nt_max, 1, te_rows),
                            tm=te_rows, nf=2, name=name + "_experts")
    pos_rows = jnp.concatenate([pos1.reshape(m // tm, 1, tm), pos2.reshape(m // tm, 1, tm)], axis=-1)
    return combine_ln(x, route, y_sorted, pos_rows, g, b, tm=tm, name=name + "_combine")


def _xattn_kernel(q_ref, mk_ref, mv_ref, o_ref):
    scale = X_HEAD_DIM ** -0.5
    for h in range(X_HEADS):
        sl = slice(h * X_HEAD_DIM, (h + 1) * X_HEAD_DIM)
        q = q_ref[:, sl]
        k = mk_ref[0, :, sl].astype(BF16)
        v = mv_ref[0, :, sl].astype(BF16)
        s = lax.dot_general(q, k, NT_DIMS, preferred_element_type=F32) * scale
        m = jnp.max(s, axis=-1, keepdims=True)
        p = jnp.exp(s - m)
        l = jnp.sum(p, axis=-1, keepdims=True)
        o = jnp.dot(p.astype(BF16), v, preferred_element_type=F32) / l
        o_ref[:, sl] = o.astype(o_ref.dtype)


def cross_attention(q, mem_k, mem_v, *, row0, rows, tq, name):
    d = q.shape[1]
    nb = mem_k.shape[0]
    per_b = rows // nb
    nq = per_b // tq
    base = row0 // tq
    return pl.pallas_call(
        _xattn_kernel,
        grid=(nb, nq),
        in_specs=[pl.BlockSpec((tq, d), lambda b, i: (base + b * nq + i, 0)),
                  pl.BlockSpec((1, N_MEM, d), lambda b, i: (b, 0, 0)),
                  pl.BlockSpec((1, N_MEM, d), lambda b, i: (b, 0, 0))],
        out_specs=pl.BlockSpec((tq, d), lambda b, i: (b * nq + i, 0)),
        out_shape=jax.ShapeDtypeStruct((rows, d), BF16),
        compiler_params=_cparams("arbitrary", "arbitrary"),
        name=name,
    )(q, mem_k, mem_v)


A_BAND_PAD = A_BAND + CHUNK
A_BLOCK = A_WINDOW


def _head_lane_masks():
    lane = lax.broadcasted_iota(jnp.int32, (1, LANES), 1)
    return [(lane >= a * HEAD_DIM) & (lane < (a + 1) * HEAD_DIM) for a in range(PAIR)]


def _band_prompt_kernel(q_ref, kp_ref, kc_ref, vp_ref, vc_ref, bias_ref, o_ref):
    i = pl.program_id(1)
    scale = HEAD_DIM ** -0.5
    pad = jnp.zeros((CHUNK, LANES), BF16)
    kk = jnp.concatenate([kp_ref[...].astype(BF16), kc_ref[...].astype(BF16), pad], axis=0)
    vv = jnp.concatenate([vp_ref[...].astype(BF16), vc_ref[...].astype(BF16), pad], axis=0)
    masks = _head_lane_masks()
    col = lax.broadcasted_iota(jnp.int32, (1, A_BAND_PAD), 1)
    for c in range(A_BLOCK // CHUNK):
        q_c = q_ref[c * CHUNK:(c + 1) * CHUNK, :].astype(BF16)
        kwin = kk[c * CHUNK:c * CHUNK + A_BAND_PAD]
        vwin = vv[c * CHUNK:c * CHUNK + A_BAND_PAD]
        valid = (i * A_BLOCK + c * CHUNK - A_WINDOW + col) >= 0
        o_c = None
        for a in range(PAIR):
            qm = jnp.where(masks[a], q_c, jnp.zeros_like(q_c))
            s = lax.dot_general(qm, kwin, NT_DIMS, preferred_element_type=F32) * scale
            s = jnp.where(valid, s + bias_ref[a], NEG_INF)
            m = jnp.max(s, axis=-1, keepdims=True)
            p = jnp.exp(s - m)
            l = jnp.sum(p, axis=-1, keepdims=True)
            o = jnp.dot(p.astype(BF16), vwin, preferred_element_type=F32) / l
            o_c = o if o_c is None else jnp.where(masks[a], o, o_c)
        o_ref[c * CHUNK:(c + 1) * CHUNK, :] = o_c.astype(o_ref.dtype)


def band_attention_prompt(h, bias, *, seq, name):
    nblk = seq // A_BLOCK
    npair = A_WIDTH // LANES
    qcol, kcol, vcol = 0, npair, 2 * npair
    return pl.pallas_call(
        _band_prompt_kernel,
        grid=(npair, nblk),
        in_specs=[pl.BlockSpec((A_BLOCK, LANES), lambda p, i: (i, qcol + p)),
                  pl.BlockSpec((A_BLOCK, LANES), lambda p, i: (jnp.maximum(i - 1, 0), kcol + p)),
                  pl.BlockSpec((A_BLOCK, LANES), lambda p, i: (i, kcol + p)),
                  pl.BlockSpec((A_BLOCK, LANES), lambda p, i: (jnp.maximum(i - 1, 0), vcol + p)),
                  pl.BlockSpec((A_BLOCK, LANES), lambda p, i: (i, vcol + p)),
                  pl.BlockSpec((PAIR, CHUNK, A_BAND_PAD), lambda p, i: (p, 0, 0))],
        out_specs=pl.BlockSpec((A_BLOCK, LANES), lambda p, i: (i, p)),
        out_shape=jax.ShapeDtypeStruct((seq, A_WIDTH), BF16),
        compiler_params=_cparams("arbitrary", "arbitrary"),
        name=name,
    )(h, h, h, h, h, bias)


def _head_rows(x, nheads):
    t, w = x.shape
    x3 = jnp.broadcast_to(x[None], (nheads, t, w))
    hh = lax.broadcasted_iota(jnp.int32, (nheads, 1, w), 0)
    cc = lax.broadcasted_iota(jnp.int32, (nheads, 1, w), 2) // HEAD_DIM
    return jnp.where(hh == cc, x3, jnp.zeros_like(x3)).reshape(nheads * t, w)


def _head_diag(r, nheads, t):
    w = r.shape[1]
    r3 = r.reshape(nheads, t, w)
    hh = lax.broadcasted_iota(jnp.int32, (nheads, 1, w), 0)
    cc = lax.broadcasted_iota(jnp.int32, (nheads, 1, w), 2) // HEAD_DIM
    return jnp.sum(jnp.where(hh == cc, r3, 0.0), axis=0)


def _band_sample_kernel(q_ref, kn_ref, vn_ref, kc_ref, vc_ref, bc_ref, bn_ref, o_ref):
    scale = HEAD_DIM ** -0.5
    t = q_ref.shape[0]
    qr = _head_rows(q_ref[...], A_HEADS).astype(BF16)
    kc = kc_ref[0].astype(BF16)
    vc = vc_ref[0].astype(BF16)
    kn = kn_ref[...].astype(BF16)
    vn = vn_ref[...].astype(BF16)
    s_c = lax.dot_general(qr, kc, NT_DIMS, preferred_element_type=F32) * scale + bc_ref[...]
    s_n = lax.dot_general(qr, kn, NT_DIMS, preferred_element_type=F32) * scale + bn_ref[...]
    m = jnp.maximum(jnp.max(s_c, axis=-1, keepdims=True), jnp.max(s_n, axis=-1, keepdims=True))
    p_c = jnp.exp(s_c - m)
    p_n = jnp.exp(s_n - m)
    l = jnp.sum(p_c, axis=-1, keepdims=True) + jnp.sum(p_n, axis=-1, keepdims=True)
    r = (jnp.dot(p_c.astype(BF16), vc, preferred_element_type=F32)
         + jnp.dot(p_n.astype(BF16), vn, preferred_element_type=F32)) / l
    o_ref[...] = _head_diag(r, A_HEADS, t).astype(o_ref.dtype)


def band_attention_sample(h, k_cache, v_cache, bias_c, bias_n, *, row0, nb, t, name):
    base = row0 // t
    w = k_cache.shape[1]
    return pl.pallas_call(
        _band_sample_kernel,
        grid=(nb,),
        in_specs=[pl.BlockSpec((t, A_WIDTH), lambda b: (base + b, 0)),
                  pl.BlockSpec((t, A_WIDTH), lambda b: (base + b, 1)),
                  pl.BlockSpec((t, A_WIDTH), lambda b: (base + b, 2)),
                  pl.BlockSpec((1, w, A_WIDTH), lambda b: (b, 0, 0)),
                  pl.BlockSpec((1, w, A_WIDTH), lambda b: (b, 0, 0)),
                  pl.BlockSpec((A_HEADS * t, w), lambda b: (0, 0)),
                  pl.BlockSpec((A_HEADS * t, t), lambda b: (0, 0))],
        out_specs=pl.BlockSpec((t, A_WIDTH), lambda b: (b, 0)),
        out_shape=jax.ShapeDtypeStruct((nb * t, A_WIDTH), BF16),
        compiler_params=_cparams("arbitrary"),
        name=name,
    )(h, h, h, k_cache, v_cache, bias_c, bias_n)


POOL_HALO = POOL_STATE + 1


def _pool_core(ext_ref, rows, pos0, pw_ref, sc_ref):
    pos = (pos0 + lax.broadcasted_iota(jnp.int32, (rows, 1), 0)).astype(F32)
    outs = []
    for g, w in enumerate(POOL_WINDOWS):
        sl = slice(g * B_GROUP, (g + 1) * B_GROUP)
        cur = ext_ref[POOL_HALO:POOL_HALO + rows, sl]
        win = cur
        for r in range(1, w):
            win = win + ext_ref[POOL_HALO - r:POOL_HALO - r + rows, sl]
        cnt = jnp.minimum(float(w), pos + 1.0)
        d = (win / cnt - cur).astype(BF16)
        outs.append(jnp.dot(d, pw_ref[g], preferred_element_type=F32))
    return jnp.concatenate(outs, axis=-1) * sc_ref[...]


def _pool_prompt_kernel(up_ref, uc_ref, pw_ref, sc_ref, o_ref, ext_ref, *, tm):
    i = pl.program_id(0)
    ext_ref[0:POOL_HALO, :] = jnp.where(i > 0, up_ref[...], 0.0)
    ext_ref[POOL_HALO:POOL_HALO + tm, :] = uc_ref[...]
    o_ref[...] = _pool_core(ext_ref, tm, i * tm, pw_ref, sc_ref).astype(o_ref.dtype)


def pool_prompt(h, pool_w, pool_scale, *, seq, tm, name):
    ucol = 3 * A_WIDTH // B_WIDTH
    per = tm // POOL_HALO
    return pl.pallas_call(
        functools.partial(_pool_prompt_kernel, tm=tm),
        grid=(seq // tm,),
        in_specs=[pl.BlockSpec((POOL_HALO, B_WIDTH), lambda i: (jnp.maximum(i * per - 1, 0), ucol)),
                  pl.BlockSpec((tm, B_WIDTH), lambda i: (i, ucol)),
                  pl.BlockSpec((len(POOL_WINDOWS), B_GROUP, B_GROUP), lambda i: (0, 0, 0)),
                  pl.BlockSpec((1, B_WIDTH), lambda i: (0, 0))],
        out_specs=pl.BlockSpec((tm, B_WIDTH), lambda i: (i, 0)),
        out_shape=jax.ShapeDtypeStruct((seq, B_WIDTH), BF16),
        scratch_shapes=[pltpu.VMEM((POOL_HALO + tm, B_WIDTH), F32)],
        compiler_params=_cparams("arbitrary"),
        name=name,
    )(h, h, pool_w, pool_scale.reshape(1, B_WIDTH))


def _pool_sample_kernel(hist_ref, u_ref, pw_ref, sc_ref, o_ref, ext_ref, *, t, pos0):
    ext_ref[0:POOL_HALO, :] = hist_ref[0]
    ext_ref[POOL_HALO:POOL_HALO + t, :] = u_ref[...]
    o_ref[...] = _pool_core(ext_ref, t, pos0, pw_ref, sc_ref).astype(o_ref.dtype)


def pool_sample(h, hist, pool_w, pool_scale, *, row0, nb, t, pos0, name):
    ucol = 3 * A_WIDTH // B_WIDTH
    base = row0 // t
    return pl.pallas_call(
        functools.partial(_pool_sample_kernel, t=t, pos0=pos0),
        grid=(nb,),
        in_specs=[pl.BlockSpec((1, POOL_HALO, B_WIDTH), lambda b: (b, 0, 0)),
                  pl.BlockSpec((t, B_WIDTH), lambda b: (base + b, ucol)),
                  pl.BlockSpec((len(POOL_WINDOWS), B_GROUP, B_GROUP), lambda b: (0, 0, 0)),
                  pl.BlockSpec((1, B_WIDTH), lambda b: (0, 0))],
        out_specs=pl.BlockSpec((t, B_WIDTH), lambda b: (b, 0)),
        out_shape=jax.ShapeDtypeStruct((nb * t, B_WIDTH), BF16),
        scratch_shapes=[pltpu.VMEM((POOL_HALO + t, B_WIDTH), F32)],
        compiler_params=_cparams("arbitrary"),
        name=name,
    )(hist, h, pool_w, pool_scale.reshape(1, B_WIDTH))


CUMSUM_ALIGN = 8 * LANES


def _cumsum_kernel(x_ref, o_ref, *, nh, nblk):
    r = lax.broadcasted_iota(jnp.int32, (LANES, LANES), 0)
    c = lax.broadcasted_iota(jnp.int32, (LANES, LANES), 1)
    upper = (r <= c).astype(F32)
    rb = lax.broadcasted_iota(jnp.int32, (nblk, nblk), 0)
    cb = lax.broadcasted_iota(jnp.int32, (nblk, nblk), 1)
    before = (cb < rb).astype(F32)
    for h in range(nh):
        rows = slice(h * nblk, (h + 1) * nblk)
        local = jnp.dot(x_ref[0, rows, :], upper, preferred_element_type=F32, precision=lax.Precision.HIGHEST)
        tot = jnp.broadcast_to(local[:, LANES - 1:LANES], (nblk, LANES))
        o_ref[0, rows, :] = local + jnp.dot(before, tot, preferred_element_type=F32,
                                            precision=lax.Precision.HIGHEST)


def cumsum_lanes(x, *, name):
    nb, nh, ln = x.shape
    nblk = ln // LANES
    out = pl.pallas_call(
        functools.partial(_cumsum_kernel, nh=nh, nblk=nblk),
        grid=(nb,),
        in_specs=[pl.BlockSpec((1, nh * nblk, LANES), lambda b: (b, 0, 0))],
        out_specs=pl.BlockSpec((1, nh * nblk, LANES), lambda b: (b, 0, 0)),
        out_shape=jax.ShapeDtypeStruct((nb, nh * nblk, LANES), F32),
        compiler_params=_cparams("arbitrary"),
        name=name,
    )(x.reshape(nb, nh * nblk, LANES))
    return out.reshape(nb, nh, ln)


def _cumsum_rows_kernel(x_ref, o_ref, offs_ref, *, nblk):
    r = lax.broadcasted_iota(jnp.int32, (LANES, LANES), 0)
    c = lax.broadcasted_iota(jnp.int32, (LANES, LANES), 1)
    lower = (c <= r).astype(F32)

    def local(b, carry):
        off = pl.multiple_of(b * LANES, LANES)
        o_ref[pl.ds(off, LANES), :] = jnp.dot(lower, x_ref[pl.ds(off, LANES), :], preferred_element_type=F32,
                                              precision=lax.Precision.HIGHEST)
        return carry

    lax.fori_loop(0, nblk, local, 0)
    totals = o_ref[pl.ds(LANES - 1, nblk, stride=LANES), :]
    rb = lax.broadcasted_iota(jnp.int32, (nblk, nblk), 0)
    cb = lax.broadcasted_iota(jnp.int32, (nblk, nblk), 1)
    offs_ref[...] = jnp.dot((cb < rb).astype(F32), totals, preferred_element_type=F32,
                            precision=lax.Precision.HIGHEST)

    def shift(b, carry):
        off = pl.multiple_of(b * LANES, LANES)
        o_ref[pl.ds(off, LANES), :] = o_ref[pl.ds(off, LANES), :] + offs_ref[pl.ds(b, 1), :]
        return carry

    lax.fori_loop(0, nblk, shift, 0)


def cumsum_rows(x, *, name):
    ln, nh = x.shape
    nblk = ln // LANES
    return pl.pallas_call(
        functools.partial(_cumsum_rows_kernel, nblk=nblk),
        grid=(1,),
        in_specs=[pl.BlockSpec((ln, nh), lambda i: (0, 0))],
        out_specs=pl.BlockSpec((ln, nh), lambda i: (0, 0)),
        out_shape=jax.ShapeDtypeStruct((ln, nh), F32),
        scratch_shapes=[pltpu.VMEM((nblk, nh), F32)],
        compiler_params=_cparams("arbitrary"),
        name=name,
    )(x)


LOG2E = 1.4426950408889634
FA_SPARE = HEAD_DIM


def _split3(x):
    hi = x.astype(BF16)
    r = x - hi.astype(F32)
    mid = r.astype(BF16)
    lo = (r - mid.astype(F32)).astype(BF16)
    return hi, mid, lo


def _spare_lane_columns(lane, base, first, second):
    out = jnp.zeros(jnp.broadcast_shapes(lane.shape, first[0].shape, second[0].shape), F32)
    for j, val in enumerate(tuple(first) + tuple(second)):
        out = jnp.where(lane == base + j, val.astype(F32), out)
    return out


def _fa_prep_kernel(k_ref, v_ref, d_ref, ka_ref, va_ref):
    lane = lax.broadcasted_iota(jnp.int32, (1, LANES), 1)
    row = lax.broadcasted_iota(jnp.int32, (LANES, 1), 0)
    masks = _head_lane_masks()
    d = d_ref[...] * (-LOG2E)
    one = jnp.ones((1, 1), BF16)
    for h in range(C_HEADS):
        p, a = divmod(h, PAIR)
        sl = slice(p * LANES, (p + 1) * LANES)
        ext = _spare_lane_columns(lane, HEAD_DIM * (1 - a), _split3(d[:, h:h + 1]), (one, one, one))
        ka_ref[h] = jnp.where(masks[a], k_ref[:, sl].astype(F32), ext).astype(BF16)
        vt = v_ref[:, sl].T
        mine = (row >= a * HEAD_DIM) & (row < (a + 1) * HEAD_DIM)
        va_ref[h] = jnp.where(mine, vt, 1.0).astype(BF16)


def fa_prep(qkv16, qkv32, dcum, *, seq, tr, name):
    kcol, vcol = 1, 2
    return pl.pallas_call(
        _fa_prep_kernel,
        grid=(seq // tr,),
        in_specs=[pl.BlockSpec((tr, C_WIDTH), lambda r: (r, kcol)),
                  pl.BlockSpec((tr, C_WIDTH), lambda r: (r, vcol)),
                  pl.BlockSpec((tr, C_HEADS), lambda r: (r, 0))],
        out_specs=[pl.BlockSpec((C_HEADS, tr, LANES), lambda r: (0, r, 0)),
                   pl.BlockSpec((C_HEADS, LANES, tr), lambda r: (0, 0, r))],
        out_shape=[jax.ShapeDtypeStruct((C_HEADS, seq, LANES), BF16),
                   jax.ShapeDtypeStruct((C_HEADS, LANES, seq), BF16)],
        compiler_params=_cparams("arbitrary"),
        name=name,
    )(qkv16, qkv32, dcum)


def _fa_prompt_kernel(q_ref, ka_ref, va_ref, d_ref, o_ref, m_ref, acc_ref, *, tq):
    p = pl.program_id(0)
    qi = pl.program_id(1)
    lane = lax.broadcasted_iota(jnp.int32, (1, LANES), 1)
    head_col = lax.broadcasted_iota(jnp.int32, (1, C_HEADS), 1)
    masks = _head_lane_masks()
    q = q_ref[...].astype(F32) * (HEAD_DIM ** -0.5 * LOG2E)
    d0 = d_ref[0:1, :] * LOG2E
    one = jnp.ones((1, 1), BF16)
    qa = []
    for a in range(PAIR):
        dref = jnp.sum(jnp.where(head_col == PAIR * p + a, d0, 0.0), axis=-1, keepdims=True)
        ext = _spare_lane_columns(lane, HEAD_DIM * (1 - a), (one, one, one), _split3(dref))
        qa.append(jnp.where(masks[a], q, ext).astype(BF16))

    m_ref[...] = jnp.full(m_ref.shape, NEG_INF, F32)
    acc_ref[...] = jnp.zeros(acc_ref.shape, F32)

    def scores(j, diagonal):
        off = pl.multiple_of(j * tq, tq)
        out = []
        for a in range(PAIR):
            st = lax.dot_general(ka_ref[a, pl.ds(off, tq), :], qa[a], NT_DIMS, preferred_element_type=F32)
            if diagonal:
                krow = lax.broadcasted_iota(jnp.int32, (tq, tq), 0)
                qcol = lax.broadcasted_iota(jnp.int32, (tq, tq), 1)
                st = jnp.where(krow <= qcol, st, NEG_INF)
            out.append(st)
        return tuple(out)

    def absorb(j, sts):
        off = pl.multiple_of(j * tq, tq)
        for a in range(PAIR):
            st = sts[a]
            m_old = m_ref[a]
            m_new = jnp.maximum(m_old, jnp.max(st, axis=0, keepdims=True))
            alpha = jnp.exp2(m_old - m_new)
            pt = jnp.exp2(st - m_new).astype(BF16)
            acc_ref[a] = alpha * acc_ref[a] + jnp.dot(va_ref[a, :, pl.ds(off, tq)], pt, preferred_element_type=F32)
            m_ref[a] = m_new

    def body(t, sts):
        nxt = scores(t, False)
        absorb(jnp.where(t == 0, qi, t - 1), sts)
        return nxt

    sts = lax.fori_loop(0, qi, body, scores(qi, True))
    absorb(jnp.where(qi == 0, qi, qi - 1), sts)

    row = lax.broadcasted_iota(jnp.int32, (LANES, 1), 0)
    out = None
    for a in range(PAIR):
        acc = acc_ref[a]
        lrow = HEAD_DIM * (1 - a)
        o = acc / acc[lrow:lrow + 1, :]
        out = o if out is None else jnp.where(row < HEAD_DIM, out, o)
    o_ref[...] = out.T.astype(o_ref.dtype)


def forgetting_attention_prompt(qkv16, ka, va, dcum, *, seq, tq, name):
    npair = C_WIDTH // LANES
    return pl.pallas_call(
        functools.partial(_fa_prompt_kernel, tq=tq),
        grid=(npair, seq // tq),
        in_specs=[pl.BlockSpec((tq, LANES), lambda p, i: (i, p)),
                  pl.BlockSpec((PAIR, seq, LANES), lambda p, i: (p, 0, 0)),
                  pl.BlockSpec((PAIR, LANES, seq), lambda p, i: (p, 0, 0)),
                  pl.BlockSpec((8, C_HEADS), lambda p, i: (i * (tq // 8), 0))],
        out_specs=pl.BlockSpec((tq, LANES), lambda p, i: (i, p)),
        out_shape=jax.ShapeDtypeStruct((seq, C_WIDTH), BF16),
        scratch_shapes=[pltpu.VMEM((PAIR, 1, tq), F32),
                        pltpu.VMEM((PAIR, LANES, tq), F32)],
        compiler_params=_cparams("arbitrary", "arbitrary"),
        name=name,
    )(qkv16, ka, va, dcum)


def _fa_sample_kernel(q_ref, kn_ref, vn_ref, kc_ref, vc_ref, dc_ref, dp_ref, dn_ref, o_ref,
                      qr_ref, m_ref, l_ref, acc_ref, *, t, nkc):
    kc_i = pl.program_id(1)
    scale = HEAD_DIM ** -0.5
    rows = C_HEADS * t

    @pl.when(kc_i == 0)
    def _():
        qr_ref[...] = _head_rows(q_ref[...], C_HEADS)
        m_ref[...] = jnp.full(m_ref.shape, NEG_INF, F32)
        l_ref[...] = jnp.zeros(l_ref.shape, F32)
        acc_ref[...] = jnp.zeros(acc_ref.shape, F32)

    qr = qr_ref[...]
    d_tot = dp_ref[0][:, :, LANES - 1:LANES]

    def update(s, v):
        m = m_ref[...]
        m_new = jnp.maximum(m, jnp.max(s, axis=-1, keepdims=True))
        alpha = jnp.exp(m - m_new)
        p = jnp.exp(s - m_new)
        l_ref[...] = alpha * l_ref[...] + jnp.sum(p, axis=-1, keepdims=True)
        acc_ref[...] = alpha * acc_ref[...] + jnp.dot(p.astype(BF16), v, preferred_element_type=F32)
        m_ref[...] = m_new

    tk = kc_ref.shape[1]
    bias_c = jnp.broadcast_to(d_tot - dc_ref[0], (C_HEADS, t, tk)).reshape(rows, tk)
    s_c = lax.dot_general(qr, kc_ref[0].astype(BF16), NT_DIMS, preferred_element_type=F32) * scale + bias_c
    update(s_c, vc_ref[0].astype(BF16))

    @pl.when(kc_i == nkc - 1)
    def _():
        bias_n = jnp.broadcast_to(d_tot - dn_ref[0][:, :, 0:t], (C_HEADS, t, t)).reshape(rows, t)
        s_n = lax.dot_general(qr, kn_ref[...], NT_DIMS, preferred_element_type=F32) * scale + bias_n
        qpos = lax.broadcasted_iota(jnp.int32, (C_HEADS, t, t), 1).reshape(rows, t)
        kpos = lax.broadcasted_iota(jnp.int32, (C_HEADS, t, t), 2).reshape(rows, t)
        s_n = jnp.where(kpos <= qpos, s_n, NEG_INF)
        update(s_n, vn_ref[...])
        r = acc_ref[...] / l_ref[...]
        o_ref[...] = _head_diag(r, C_HEADS, t).astype(o_ref.dtype)


def forgetting_attention_sample(qkv, k_cache, v_cache, dall, *, row0, nb, t, tk, name):
    base = row0 // t
    past = k_cache.shape[1]
    nkc = past // tk
    rows = C_HEADS * t
    return pl.pallas_call(
        functools.partial(_fa_sample_kernel, t=t, nkc=nkc),
        grid=(nb, nkc),
        in_specs=[pl.BlockSpec((t, C_WIDTH), lambda b, c: (base + b, 0)),
                  pl.BlockSpec((t, C_WIDTH), lambda b, c: (base + b, 1)),
                  pl.BlockSpec((t, C_WIDTH), lambda b, c: (base + b, 2)),
                  pl.BlockSpec((1, tk, C_WIDTH), lambda b, c: (b, c, 0)),
                  pl.BlockSpec((1, tk, C_WIDTH), lambda b, c: (b, c, 0)),
                  pl.BlockSpec((1, C_HEADS, 1, tk), lambda b, c: (b, 0, 0, c)),
                  pl.BlockSpec((1, C_HEADS, 1, LANES), lambda b, c: (b, 0, 0, past // LANES - 1)),
                  pl.BlockSpec((1, C_HEADS, 1, LANES), lambda b, c: (b, 0, 0, past // LANES))],
        out_specs=pl.BlockSpec((t, C_WIDTH), lambda b, c: (b, 0)),
        out_shape=jax.ShapeDtypeStruct((nb * t, C_WIDTH), BF16),
        scratch_shapes=[pltpu.VMEM((rows, C_WIDTH), BF16),
                        pltpu.VMEM((rows, 1), F32),
                        pltpu.VMEM((rows, 1), F32),
                        pltpu.VMEM((rows, C_WIDTH), F32)],
        compiler_params=_cparams("arbitrary", "arbitrary"),
        name=name,
    )(qkv, qkv, qkv, k_cache, v_cache, dall, dall, dall)


def _rel_bias_matrix(rel_table, nq, nk):
    d = jnp.arange(nq)[:, None] - (jnp.arange(nk)[None, :] - A_WINDOW)
    return rel_table[:, jnp.clip(d, -REL_CLIP, REL_CLIP) + REL_CLIP].astype(F32)


def kernel(x_prompt, x_sample, cache_a_k, cache_a_v, state_pool, cache_c_k, cache_c_v, cache_c_logf,
           cache_mem_k, cache_mem_v, mem_prompt, w_in_ab, rel_bias_a, pool_w, pool_scale, w_out_ab,
           w_in_c, b_f, w_out_c, w_xq, w_xk, w_xv, w_xo, ln_g, ln_b, ffn_w1, ffn_w3, ffn_w2,
           w_router, b_router, moe_w1, moe_w3, moe_w2):
    bp, seq, d = x_prompt.shape
    nb, t, _ = x_sample.shape
    assert bp == 1 and d == D_MODEL
    ns = nb * t
    tot = seq + ns
    past = cache_c_k.shape[2]
    tm = 512
    assert tot % tm == 0 and seq % tm == 0

    x = jnp.concatenate([x_prompt.reshape(seq, d), x_sample.reshape(ns, d)], axis=0)

    mem = mem_prompt.reshape(N_MEM, d)
    p_mem_k, p_mem_v = [], []
    for layer in range(DEPTH):
        (mk,) = matmul(mem, w_xk[layer].astype(BF16), [F32], tm=N_MEM, tn=d, name=f"mem_k{layer}")
        (mv,) = matmul(mem, w_xv[layer].astype(BF16), [F32], tm=N_MEM, tn=d, name=f"mem_v{layer}")
        p_mem_k.append(mk)
        p_mem_v.append(mv)

    def cross_block(x, layer):
        (q,) = matmul(x, w_xq[layer].astype(BF16), [BF16], tm=tm, tn=d, name=f"xq{layer}")
        o_p = cross_attention(q, p_mem_k[layer].reshape(1, N_MEM, d), p_mem_v[layer].reshape(1, N_MEM, d),
                              row0=0, rows=seq, tq=tm, name=f"xattn_p{layer}")
        o_s = cross_attention(q, cache_mem_k[layer].reshape(nb, N_MEM, d), cache_mem_v[layer].reshape(nb, N_MEM, d),
                              row0=seq, rows=ns, tq=t, name=f"xattn_s{layer}")
        o = jnp.concatenate([o_p, o_s], axis=0)
        return matmul_ln(o, w_xo[layer].astype(BF16), x, ln_g[layer, 1], ln_b[layer, 1], tm=tm, name=f"xo{layer}")

    pr = 0
    (h,) = matmul(x, w_in_ab[pr].astype(BF16), [F32], tm=tm, tn=tm, name="in_ab")
    bias = _rel_bias_matrix(rel_bias_a[pr], CHUNK, A_BAND)
    bias_p = jnp.concatenate([bias, jnp.full((A_HEADS, CHUNK, A_BAND_PAD - A_BAND), NEG_INF, F32)], axis=-1)
    oa_p = band_attention_prompt(h, bias_p, seq=seq, name="band_p")
    bias_s = bias[:, :t, :A_WINDOW + t].reshape(A_HEADS * t, A_WINDOW + t)
    oa_s = band_attention_sample(h, cache_a_k[pr].reshape(nb, A_WINDOW, A_WIDTH),
                                 cache_a_v[pr].reshape(nb, A_WINDOW, A_WIDTH),
                                 bias_s[:, :A_WINDOW], bias_s[:, A_WINDOW:], row0=seq, nb=nb, t=t, name="band_s")
    pw = pool_w[pr].astype(BF16)
    ob_p = pool_prompt(h, pw, pool_scale[pr], seq=seq, tm=tm, name="pool_p")
    hist = jnp.concatenate([jnp.zeros((nb, 1, B_WIDTH), F32), state_pool[pr]], axis=1)
    ob_s = pool_sample(h, hist, pw, pool_scale[pr], row0=seq, nb=nb, t=t, pos0=past, name="pool_s")
    mix = jnp.concatenate([jnp.concatenate([oa_p, ob_p], axis=1), jnp.concatenate([oa_s, ob_s], axis=1)], axis=0)
    x = matmul_ln(mix, w_out_ab[pr].astype(BF16), x, ln_g[0, 0], ln_b[0, 0], tm=tm, name="out_ab")
    x = cross_block(x, 0)
    ones = jnp.ones((1, tot, 1), F32)
    x = moe_ln(x, ones, ffn_w1.astype(BF16), ffn_w3.astype(BF16), ffn_w2.astype(BF16),
               ln_g[0, 2], ln_b[0, 2], tm=tm, tf=D_FF // 2, name="ffn")

    k_a = h[:, A_WIDTH:2 * A_WIDTH]
    v_a = h[:, 2 * A_WIDTH:3 * A_WIDTH]
    u_b = h[:, 3 * A_WIDTH:]
    keep = min(A_WINDOW, seq)
    p_a_k = k_a[seq - keep:seq].reshape(1, 1, keep, A_HEADS, HEAD_DIM)
    p_a_v = v_a[seq - keep:seq].reshape(1, 1, keep, A_HEADS, HEAD_DIM)
    p_pool = u_b[seq - POOL_STATE:seq].reshape(1, 1, POOL_STATE, B_WIDTH)
    s_a_k = k_a[seq:].reshape(1, nb, t, A_HEADS, HEAD_DIM)
    s_a_v = v_a[seq:].reshape(1, nb, t, A_HEADS, HEAD_DIM)
    s_pool = jnp.concatenate([state_pool[pr], u_b[seq:].reshape(nb, t, B_WIDTH)], axis=1)[:, -POOL_STATE:][None]

    w_c = w_in_c[pr]
    qkv32, qkv16 = matmul(x, w_c[:, :3 * C_WIDTH].astype(BF16), [F32, BF16], tm=tm, tn=tm, name="in_c")
    w_f = jnp.pad(w_c[:, 3 * C_WIDTH:], ((0, 0), (0, LANES - C_HEADS))).astype(BF16)
    bf_pad = jnp.pad(b_f[pr].astype(F32), (0, LANES - C_HEADS)).reshape(1, LANES)
    logf = matmul_logsig(x, w_f, bf_pad, tm=tm, name="in_c_f")[:, :C_HEADS]
    dcum_p = cumsum_rows(logf[:seq], name="cumsum_p")
    ka, va = fa_prep(qkv16, qkv32, dcum_p, seq=seq, tr=tm, name="fa_prep")
    oc_p = forgetting_attention_prompt(qkv16, ka, va, dcum_p, seq=seq, tq=tm, name="fa_p")
    lf_s = logf[seq:].reshape(nb, t, C_HEADS)
    lf_all = jnp.concatenate([cache_c_logf[pr].astype(F32), lf_s,
                              jnp.zeros((nb, CUMSUM_ALIGN - t, C_HEADS), F32)], axis=1)
    dall = cumsum_lanes(jnp.swapaxes(lf_all, 1, 2), name="cumsum_s").reshape(nb, C_HEADS, 1, past + CUMSUM_ALIGN)
    oc_s = forgetting_attention_sample(qkv16, cache_c_k[pr].reshape(nb, past, C_WIDTH),
                                       cache_c_v[pr].reshape(nb, past, C_WIDTH), dall,
                                       row0=seq, nb=nb, t=t, tk=1024, name="fa_s")
    oc = jnp.concatenate([oc_p, oc_s], axis=0)
    x = matmul_ln(oc, w_out_c[pr].astype(BF16), x, ln_g[1, 0], ln_b[1, 0], tm=tm, name="out_c")
    x = cross_block(x, 1)
    w_r = jnp.pad(w_router[pr], ((0, 0), (0, LANES - N_EXPERTS)))
    b_r = jnp.concatenate([b_router[pr].astype(F32), jnp.full((LANES - N_EXPERTS,), NEG_INF, F32)]).reshape(1, LANES)
    x = moe_routed_ln(x, w_r, b_r, moe_w1[pr].astype(BF16), moe_w3[pr].astype(BF16), moe_w2[pr].astype(BF16),
                      ln_g[1, 2], ln_b[1, 2], tm=tm, name="moe")

    k_c = qkv32[:, C_WIDTH:2 * C_WIDTH]
    v_c = qkv32[:, 2 * C_WIDTH:]
    p_c_k = k_c[:seq].reshape(1, 1, seq, C_HEADS, HEAD_DIM)
    p_c_v = v_c[:seq].reshape(1, 1, seq, C_HEADS, HEAD_DIM)
    p_c_logf = logf[:seq].reshape(1, 1, seq, C_HEADS)
    s_c_k = k_c[seq:].reshape(1, nb, t, C_HEADS, HEAD_DIM)
    s_c_v = v_c[seq:].reshape(1, nb, t, C_HEADS, HEAD_DIM)
    s_c_logf = lf_s[None]

    y_prompt = x[:seq].reshape(1, seq, d)
    y_sample = x[seq:].reshape(nb, t, d)
    pmk = jnp.stack(p_mem_k).reshape(DEPTH, 1, N_MEM, X_HEADS, X_HEAD_DIM)
    pmv = jnp.stack(p_mem_v).reshape(DEPTH, 1, N_MEM, X_HEADS, X_HEAD_DIM)
    return (y_prompt, y_sample, p_a_k, p_a_v, p_pool, p_c_k, p_c_v, p_c_logf, pmk, pmv,
            s_a_k, s_a_v, s_pool, s_c_k, s_c_v, s_c_logf)
```

```python
import functools

import jax
import jax.numpy as jnp
from jax import lax
from jax.experimental import pallas as pl
from jax.experimental.pallas import tpu as pltpu

F32 = jnp.float32
BF16 = jnp.bfloat16

D_MODEL = 1024
DEPTH = 2
CHUNK = 64
N_MEM = 256
HEAD_DIM = 64
A_HEADS = 8
A_WIDTH = A_HEADS * HEAD_DIM
A_PREV_CHUNKS = 8
A_WINDOW = A_PREV_CHUNKS * CHUNK
A_BAND = (A_PREV_CHUNKS + 1) * CHUNK
REL_CLIP = 128
POOL_WINDOWS = (2, 4, 8, 16)
B_WIDTH = D_MODEL - A_WIDTH
B_GROUP = B_WIDTH // len(POOL_WINDOWS)
POOL_STATE = max(POOL_WINDOWS) - 1
C_HEADS = D_MODEL // HEAD_DIM
C_WIDTH = C_HEADS * HEAD_DIM
X_HEADS = 4
X_HEAD_DIM = D_MODEL // X_HEADS
D_FF = 2816
N_EXPERTS = 8
ALPHA = (2.0 * DEPTH) ** 0.25
LN_EPS = 1e-5
NEG_INF = -1e30
LOG2E = 1.4426950408889634

LANES = 128
PAIR = LANES // HEAD_DIM
VMEM_LIMIT = 56 * 1024 * 1024

NT_DIMS = (((1,), (1,)), ((), ()))


def _cparams(*sem):
    return pltpu.CompilerParams(dimension_semantics=sem, vmem_limit_bytes=VMEM_LIMIT)


def _deepnorm_ln(res, sub, g, b):
    z = ALPHA * res + sub
    mu = jnp.mean(z, axis=-1, keepdims=True)
    zc = z - mu
    var = jnp.mean(zc * zc, axis=-1, keepdims=True)
    return zc * lax.rsqrt(var + LN_EPS) * g + b


def _mm_kernel(x_ref, w_ref, *o_refs):
    y = jnp.dot(x_ref[...].astype(BF16), w_ref[...], preferred_element_type=F32)
    for o_ref in o_refs:
        o_ref[...] = y.astype(o_ref.dtype)


def matmul(x, w, out_dtypes, *, tm, tn, name):
    m, k = x.shape
    n = w.shape[1]
    outs = pl.pallas_call(
        _mm_kernel,
        grid=(n // tn, m // tm),
        in_specs=[pl.BlockSpec((tm, k), lambda j, i: (i, 0)),
                  pl.BlockSpec((k, tn), lambda j, i: (0, j))],
        out_specs=[pl.BlockSpec((tm, tn), lambda j, i: (i, j)) for _ in out_dtypes],
        out_shape=[jax.ShapeDtypeStruct((m, n), dt) for dt in out_dtypes],
        compiler_params=_cparams("arbitrary", "arbitrary"),
        name=name,
    )(x, w)
    return outs


def _mm_qkv_kernel(x_ref, w_ref, qkv_ref, kp_ref, vp_ref, ks_ref, vs_ref, *, n_prompt):
    i = pl.program_id(0)
    y = jnp.dot(x_ref[...].astype(BF16), w_ref[...], preferred_element_type=F32)
    qkv_ref[...] = y.astype(BF16)
    k = y[:, C_WIDTH:2 * C_WIDTH]
    v = y[:, 2 * C_WIDTH:]

    @pl.when(i < n_prompt)
    def _():
        kp_ref[...] = k
        vp_ref[...] = v

    @pl.when(i >= n_prompt)
    def _():
        ks_ref[...] = k
        vs_ref[...] = v


def matmul_qkv(x, w, *, seq, tm, name):
    m, d = x.shape
    n = w.shape[1]
    n_prompt = seq // tm
    n_sample = (m - seq) // tm
    prompt_map = lambda i: (jnp.minimum(i, n_prompt - 1), 0)
    sample_map = lambda i: (jnp.maximum(i - n_prompt, 0), 0)
    return pl.pallas_call(
        functools.partial(_mm_qkv_kernel, n_prompt=n_prompt),
        grid=(n_prompt + n_sample,),
        in_specs=[pl.BlockSpec((tm, d), lambda i: (i, 0)),
                  pl.BlockSpec((d, n), lambda i: (0, 0))],
        out_specs=[pl.BlockSpec((tm, n), lambda i: (i, 0)),
                   pl.BlockSpec((tm, C_WIDTH), prompt_map), pl.BlockSpec((tm, C_WIDTH), prompt_map),
                   pl.BlockSpec((tm, C_WIDTH), sample_map), pl.BlockSpec((tm, C_WIDTH), sample_map)],
        out_shape=[jax.ShapeDtypeStruct((m, n), BF16),
                   jax.ShapeDtypeStruct((seq, C_WIDTH), F32), jax.ShapeDtypeStruct((seq, C_WIDTH), F32),
                   jax.ShapeDtypeStruct((m - seq, C_WIDTH), F32), jax.ShapeDtypeStruct((m - seq, C_WIDTH), F32)],
        compiler_params=_cparams("arbitrary"),
        name=name,
    )(x, w)


def _mm_logsig_kernel(x_ref, w_ref, b_ref, o_ref):
    z = jnp.dot(x_ref[...].astype(BF16), w_ref[...], preferred_element_type=F32)
    o_ref[...] = jax.nn.log_sigmoid(z + b_ref[...])


def matmul_logsig(x, w, b, *, tm, name):
    m, k = x.shape
    n = w.shape[1]
    return pl.pallas_call(
        _mm_logsig_kernel,
        grid=(m // tm,),
        in_specs=[pl.BlockSpec((tm, k), lambda i: (i, 0)),
                  pl.BlockSpec((k, n), lambda i: (0, 0)),
                  pl.BlockSpec((1, n), lambda i: (0, 0))],
        out_specs=pl.BlockSpec((tm, n), lambda i: (i, 0)),
        out_shape=jax.ShapeDtypeStruct((m, n), F32),
        compiler_params=_cparams("arbitrary"),
        name=name,
    )(x, w, b)


def _mm_ln_kernel(*refs, n_groups, n_first):
    x_refs = refs[:2 * n_groups]
    w_ref, res_ref, g_ref, b_ref, o_ref = refs[2 * n_groups:]
    i = pl.program_id(0)

    def run(xs):
        y, k0 = None, 0
        for x_ref in xs:
            kg = x_ref.shape[1]
            part = jnp.dot(x_ref[...].astype(BF16), w_ref[k0:k0 + kg, :], preferred_element_type=F32)
            y = part if y is None else y + part
            k0 += kg
        o_ref[...] = _deepnorm_ln(res_ref[...], y, g_ref[...], b_ref[...])

    @pl.when(i < n_first)
    def _():
        run(x_refs[0::2])

    @pl.when(i >= n_first)
    def _():
        run(x_refs[1::2])


def matmul_ln(groups, w, res, g, b, *, tm, name):
    m, n = res.shape
    split = groups[0][0].shape[0]
    n_first = split // tm
    first_map = lambda i: (jnp.minimum(i, n_first - 1), 0)
    rest_map = lambda i: (jnp.maximum(i - n_first, 0), 0)
    x_specs, x_args = [], []
    for xf, xr in groups:
        x_specs += [pl.BlockSpec((tm, xf.shape[1]), first_map), pl.BlockSpec((tm, xr.shape[1]), rest_map)]
        x_args += [xf, xr]
    return pl.pallas_call(
        functools.partial(_mm_ln_kernel, n_groups=len(groups), n_first=n_first),
        grid=(m // tm,),
        in_specs=x_specs + [pl.BlockSpec(w.shape, lambda i: (0, 0)),
                            pl.BlockSpec((tm, n), lambda i: (i, 0)),
                            pl.BlockSpec((1, n), lambda i: (0, 0)),
                            pl.BlockSpec((1, n), lambda i: (0, 0))],
        out_specs=pl.BlockSpec((tm, n), lambda i: (i, 0)),
        out_shape=jax.ShapeDtypeStruct((m, n), F32),
        compiler_params=_cparams("arbitrary"),
        name=name,
    )(*x_args, w, res, g.reshape(1, n), b.reshape(1, n))


def _moe_kernel(x_ref, comb_ref, w1_ref, w3_ref, w2_ref, g_ref, b_ref, o_ref, acc_ref, *, ne, nf):
    e = pl.program_id(1)
    f = pl.program_id(2)

    @pl.when((e == 0) & (f == 0))
    def _():
        acc_ref[...] = jnp.zeros_like(acc_ref)

    x = x_ref[...].astype(BF16)
    h1 = jnp.dot(x, w1_ref[0], preferred_element_type=F32)
    h3 = jnp.dot(x, w3_ref[0], preferred_element_type=F32)
    a = (jax.nn.silu(h1) * h3).astype(BF16)
    y = jnp.dot(a, w2_ref[0], preferred_element_type=F32)
    acc_ref[...] += comb_ref[0] * y

    @pl.when((e == ne - 1) & (f == nf - 1))
    def _():
        o_ref[...] = _deepnorm_ln(x_ref[...], acc_ref[...], g_ref[...], b_ref[...])


def moe_ln(x, comb, w1, w3, w2, g, b, *, tm, tf, name):
    m, d = x.shape
    ne, _, dff = w1.shape
    nf = dff // tf
    return pl.pallas_call(
        functools.partial(_moe_kernel, ne=ne, nf=nf),
        grid=(m // tm, ne, nf),
        in_specs=[pl.BlockSpec((tm, d), lambda i, e, f: (i, 0)),
                  pl.BlockSpec((1, tm, 1), lambda i, e, f: (e, i, 0)),
                  pl.BlockSpec((1, d, tf), lambda i, e, f: (e, 0, f)),
                  pl.BlockSpec((1, d, tf), lambda i, e, f: (e, 0, f)),
                  pl.BlockSpec((1, tf, d), lambda i, e, f: (e, f, 0)),
                  pl.BlockSpec((1, d), lambda i, e, f: (0, 0)),
                  pl.BlockSpec((1, d), lambda i, e, f: (0, 0))],
        out_specs=pl.BlockSpec((tm, d), lambda i, e, f: (i, 0)),
        out_shape=jax.ShapeDtypeStruct((m, d), F32),
        scratch_shapes=[pltpu.VMEM((tm, d), F32)],
        compiler_params=_cparams("arbitrary", "arbitrary", "arbitrary"),
        name=name,
    )(x, comb, w1, w3, w2, g.reshape(1, d), b.reshape(1, d))


ROUTE_I1, ROUTE_I2, ROUTE_G1, ROUTE_G2, ROUTE_R1, ROUTE_R2 = range(6)
EXPERT_TILE = 256


def _router_kernel(x_ref, w_ref, b_ref, o_ref, cnt_ref, carry_ref):
    i = pl.program_id(0)

    @pl.when(i == 0)
    def _():
        carry_ref[...] = jnp.zeros_like(carry_ref)

    tm = x_ref.shape[0]
    logits = jnp.dot(x_ref[...], w_ref[...], preferred_element_type=F32,
                     precision=lax.Precision.HIGHEST) + b_ref[...]
    lane = lax.broadcasted_iota(jnp.int32, logits.shape, 1)
    big = jnp.int32(LANES)
    v1 = jnp.max(logits, axis=-1, keepdims=True)
    i1 = jnp.min(jnp.where(logits == v1, lane, big), axis=-1, keepdims=True)
    rest = jnp.where(lane == i1, NEG_INF, logits)
    v2 = jnp.max(rest, axis=-1, keepdims=True)
    i2 = jnp.min(jnp.where(rest == v2, lane, big), axis=-1, keepdims=True)
    e2 = jnp.exp(v2 - v1)
    den = 1.0 + e2
    oh1 = lane == i1
    oh2 = lane == i2
    cnt = jnp.where(oh1 | oh2, 1.0, 0.0)
    r = lax.broadcasted_iota(jnp.int32, (tm, tm), 0)
    c = lax.broadcasted_iota(jnp.int32, (tm, tm), 1)
    earlier = jnp.where(c < r, 1.0, 0.0).astype(BF16)
    before = jnp.dot(earlier, cnt.astype(BF16), preferred_element_type=F32) + carry_ref[...]
    r1 = jnp.sum(jnp.where(oh1, before, 0.0), axis=-1, keepdims=True)
    r2 = jnp.sum(jnp.where(oh2, before, 0.0), axis=-1, keepdims=True)
    carry_ref[...] += jnp.sum(cnt, axis=0, keepdims=True)
    cnt_ref[...] = carry_ref[...]
    out = jnp.zeros(logits.shape, F32)
    for ln, val in ((ROUTE_I1, i1.astype(F32)), (ROUTE_I2, i2.astype(F32)), (ROUTE_G1, 1.0 / den),
                    (ROUTE_G2, e2 / den), (ROUTE_R1, r1), (ROUTE_R2, r2)):
        out = jnp.where(lane == ln, val, out)
    o_ref[...] = out


def router(x, w_pad, b_pad, *, tm, name):
    m, d = x.shape
    return pl.pallas_call(
        _router_kernel,
        grid=(m // tm,),
        in_specs=[pl.BlockSpec((tm, d), lambda i: (i, 0)),
                  pl.BlockSpec((d, LANES), lambda i: (0, 0)),
                  pl.BlockSpec((1, LANES), lambda i: (0, 0))],
        out_specs=[pl.BlockSpec((tm, LANES), lambda i: (i, 0)),
                   pl.BlockSpec((1, LANES), lambda i: (0, 0))],
        out_shape=[jax.ShapeDtypeStruct((m, LANES), F32), jax.ShapeDtypeStruct((1, LANES), F32)],
        scratch_shapes=[pltpu.VMEM((1, LANES), F32)],
        compiler_params=_cparams("arbitrary"),
        name=name,
    )(x, w_pad, b_pad)


def _gather_rows_start(src_hbm, idx_ref, n, dst, sem):
    def issue(r, carry):
        pltpu.make_async_copy(src_hbm.at[pl.ds(idx_ref[0, 0, r], 1)], dst.at[pl.ds(r, 1)], sem).start()
        return carry

    lax.fori_loop(0, n, issue, 0)


def _gather_rows_wait(src_hbm, n, dst, sem):
    pltpu.make_async_copy(src_hbm.at[pl.ds(0, n)], dst, sem).wait()


def _expert_kernel(te_ref, nt_ref, idx0_ref, idxn_ref, x_hbm, w1_ref, w3_ref, w2_ref, o_ref, xbuf, sem, *, tm, nf):
    m = pl.program_id(0)
    nt = nt_ref[0]
    slot = m % 2

    @pl.when(m == 0)
    def _():
        _gather_rows_start(x_hbm, idx0_ref, tm, xbuf.at[0], sem.at[0])

    @pl.when(m + 1 < nt)
    def _():
        _gather_rows_start(x_hbm, idxn_ref, tm, xbuf.at[1 - slot], sem.at[1 - slot])

    @pl.when(m < nt)
    def _():
        _gather_rows_wait(x_hbm, tm, xbuf.at[slot], sem.at[slot])
        x = xbuf[slot].astype(BF16)
        tf = w1_ref.shape[2] // nf
        y = None
        for f in range(nf):
            sl = slice(f * tf, (f + 1) * tf)
            h1 = jnp.dot(x, w1_ref[0, :, sl], preferred_element_type=F32)
            h3 = jnp.dot(x, w3_ref[0, :, sl], preferred_element_type=F32)
            a = (jax.nn.silu(h1) * h3).astype(BF16)
            part = jnp.dot(a, w2_ref[0, sl, :], preferred_element_type=F32)
            y = part if y is None else y + part
        o_ref[...] = y

    @pl.when(m >= nt)
    def _():
        o_ref[...] = jnp.zeros_like(o_ref)


def expert_tiles(x, w1, w3, w2, tile_expert, n_tiles, src_rows, *, tm, nf, name):
    m, d = x.shape
    ne, _, dff = w1.shape
    nt_max = src_rows.shape[0]

    def wmap(i, te, nt):
        return (te[jnp.minimum(i, nt[0] - 1)], 0, 0)

    grid_spec = pltpu.PrefetchScalarGridSpec(
        num_scalar_prefetch=2,
        grid=(nt_max,),
        in_specs=[pl.BlockSpec((1, 1, tm), lambda i, te, nt: (0, 0, 0), memory_space=pltpu.SMEM),
                  pl.BlockSpec((1, 1, tm), lambda i, te, nt: (jnp.minimum(i + 1, nt_max - 1), 0, 0),
                               memory_space=pltpu.SMEM),
                  pl.BlockSpec(memory_space=pl.ANY),
                  pl.BlockSpec((1, d, dff), wmap),
                  pl.BlockSpec((1, d, dff), wmap),
                  pl.BlockSpec((1, dff, d), wmap)],
        out_specs=pl.BlockSpec((tm, d), lambda i, te, nt: (i, 0)),
        scratch_shapes=[pltpu.VMEM((2, tm, d), F32), pltpu.SemaphoreType.DMA((2,))],
    )
    return pl.pallas_call(
        functools.partial(_expert_kernel, tm=tm, nf=nf),
        grid_spec=grid_spec,
        out_shape=jax.ShapeDtypeStruct((nt_max * tm, d), F32),
        compiler_params=_cparams("arbitrary"),
        name=name,
    )(tile_expert, n_tiles, src_rows, src_rows, x, w1, w3, w2)


def _combine_ln_kernel(idx0_ref, idxn_ref, y_hbm, x_ref, route_ref, g_ref, b_ref, op_ref, os_ref, ybuf, sem,
                       *, tm, nsteps, n_first):
    i = pl.program_id(0)
    slot = i % 2

    @pl.when(i == 0)
    def _():
        _gather_rows_start(y_hbm, idx0_ref, 2 * tm, ybuf.at[0], sem.at[0])

    @pl.when(i + 1 < nsteps)
    def _():
        _gather_rows_start(y_hbm, idxn_ref, 2 * tm, ybuf.at[1 - slot], sem.at[1 - slot])

    _gather_rows_wait(y_hbm, 2 * tm, ybuf.at[slot], sem.at[slot])
    route = route_ref[...]
    g1 = route[:, ROUTE_G1:ROUTE_G1 + 1]
    g2 = route[:, ROUTE_G2:ROUTE_G2 + 1]
    y = g1 * ybuf[slot, 0:tm, :] + g2 * ybuf[slot, tm:2 * tm, :]
    out = _deepnorm_ln(x_ref[...], y, g_ref[...], b_ref[...])

    @pl.when(i < n_first)
    def _():
        op_ref[...] = out

    @pl.when(i >= n_first)
    def _():
        os_ref[...] = out


def combine_ln(x, route, y_sorted, pos_rows, g, b, *, tm, split, name):
    m, d = x.shape
    nsteps = m // tm
    n_first = split // tm
    return pl.pallas_call(
        functools.partial(_combine_ln_kernel, tm=tm, nsteps=nsteps, n_first=n_first),
        grid=(nsteps,),
        in_specs=[pl.BlockSpec((1, 1, 2 * tm), lambda i: (0, 0, 0), memory_space=pltpu.SMEM),
                  pl.BlockSpec((1, 1, 2 * tm), lambda i: (jnp.minimum(i + 1, nsteps - 1), 0, 0),
                               memory_space=pltpu.SMEM),
                  pl.BlockSpec(memory_space=pl.ANY),
                  pl.BlockSpec((tm, d), lambda i: (i, 0)),
                  pl.BlockSpec((tm, LANES), lambda i: (i, 0)),
                  pl.BlockSpec((1, d), lambda i: (0, 0)),
                  pl.BlockSpec((1, d), lambda i: (0, 0))],
        out_specs=[pl.BlockSpec((tm, d), lambda i: (jnp.minimum(i, n_first - 1), 0)),
                   pl.BlockSpec((tm, d), lambda i: (jnp.maximum(i - n_first, 0), 0))],
        out_shape=[jax.ShapeDtypeStruct((split, d), F32), jax.ShapeDtypeStruct((m - split, d), F32)],
        scratch_shapes=[pltpu.VMEM((2, 2 * tm, d), F32), pltpu.SemaphoreType.DMA((2,))],
        compiler_params=_cparams("arbitrary"),
        name=name,
    )(pos_rows, pos_rows, y_sorted, x, route, g.reshape(1, d), b.reshape(1, d))


def moe_routed_ln(x, w_router_pad, b_router_pad, w1, w3, w2, g, b, *, tm, split, name):
    m, d = x.shape
    ne = w1.shape[0]
    te_rows = EXPERT_TILE
    route, counts = router(x, w_router_pad, b_router_pad, tm=tm, name=name + "_router")
    counts = counts[0, :ne].astype(jnp.int32)
    padded = (counts + te_rows - 1) // te_rows * te_rows
    upto = jnp.arange(ne)[None, :] <= jnp.arange(ne)[:, None]
    ends = jnp.sum(jnp.where(upto, padded[None, :], 0), axis=1)
    base = ends - padded
    i1 = route[:, ROUTE_I1].astype(jnp.int32)
    i2 = route[:, ROUTE_I2].astype(jnp.int32)
    pos1 = base[i1] + route[:, ROUTE_R1].astype(jnp.int32)
    pos2 = base[i2] + route[:, ROUTE_R2].astype(jnp.int32)
    nt_max = (2 * m + ne * (te_rows - 1)) // te_rows + 1
    tok = jnp.arange(m, dtype=jnp.int32)
    src = jnp.zeros((nt_max * te_rows,), jnp.int32).at[pos1].set(tok).at[pos2].set(tok)
    n_tiles = (ends[-1] // te_rows).astype(jnp.int32).reshape(1)
    tile_start = jnp.arange(nt_max, dtype=jnp.int32) * te_rows
    tile_expert = jnp.minimum(jnp.sum(tile_start[:, None] >= ends[None, :], axis=1), ne - 1).astype(jnp.int32)
    y_sorted = expert_tiles(x, w1, w3, w2, tile_expert, n_tiles, src.reshape(nt_max, 1, te_rows),
                            tm=te_rows, nf=2, name=name + "_experts")
    pos_rows = jnp.concatenate([pos1.reshape(m // tm, 1, tm), pos2.reshape(m // tm, 1, tm)], axis=-1)
    return combine_ln(x, route, y_sorted, pos_rows, g, b, tm=tm, split=split, name=name + "_combine")


def _xattn_kernel(q_ref, mk_ref, mv_ref, o_ref):
    scale = X_HEAD_DIM ** -0.5
    for h in range(X_HEADS):
        sl = slice(h * X_HEAD_DIM, (h + 1) * X_HEAD_DIM)
        q = q_ref[:, sl]
        k = mk_ref[0, :, sl].astype(BF16)
        v = mv_ref[0, :, sl].astype(BF16)
        s = lax.dot_general(q, k, NT_DIMS, preferred_element_type=F32) * scale
        m = jnp.max(s, axis=-1, keepdims=True)
        p = jnp.exp(s - m)
        l = jnp.sum(p, axis=-1, keepdims=True)
        o = jnp.dot(p.astype(BF16), v, preferred_element_type=F32) / l
        o_ref[:, sl] = o.astype(o_ref.dtype)


def cross_attention(q, mem_k, mem_v, *, row0, rows, tq, name):
    d = q.shape[1]
    nb = mem_k.shape[0]
    per_b = rows // nb
    nq = per_b // tq
    base = row0 // tq
    return pl.pallas_call(
        _xattn_kernel,
        grid=(nb, nq),
        in_specs=[pl.BlockSpec((tq, d), lambda b, i: (base + b * nq + i, 0)),
                  pl.BlockSpec((1, N_MEM, d), lambda b, i: (b, 0, 0)),
                  pl.BlockSpec((1, N_MEM, d), lambda b, i: (b, 0, 0))],
        out_specs=pl.BlockSpec((tq, d), lambda b, i: (b * nq + i, 0)),
        out_shape=jax.ShapeDtypeStruct((rows, d), BF16),
        compiler_params=_cparams("arbitrary", "arbitrary"),
        name=name,
    )(q, mem_k, mem_v)


A_BLOCK = A_WINDOW
A_KEYS = A_WINDOW + A_BLOCK


def _head_lane_masks():
    lane = lax.broadcasted_iota(jnp.int32, (1, LANES), 1)
    return [(lane >= a * HEAD_DIM) & (lane < (a + 1) * HEAD_DIM) for a in range(PAIR)]


def _band_prompt_kernel(q_ref, kp_ref, kc_ref, vp_ref, vc_ref, bias_ref, o_ref):
    kk = jnp.concatenate([kp_ref[...].astype(BF16), kc_ref[...].astype(BF16)], axis=0)
    vv = jnp.concatenate([vp_ref[...].astype(BF16), vc_ref[...].astype(BF16)], axis=0)
    masks = _head_lane_masks()
    q = q_ref[...] * (HEAD_DIM ** -0.5 * LOG2E)
    out = None
    for a in range(PAIR):
        qm = jnp.where(masks[a], q, 0.0).astype(BF16)
        s = lax.dot_general(qm, kk, NT_DIMS, preferred_element_type=F32) + bias_ref[0, a]
        m = jnp.max(s, axis=-1, keepdims=True)
        p = jnp.exp2(s - m)
        l = jnp.sum(p, axis=-1, keepdims=True)
        o = jnp.dot(p.astype(BF16), vv, preferred_element_type=F32) / l
        out = o if out is None else jnp.where(masks[a], o, out)
    o_ref[...] = out.astype(o_ref.dtype)


def band_attention_prompt(h, bias, *, seq, name):
    nblk = seq // A_BLOCK
    npair = A_WIDTH // LANES
    qcol, kcol, vcol = 0, npair, 2 * npair
    return pl.pallas_call(
        _band_prompt_kernel,
        grid=(npair, nblk),
        in_specs=[pl.BlockSpec((A_BLOCK, LANES), lambda p, i: (i, qcol + p)),
                  pl.BlockSpec((A_BLOCK, LANES), lambda p, i: (jnp.maximum(i - 1, 0), kcol + p)),
                  pl.BlockSpec((A_BLOCK, LANES), lambda p, i: (i, kcol + p)),
                  pl.BlockSpec((A_BLOCK, LANES), lambda p, i: (jnp.maximum(i - 1, 0), vcol + p)),
                  pl.BlockSpec((A_BLOCK, LANES), lambda p, i: (i, vcol + p)),
                  pl.BlockSpec((1, PAIR, A_BLOCK, A_KEYS), lambda p, i: (jnp.minimum(i, 1), p, 0, 0))],
        out_specs=pl.BlockSpec((A_BLOCK, LANES), lambda p, i: (i, p)),
        out_shape=jax.ShapeDtypeStruct((seq, A_WIDTH), BF16),
        compiler_params=_cparams("arbitrary", "arbitrary"),
        name=name,
    )(h, h, h, h, h, bias)


def _head_rows(x, nheads):
    t, w = x.shape
    x3 = jnp.broadcast_to(x[None], (nheads, t, w))
    hh = lax.broadcasted_iota(jnp.int32, (nheads, 1, w), 0)
    cc = lax.broadcasted_iota(jnp.int32, (nheads, 1, w), 2) // HEAD_DIM
    return jnp.where(hh == cc, x3, jnp.zeros_like(x3)).reshape(nheads * t, w)


def _head_diag(r, nheads, t):
    w = r.shape[1]
    r3 = r.reshape(nheads, t, w)
    hh = lax.broadcasted_iota(jnp.int32, (nheads, 1, w), 0)
    cc = lax.broadcasted_iota(jnp.int32, (nheads, 1, w), 2) // HEAD_DIM
    return jnp.sum(jnp.where(hh == cc, r3, 0.0), axis=0)


def _band_sample_kernel(q_ref, kn_ref, vn_ref, kc_ref, vc_ref, bc_ref, bn_ref, o_ref):
    scale = HEAD_DIM ** -0.5
    t = q_ref.shape[0]
    qr = _head_rows(q_ref[...], A_HEADS).astype(BF16)
    kc = kc_ref[0].astype(BF16)
    vc = vc_ref[0].astype(BF16)
    kn = kn_ref[...].astype(BF16)
    vn = vn_ref[...].astype(BF16)
    s_c = lax.dot_general(qr, kc, NT_DIMS, preferred_element_type=F32) * scale + bc_ref[...]
    s_n = lax.dot_general(qr, kn, NT_DIMS, preferred_element_type=F32) * scale + bn_ref[...]
    m = jnp.maximum(jnp.max(s_c, axis=-1, keepdims=True), jnp.max(s_n, axis=-1, keepdims=True))
    p_c = jnp.exp(s_c - m)
    p_n = jnp.exp(s_n - m)
    l = jnp.sum(p_c, axis=-1, keepdims=True) + jnp.sum(p_n, axis=-1, keepdims=True)
    r = (jnp.dot(p_c.astype(BF16), vc, preferred_element_type=F32)
         + jnp.dot(p_n.astype(BF16), vn, preferred_element_type=F32)) / l
    o_ref[...] = _head_diag(r, A_HEADS, t).astype(o_ref.dtype)


def band_attention_sample(h, k_cache, v_cache, bias_c, bias_n, *, row0, nb, t, name):
    base = row0 // t
    w = k_cache.shape[1]
    return pl.pallas_call(
        _band_sample_kernel,
        grid=(nb,),
        in_specs=[pl.BlockSpec((t, A_WIDTH), lambda b: (base + b, 0)),
                  pl.BlockSpec((t, A_WIDTH), lambda b: (base + b, 1)),
                  pl.BlockSpec((t, A_WIDTH), lambda b: (base + b, 2)),
                  pl.BlockSpec((1, w, A_WIDTH), lambda b: (b, 0, 0)),
                  pl.BlockSpec((1, w, A_WIDTH), lambda b: (b, 0, 0)),
                  pl.BlockSpec((A_HEADS * t, w), lambda b: (0, 0)),
                  pl.BlockSpec((A_HEADS * t, t), lambda b: (0, 0))],
        out_specs=pl.BlockSpec((t, A_WIDTH), lambda b: (b, 0)),
        out_shape=jax.ShapeDtypeStruct((nb * t, A_WIDTH), BF16),
        compiler_params=_cparams("arbitrary"),
        name=name,
    )(h, h, h, k_cache, v_cache, bias_c, bias_n)


POOL_HALO = POOL_STATE + 1


def _pool_core(ext_ref, rows, pos0, pw_ref, sc_ref):
    pos = (pos0 + lax.broadcasted_iota(jnp.int32, (rows, 1), 0)).astype(F32)
    outs = []
    for g, w in enumerate(POOL_WINDOWS):
        sl = slice(g * B_GROUP, (g + 1) * B_GROUP)
        cur = ext_ref[POOL_HALO:POOL_HALO + rows, sl]
        win = cur
        for r in range(1, w):
            win = win + ext_ref[POOL_HALO - r:POOL_HALO - r + rows, sl]
        cnt = jnp.minimum(float(w), pos + 1.0)
        d = (win / cnt - cur).astype(BF16)
        outs.append(jnp.dot(d, pw_ref[g], preferred_element_type=F32))
    return jnp.concatenate(outs, axis=-1) * sc_ref[...]


def _pool_prompt_kernel(up_ref, uc_ref, pw_ref, sc_ref, o_ref, ext_ref, *, tm):
    i = pl.program_id(0)
    ext_ref[0:POOL_HALO, :] = jnp.where(i > 0, up_ref[...], 0.0)
    ext_ref[POOL_HALO:POOL_HALO + tm, :] = uc_ref[...]
    o_ref[...] = _pool_core(ext_ref, tm, i * tm, pw_ref, sc_ref).astype(o_ref.dtype)


def pool_prompt(h, pool_w, pool_scale, *, seq, tm, name):
    ucol = 3 * A_WIDTH // B_WIDTH
    per = tm // POOL_HALO
    return pl.pallas_call(
        functools.partial(_pool_prompt_kernel, tm=tm),
        grid=(seq // tm,),
        in_specs=[pl.BlockSpec((POOL_HALO, B_WIDTH), lambda i: (jnp.maximum(i * per - 1, 0), ucol)),
                  pl.BlockSpec((tm, B_WIDTH), lambda i: (i, ucol)),
                  pl.BlockSpec((len(POOL_WINDOWS), B_GROUP, B_GROUP), lambda i: (0, 0, 0)),
                  pl.BlockSpec((1, B_WIDTH), lambda i: (0, 0))],
        out_specs=pl.BlockSpec((tm, B_WIDTH), lambda i: (i, 0)),
        out_shape=jax.ShapeDtypeStruct((seq, B_WIDTH), BF16),
        scratch_shapes=[pltpu.VMEM((POOL_HALO + tm, B_WIDTH), F32)],
        compiler_params=_cparams("arbitrary"),
        name=name,
    )(h, h, pool_w, pool_scale.reshape(1, B_WIDTH))


def _pool_sample_kernel(hist_ref, u_ref, pw_ref, sc_ref, o_ref, ext_ref, *, t, pos0):
    ext_ref[0:POOL_HALO, :] = hist_ref[0]
    ext_ref[POOL_HALO:POOL_HALO + t, :] = u_ref[...]
    o_ref[...] = _pool_core(ext_ref, t, pos0, pw_ref, sc_ref).astype(o_ref.dtype)


def pool_sample(h, hist, pool_w, pool_scale, *, row0, nb, t, pos0, name):
    ucol = 3 * A_WIDTH // B_WIDTH
    base = row0 // t
    return pl.pallas_call(
        functools.partial(_pool_sample_kernel, t=t, pos0=pos0),
        grid=(nb,),
        in_specs=[pl.BlockSpec((1, POOL_HALO, B_WIDTH), lambda b: (b, 0, 0)),
                  pl.BlockSpec((t, B_WIDTH), lambda b: (base + b, ucol)),
                  pl.BlockSpec((len(POOL_WINDOWS), B_GROUP, B_GROUP), lambda b: (0, 0, 0)),
                  pl.BlockSpec((1, B_WIDTH), lambda b: (0, 0))],
        out_specs=pl.BlockSpec((t, B_WIDTH), lambda b: (b, 0)),
        out_shape=jax.ShapeDtypeStruct((nb * t, B_WIDTH), BF16),
        scratch_shapes=[pltpu.VMEM((POOL_HALO + t, B_WIDTH), F32)],
        compiler_params=_cparams("arbitrary"),
        name=name,
    )(hist, h, pool_w, pool_scale.reshape(1, B_WIDTH))


CUMSUM_ALIGN = 8 * LANES


def _cumsum_kernel(x_ref, o_ref, *, nh, nblk):
    r = lax.broadcasted_iota(jnp.int32, (LANES, LANES), 0)
    c = lax.broadcasted_iota(jnp.int32, (LANES, LANES), 1)
    upper = (r <= c).astype(F32)
    rb = lax.broadcasted_iota(jnp.int32, (nblk, nblk), 0)
    cb = lax.broadcasted_iota(jnp.int32, (nblk, nblk), 1)
    before = (cb < rb).astype(F32)
    for h in range(nh):
        rows = slice(h * nblk, (h + 1) * nblk)
        local = jnp.dot(x_ref[0, rows, :], upper, preferred_element_type=F32, precision=lax.Precision.HIGHEST)
        tot = jnp.broadcast_to(local[:, LANES - 1:LANES], (nblk, LANES))
        o_ref[0, rows, :] = local + jnp.dot(before, tot, preferred_element_type=F32,
                                            precision=lax.Precision.HIGHEST)


def cumsum_lanes(x, *, name):
    nb, nh, ln = x.shape
    nblk = ln // LANES
    out = pl.pallas_call(
        functools.partial(_cumsum_kernel, nh=nh, nblk=nblk),
        grid=(nb,),
        in_specs=[pl.BlockSpec((1, nh * nblk, LANES), lambda b: (b, 0, 0))],
        out_specs=pl.BlockSpec((1, nh * nblk, LANES), lambda b: (b, 0, 0)),
        out_shape=jax.ShapeDtypeStruct((nb, nh * nblk, LANES), F32),
        compiler_params=_cparams("arbitrary"),
        name=name,
    )(x.reshape(nb, nh * nblk, LANES))
    return out.reshape(nb, nh, ln)


def _cumsum_rows_kernel(x_ref, o_ref, offs_ref, *, nblk):
    r = lax.broadcasted_iota(jnp.int32, (LANES, LANES), 0)
    c = lax.broadcasted_iota(jnp.int32, (LANES, LANES), 1)
    lower = (c <= r).astype(F32)

    def local(b, carry):
        off = pl.multiple_of(b * LANES, LANES)
        o_ref[pl.ds(off, LANES), :] = jnp.dot(lower, x_ref[pl.ds(off, LANES), :], preferred_element_type=F32,
                                              precision=lax.Precision.HIGHEST)
        return carry

    lax.fori_loop(0, nblk, local, 0)
    totals = o_ref[pl.ds(LANES - 1, nblk, stride=LANES), :]
    rb = lax.broadcasted_iota(jnp.int32, (nblk, nblk), 0)
    cb = lax.broadcasted_iota(jnp.int32, (nblk, nblk), 1)
    offs_ref[...] = jnp.dot((cb < rb).astype(F32), totals, preferred_element_type=F32,
                            precision=lax.Precision.HIGHEST)

    def shift(b, carry):
        off = pl.multiple_of(b * LANES, LANES)
        o_ref[pl.ds(off, LANES), :] = o_ref[pl.ds(off, LANES), :] + offs_ref[pl.ds(b, 1), :]
        return carry

    lax.fori_loop(0, nblk, shift, 0)


def cumsum_rows(x, *, name):
    ln, nh = x.shape
    nblk = ln // LANES
    return pl.pallas_call(
        functools.partial(_cumsum_rows_kernel, nblk=nblk),
        grid=(1,),
        in_specs=[pl.BlockSpec((ln, nh), lambda i: (0, 0))],
        out_specs=pl.BlockSpec((ln, nh), lambda i: (0, 0)),
        out_shape=jax.ShapeDtypeStruct((ln, nh), F32),
        scratch_shapes=[pltpu.VMEM((nblk, nh), F32)],
        compiler_params=_cparams("arbitrary"),
        name=name,
    )(x)


FA_CHUNK = 64


def _split3(x):
    hi = x.astype(BF16)
    r = x - hi.astype(F32)
    mid = r.astype(BF16)
    lo = (r - mid.astype(F32)).astype(BF16)
    return hi, mid, lo


def _spare_lane_columns(lane, base, first, second):
    out = jnp.zeros(jnp.broadcast_shapes(lane.shape, first[0].shape, second[0].shape), F32)
    for j, val in enumerate(tuple(first) + tuple(second)):
        out = jnp.where(lane == base + j, val.astype(F32), out)
    return out


def _fa_prep_kernel(k_ref, v_ref, d_ref, ka_ref, va_ref):
    lane = lax.broadcasted_iota(jnp.int32, (1, LANES), 1)
    row = lax.broadcasted_iota(jnp.int32, (LANES, 1), 0)
    masks = _head_lane_masks()
    d = d_ref[...] * (-LOG2E)
    one = jnp.ones((1, 1), BF16)
    for h in range(C_HEADS):
        p, a = divmod(h, PAIR)
        sl = slice(p * LANES, (p + 1) * LANES)
        ext = _spare_lane_columns(lane, HEAD_DIM * (1 - a), _split3(d[:, h:h + 1]), (one, one, one))
        ka_ref[h] = jnp.where(masks[a], k_ref[:, sl].astype(F32), ext).astype(BF16)
        vt = v_ref[:, sl].T
        mine = (row >= a * HEAD_DIM) & (row < (a + 1) * HEAD_DIM)
        va_ref[h] = jnp.where(mine, vt, 1.0).astype(BF16)


def fa_prep(qkv16, v32, dcum, *, seq, tr, name):
    kcol = 1
    return pl.pallas_call(
        _fa_prep_kernel,
        grid=(seq // tr,),
        in_specs=[pl.BlockSpec((tr, C_WIDTH), lambda r: (r, kcol)),
                  pl.BlockSpec((tr, C_WIDTH), lambda r: (r, 0)),
                  pl.BlockSpec((tr, C_HEADS), lambda r: (r, 0))],
        out_specs=[pl.BlockSpec((C_HEADS, tr, LANES), lambda r: (0, r, 0)),
                   pl.BlockSpec((C_HEADS, LANES, tr), lambda r: (0, 0, r))],
        out_shape=[jax.ShapeDtypeStruct((C_HEADS, seq, LANES), BF16),
                   jax.ShapeDtypeStruct((C_HEADS, LANES, seq), BF16)],
        compiler_params=_cparams("arbitrary"),
        name=name,
    )(qkv16, v32, dcum)


def _fa_prompt_kernel(q_ref, ka_ref, va_ref, d_ref, o_ref, m_ref, acc_ref, s0_ref, s1_ref, p_ref, *, tq, tk):
    p = pl.program_id(0)
    qi = pl.program_id(1)
    lane = lax.broadcasted_iota(jnp.int32, (1, LANES), 1)
    head_col = lax.broadcasted_iota(jnp.int32, (1, C_HEADS), 1)
    masks = _head_lane_masks()
    q = q_ref[...].astype(F32) * (HEAD_DIM ** -0.5 * LOG2E)
    d0 = d_ref[0:1, :] * LOG2E
    one = jnp.ones((1, 1), BF16)
    qa = []
    for a in range(PAIR):
        dref = jnp.sum(jnp.where(head_col == PAIR * p + a, d0, 0.0), axis=-1, keepdims=True)
        ext = _spare_lane_columns(lane, HEAD_DIM * (1 - a), (one, one, one), _split3(dref))
        qa.append(jnp.where(masks[a], q, ext).astype(BF16))

    m_ref[...] = jnp.full(m_ref.shape, NEG_INF, F32)
    acc_ref[...] = jnp.zeros(acc_ref.shape, F32)

    jd = (qi * tq) // tk
    nchunk = tk // FA_CHUNK
    s_refs = (s0_ref, s1_ref)

    def scores(j, slot, diagonal):
        off = pl.multiple_of(j * tk, tk)
        s_ref = s_refs[slot]
        for a in range(PAIR):
            st = lax.dot_general(ka_ref[a, pl.ds(off, tk), :], qa[a], NT_DIMS, preferred_element_type=F32)
            if diagonal:
                krow = lax.broadcasted_iota(jnp.int32, (tk, tq), 0)
                qcol = lax.broadcasted_iota(jnp.int32, (tk, tq), 1)
                st = jnp.where(krow - qcol <= qi * tq - j * tk, st, NEG_INF)
            s_ref[a] = st

    def absorb(j, slot):
        off = pl.multiple_of(j * tk, tk)
        s_ref = s_refs[slot]
        for a in range(PAIR):
            cm = None
            for c in range(nchunk):
                x = s_ref[a, c * FA_CHUNK:(c + 1) * FA_CHUNK, :].reshape(FA_CHUNK // 8, 8, tq)
                part = jnp.max(x, axis=0)
                cm = part if cm is None else jnp.maximum(cm, part)
            m_old = m_ref[a]
            m_new = jnp.maximum(m_old, jnp.max(cm, axis=0, keepdims=True))
            alpha = jnp.exp2(m_old - m_new)
            for c in range(nchunk):
                rows = slice(c * FA_CHUNK, (c + 1) * FA_CHUNK)
                p_ref[a, rows, :] = jnp.exp2(s_ref[a, rows, :] - m_new).astype(BF16)
            pv = jnp.dot(va_ref[a, :, pl.ds(off, tk)], p_ref[a], preferred_element_type=F32)
            acc_ref[a] = alpha * acc_ref[a] + pv
            m_ref[a] = m_new

    scores(jd, 0, True)

    def body(u, carry):
        scores(2 * u, 1, False)
        absorb(jnp.where(u == 0, jd, 2 * u - 1), 0)
        scores(2 * u + 1, 0, False)
        absorb(2 * u, 1)
        return carry

    npairs = jd // 2
    lax.fori_loop(0, npairs, body, 0)
    last0 = jnp.where(npairs == 0, jd, 2 * npairs - 1)

    @pl.when(jd % 2 == 1)
    def _():
        scores(jd - 1, 1, False)
        absorb(last0, 0)
        absorb(jd - 1, 1)

    @pl.when(jd % 2 == 0)
    def _():
        absorb(last0, 0)

    row = lax.broadcasted_iota(jnp.int32, (LANES, 1), 0)
    out = None
    for a in range(PAIR):
        acc = acc_ref[a]
        lrow = HEAD_DIM * (1 - a)
        o = acc / acc[lrow:lrow + 1, :]
        out = o if out is None else jnp.where(row < HEAD_DIM, out, o)
    o_ref[...] = out.T.astype(o_ref.dtype)


def forgetting_attention_prompt(qkv16, ka, va, dcum, *, seq, tq, tk, name):
    npair = C_WIDTH // LANES
    return pl.pallas_call(
        functools.partial(_fa_prompt_kernel, tq=tq, tk=tk),
        grid=(npair, seq // tq),
        in_specs=[pl.BlockSpec((tq, LANES), lambda p, i: (i, p)),
                  pl.BlockSpec((PAIR, seq, LANES), lambda p, i: (p, 0, 0)),
                  pl.BlockSpec((PAIR, LANES, seq), lambda p, i: (p, 0, 0)),
                  pl.BlockSpec((8, C_HEADS), lambda p, i: (i * (tq // 8), 0))],
        out_specs=pl.BlockSpec((tq, LANES), lambda p, i: (i, p)),
        out_shape=jax.ShapeDtypeStruct((seq, C_WIDTH), BF16),
        scratch_shapes=[pltpu.VMEM((PAIR, 1, tq), F32),
                        pltpu.VMEM((PAIR, LANES, tq), F32),
                        pltpu.VMEM((PAIR, tk, tq), F32),
                        pltpu.VMEM((PAIR, tk, tq), F32),
                        pltpu.VMEM((PAIR, tk, tq), BF16)],
        compiler_params=_cparams("arbitrary", "arbitrary"),
        name=name,
    )(qkv16, ka, va, dcum)


def _fa_sample_kernel(q_ref, kn_ref, vn_ref, kc_ref, vc_ref, dc_ref, dp_ref, dn_ref, o_ref,
                      qr_ref, m_ref, l_ref, acc_ref, *, t, nkc):
    kc_i = pl.program_id(1)
    scale = HEAD_DIM ** -0.5
    rows = C_HEADS * t

    @pl.when(kc_i == 0)
    def _():
        qr_ref[...] = _head_rows(q_ref[...], C_HEADS)
        m_ref[...] = jnp.full(m_ref.shape, NEG_INF, F32)
        l_ref[...] = jnp.zeros(l_ref.shape, F32)
        acc_ref[...] = jnp.zeros(acc_ref.shape, F32)

    qr = qr_ref[...]
    d_tot = dp_ref[0][:, :, LANES - 1:LANES]

    def update(s, v):
        m = m_ref[...]
        m_new = jnp.maximum(m, jnp.max(s, axis=-1, keepdims=True))
        alpha = jnp.exp(m - m_new)
        p = jnp.exp(s - m_new)
        l_ref[...] = alpha * l_ref[...] + jnp.sum(p, axis=-1, keepdims=True)
        acc_ref[...] = alpha * acc_ref[...] + jnp.dot(p.astype(BF16), v, preferred_element_type=F32)
        m_ref[...] = m_new

    tk = kc_ref.shape[1]
    bias_c = jnp.broadcast_to(d_tot - dc_ref[0], (C_HEADS, t, tk)).reshape(rows, tk)
    s_c = lax.dot_general(qr, kc_ref[0].astype(BF16), NT_DIMS, preferred_element_type=F32) * scale + bias_c
    update(s_c, vc_ref[0].astype(BF16))

    @pl.when(kc_i == nkc - 1)
    def _():
        bias_n = jnp.broadcast_to(d_tot - dn_ref[0][:, :, 0:t], (C_HEADS, t, t)).reshape(rows, t)
        s_n = lax.dot_general(qr, kn_ref[...], NT_DIMS, preferred_element_type=F32) * scale + bias_n
        qpos = lax.broadcasted_iota(jnp.int32, (C_HEADS, t, t), 1).reshape(rows, t)
        kpos = lax.broadcasted_iota(jnp.int32, (C_HEADS, t, t), 2).reshape(rows, t)
        s_n = jnp.where(kpos <= qpos, s_n, NEG_INF)
        update(s_n, vn_ref[...])
        r = acc_ref[...] / l_ref[...]
        o_ref[...] = _head_diag(r, C_HEADS, t).astype(o_ref.dtype)


def forgetting_attention_sample(qkv, k_cache, v_cache, dall, *, row0, nb, t, tk, name):
    base = row0 // t
    past = k_cache.shape[1]
    nkc = past // tk
    rows = C_HEADS * t
    return pl.pallas_call(
        functools.partial(_fa_sample_kernel, t=t, nkc=nkc),
        grid=(nb, nkc),
        in_specs=[pl.BlockSpec((t, C_WIDTH), lambda b, c: (base + b, 0)),
                  pl.BlockSpec((t, C_WIDTH), lambda b, c: (base + b, 1)),
                  pl.BlockSpec((t, C_WIDTH), lambda b, c: (base + b, 2)),
                  pl.BlockSpec((1, tk, C_WIDTH), lambda b, c: (b, c, 0)),
                  pl.BlockSpec((1, tk, C_WIDTH), lambda b, c: (b, c, 0)),
                  pl.BlockSpec((1, C_HEADS, 1, tk), lambda b, c: (b, 0, 0, c)),
                  pl.BlockSpec((1, C_HEADS, 1, LANES), lambda b, c: (b, 0, 0, past // LANES - 1)),
                  pl.BlockSpec((1, C_HEADS, 1, LANES), lambda b, c: (b, 0, 0, past // LANES))],
        out_specs=pl.BlockSpec((t, C_WIDTH), lambda b, c: (b, 0)),
        out_shape=jax.ShapeDtypeStruct((nb * t, C_WIDTH), BF16),
        scratch_shapes=[pltpu.VMEM((rows, C_WIDTH), BF16),
                        pltpu.VMEM((rows, 1), F32),
                        pltpu.VMEM((rows, 1), F32),
                        pltpu.VMEM((rows, C_WIDTH), F32)],
        compiler_params=_cparams("arbitrary", "arbitrary"),
        name=name,
    )(qkv, qkv, qkv, k_cache, v_cache, dall, dall, dall)


def _rel_bias_matrix(rel_table, nq, nk):
    diag = jnp.arange(nq + nk - 1) - (nq - 1)
    vals = rel_table[:, jnp.clip(A_WINDOW - diag, -REL_CLIP, REL_CLIP) + REL_CLIP].astype(F32)
    rows = [lax.slice_in_dim(vals, nq - 1 - i, nq - 1 - i + nk, axis=1) for i in range(nq)]
    return jnp.stack(rows, axis=1)


def _band_block_bias(bias_chunk):
    nchunk = A_BLOCK // CHUNK
    parts = [jnp.pad(bias_chunk * LOG2E, ((0, 0), (0, 0), (CHUNK * c, A_KEYS - A_BAND - CHUNK * c)),
                     constant_values=NEG_INF) for c in range(nchunk)]
    later = jnp.stack(parts, axis=1).reshape(A_HEADS, A_BLOCK, A_KEYS)
    col = jnp.arange(A_KEYS)
    first = jnp.where(col >= A_WINDOW, later, NEG_INF)
    return jnp.stack([first, later])


def kernel(x_prompt, x_sample, cache_a_k, cache_a_v, state_pool, cache_c_k, cache_c_v, cache_c_logf,
           cache_mem_k, cache_mem_v, mem_prompt, w_in_ab, rel_bias_a, pool_w, pool_scale, w_out_ab,
           w_in_c, b_f, w_out_c, w_xq, w_xk, w_xv, w_xo, ln_g, ln_b, ffn_w1, ffn_w3, ffn_w2,
           w_router, b_router, moe_w1, moe_w3, moe_w2):
    bp, seq, d = x_prompt.shape
    nb, t, _ = x_sample.shape
    assert bp == 1 and d == D_MODEL
    ns = nb * t
    tot = seq + ns
    past = cache_c_k.shape[2]
    tm = 512
    assert tot % tm == 0 and seq % tm == 0

    x = jnp.concatenate([x_prompt.reshape(seq, d), x_sample.reshape(ns, d)], axis=0)

    mem = mem_prompt.reshape(N_MEM, d)
    p_mem_k, p_mem_v = [], []
    for layer in range(DEPTH):
        (mk,) = matmul(mem, w_xk[layer].astype(BF16), [F32], tm=N_MEM, tn=d, name=f"mem_k{layer}")
        (mv,) = matmul(mem, w_xv[layer].astype(BF16), [F32], tm=N_MEM, tn=d, name=f"mem_v{layer}")
        p_mem_k.append(mk)
        p_mem_v.append(mv)

    def cross_block(x, layer):
        (q,) = matmul(x, w_xq[layer].astype(BF16), [BF16], tm=tm, tn=d, name=f"xq{layer}")
        o_p = cross_attention(q, p_mem_k[layer].reshape(1, N_MEM, d), p_mem_v[layer].reshape(1, N_MEM, d),
                              row0=0, rows=seq, tq=tm, name=f"xattn_p{layer}")
        o_s = cross_attention(q, cache_mem_k[layer].reshape(nb, N_MEM, d), cache_mem_v[layer].reshape(nb, N_MEM, d),
                              row0=seq, rows=ns, tq=t, name=f"xattn_s{layer}")
        return matmul_ln([(o_p, o_s)], w_xo[layer].astype(BF16), x, ln_g[layer, 1], ln_b[layer, 1], tm=tm,
                         name=f"xo{layer}")

    pr = 0
    (h,) = matmul(x, w_in_ab[pr].astype(BF16), [F32], tm=tm, tn=w_in_ab.shape[-1], name="in_ab")
    bias = _rel_bias_matrix(rel_bias_a[pr], CHUNK, A_BAND)
    oa_p = band_attention_prompt(h, _band_block_bias(bias), seq=seq, name="band_p")
    bias_s = bias[:, :t, :A_WINDOW + t].reshape(A_HEADS * t, A_WINDOW + t)
    oa_s = band_attention_sample(h, cache_a_k[pr].reshape(nb, A_WINDOW, A_WIDTH),
                                 cache_a_v[pr].reshape(nb, A_WINDOW, A_WIDTH),
                                 bias_s[:, :A_WINDOW], bias_s[:, A_WINDOW:], row0=seq, nb=nb, t=t, name="band_s")
    pw = pool_w[pr].astype(BF16)
    ob_p = pool_prompt(h, pw, pool_scale[pr], seq=seq, tm=tm, name="pool_p")
    hist = jnp.concatenate([jnp.zeros((nb, 1, B_WIDTH), F32), state_pool[pr]], axis=1)
    ob_s = pool_sample(h, hist, pw, pool_scale[pr], row0=seq, nb=nb, t=t, pos0=past, name="pool_s")
    x = matmul_ln([(oa_p, oa_s), (ob_p, ob_s)], w_out_ab[pr].astype(BF16), x, ln_g[0, 0], ln_b[0, 0], tm=tm,
                  name="out_ab")
    x = cross_block(x, 0)
    ones = jnp.ones((1, tot, 1), F32)
    x = moe_ln(x, ones, ffn_w1.astype(BF16), ffn_w3.astype(BF16), ffn_w2.astype(BF16),
               ln_g[0, 2], ln_b[0, 2], tm=tm, tf=D_FF // 2, name="ffn")

    k_a = h[:, A_WIDTH:2 * A_WIDTH]
    v_a = h[:, 2 * A_WIDTH:3 * A_WIDTH]
    u_b = h[:, 3 * A_WIDTH:]
    keep = min(A_WINDOW, seq)
    p_a_k = k_a[seq - keep:seq].reshape(1, 1, keep, A_HEADS, HEAD_DIM)
    p_a_v = v_a[seq - keep:seq].reshape(1, 1, keep, A_HEADS, HEAD_DIM)
    p_pool = u_b[seq - POOL_STATE:seq].reshape(1, 1, POOL_STATE, B_WIDTH)
    s_a_k = k_a[seq:].reshape(1, nb, t, A_HEADS, HEAD_DIM)
    s_a_v = v_a[seq:].reshape(1, nb, t, A_HEADS, HEAD_DIM)
    s_pool = jnp.concatenate([state_pool[pr], u_b[seq:].reshape(nb, t, B_WIDTH)], axis=1)[:, -POOL_STATE:][None]

    w_c = w_in_c[pr]
    qkv16, k_cp, v_cp, k_cs, v_cs = matmul_qkv(x, w_c[:, :3 * C_WIDTH].astype(BF16), seq=seq, tm=tm, name="in_c")
    w_f = jnp.pad(w_c[:, 3 * C_WIDTH:], ((0, 0), (0, LANES - C_HEADS))).astype(BF16)
    bf_pad = jnp.pad(b_f[pr].astype(F32), (0, LANES - C_HEADS)).reshape(1, LANES)
    logf = matmul_logsig(x, w_f, bf_pad, tm=tm, name="in_c_f")[:, :C_HEADS]
    dcum_p = cumsum_rows(logf[:seq], name="cumsum_p")
    ka, va = fa_prep(qkv16, v_cp, dcum_p, seq=seq, tr=tm, name="fa_prep")
    oc_p = forgetting_attention_prompt(qkv16, ka, va, dcum_p, seq=seq, tq=tm, tk=tm, name="fa_p")
    lf_s = logf[seq:].reshape(nb, t, C_HEADS)
    lf_all = jnp.concatenate([cache_c_logf[pr].astype(F32), lf_s,
                              jnp.zeros((nb, CUMSUM_ALIGN - t, C_HEADS), F32)], axis=1)
    dall = cumsum_lanes(jnp.swapaxes(lf_all, 1, 2), name="cumsum_s").reshape(nb, C_HEADS, 1, past + CUMSUM_ALIGN)
    oc_s = forgetting_attention_sample(qkv16, cache_c_k[pr].reshape(nb, past, C_WIDTH),
                                       cache_c_v[pr].reshape(nb, past, C_WIDTH), dall,
                                       row0=seq, nb=nb, t=t, tk=1024, name="fa_s")
    x = matmul_ln([(oc_p, oc_s)], w_out_c[pr].astype(BF16), x, ln_g[1, 0], ln_b[1, 0], tm=tm, name="out_c")
    x = cross_block(x, 1)
    w_r = jnp.pad(w_router[pr], ((0, 0), (0, LANES - N_EXPERTS)))
    b_r = jnp.concatenate([b_router[pr].astype(F32), jnp.full((LANES - N_EXPERTS,), NEG_INF, F32)]).reshape(1, LANES)
    y_p, y_s = moe_routed_ln(x, w_r, b_r, moe_w1[pr].astype(BF16), moe_w3[pr].astype(BF16),
                             moe_w2[pr].astype(BF16), ln_g[1, 2], ln_b[1, 2], tm=tm, split=seq, name="moe")

    p_c_k = k_cp.reshape(1, 1, seq, C_HEADS, HEAD_DIM)
    p_c_v = v_cp.reshape(1, 1, seq, C_HEADS, HEAD_DIM)
    p_c_logf = logf[:seq].reshape(1, 1, seq, C_HEADS)
    s_c_k = k_cs.reshape(1, nb, t, C_HEADS, HEAD_DIM)
    s_c_v = v_cs.reshape(1, nb, t, C_HEADS, HEAD_DIM)
    s_c_logf = lf_s[None]

    y_prompt = y_p.reshape(1, seq, d)
    y_sample = y_s.reshape(nb, t, d)
    pmk = jnp.stack(p_mem_k).reshape(DEPTH, 1, N_MEM, X_HEADS, X_HEAD_DIM)
    pmv = jnp.stack(p_mem_v).reshape(DEPTH, 1, N_MEM, X_HEADS, X_HEAD_DIM)
    return (y_prompt, y_sample, p_a_k, p_a_v, p_pool, p_c_k, p_c_v, p_c_logf, pmk, pmv,
            s_a_k, s_a_v, s_pool, s_c_k, s_c_v, s_c_logf)
```

```python
import functools

import jax
import jax.numpy as jnp
from jax import lax
from jax.experimental import pallas as pl
from jax.experimental.pallas import tpu as pltpu

F32 = jnp.float32
BF16 = jnp.bfloat16

D_MODEL = 1024
DEPTH = 2
CHUNK = 64
N_MEM = 256
HEAD_DIM = 64
A_HEADS = 8
A_WIDTH = A_HEADS * HEAD_DIM
A_PREV_CHUNKS = 8
A_WINDOW = A_PREV_CHUNKS * CHUNK
A_BAND = (A_PREV_CHUNKS + 1) * CHUNK
REL_CLIP = 128
POOL_WINDOWS = (2, 4, 8, 16)
B_WIDTH = D_MODEL - A_WIDTH
B_GROUP = B_WIDTH // len(POOL_WINDOWS)
POOL_STATE = max(POOL_WINDOWS) - 1
C_HEADS = D_MODEL // HEAD_DIM
C_WIDTH = C_HEADS * HEAD_DIM
X_HEADS = 4
X_HEAD_DIM = D_MODEL // X_HEADS
D_FF = 2816
N_EXPERTS = 8
ALPHA = (2.0 * DEPTH) ** 0.25
LN_EPS = 1e-5
NEG_INF = -1e30
LOG2E = 1.4426950408889634

LANES = 128
PAIR = LANES // HEAD_DIM
VMEM_LIMIT = 56 * 1024 * 1024

NT_DIMS = (((1,), (1,)), ((), ()))


def _cparams(*sem):
    return pltpu.CompilerParams(dimension_semantics=sem, vmem_limit_bytes=VMEM_LIMIT)


def _deepnorm_ln(res, sub, g, b):
    z = ALPHA * res + sub
    mu = jnp.mean(z, axis=-1, keepdims=True)
    zc = z - mu
    var = jnp.mean(zc * zc, axis=-1, keepdims=True)
    return zc * lax.rsqrt(var + LN_EPS) * g + b


def _mm_kernel(x_ref, w_ref, *o_refs):
    y = jnp.dot(x_ref[...].astype(BF16), w_ref[...], preferred_element_type=F32)
    for o_ref in o_refs:
        o_ref[...] = y.astype(o_ref.dtype)


def matmul(x, w, out_dtypes, *, tm, tn, name):
    m, k = x.shape
    n = w.shape[1]
    outs = pl.pallas_call(
        _mm_kernel,
        grid=(n // tn, m // tm),
        in_specs=[pl.BlockSpec((tm, k), lambda j, i: (i, 0)),
                  pl.BlockSpec((k, tn), lambda j, i: (0, j))],
        out_specs=[pl.BlockSpec((tm, tn), lambda j, i: (i, j)) for _ in out_dtypes],
        out_shape=[jax.ShapeDtypeStruct((m, n), dt) for dt in out_dtypes],
        compiler_params=_cparams("arbitrary", "arbitrary"),
        name=name,
    )(x, w)
    return outs


def _mm_qkv_kernel(x_ref, w_ref, bf_ref, qkv_ref, lf_ref, kp_ref, vp_ref, ks_ref, vs_ref, *, n_prompt):
    i = pl.program_id(0)
    y = jnp.dot(x_ref[...].astype(BF16), w_ref[...], preferred_element_type=F32)
    qkv_ref[...] = y[:, :3 * C_WIDTH].astype(BF16)
    lf_ref[...] = jax.nn.log_sigmoid(y[:, 3 * C_WIDTH:] + bf_ref[...])
    k = y[:, C_WIDTH:2 * C_WIDTH]
    v = y[:, 2 * C_WIDTH:3 * C_WIDTH]

    @pl.when(i < n_prompt)
    def _():
        kp_ref[...] = k
        vp_ref[...] = v

    @pl.when(i >= n_prompt)
    def _():
        ks_ref[...] = k
        vs_ref[...] = v


def matmul_qkv(x, w, b_f, *, seq, tm, name):
    m, d = x.shape
    n = 3 * C_WIDTH
    n_prompt = seq // tm
    n_sample = (m - seq) // tm
    prompt_map = lambda i: (jnp.minimum(i, n_prompt - 1), 0)
    sample_map = lambda i: (jnp.maximum(i - n_prompt, 0), 0)
    return pl.pallas_call(
        functools.partial(_mm_qkv_kernel, n_prompt=n_prompt),
        grid=(n_prompt + n_sample,),
        in_specs=[pl.BlockSpec((tm, d), lambda i: (i, 0)),
                  pl.BlockSpec((d, n + LANES), lambda i: (0, 0)),
                  pl.BlockSpec((1, LANES), lambda i: (0, 0))],
        out_specs=[pl.BlockSpec((tm, n), lambda i: (i, 0)),
                   pl.BlockSpec((tm, LANES), lambda i: (i, 0)),
                   pl.BlockSpec((tm, C_WIDTH), prompt_map), pl.BlockSpec((tm, C_WIDTH), prompt_map),
                   pl.BlockSpec((tm, C_WIDTH), sample_map), pl.BlockSpec((tm, C_WIDTH), sample_map)],
        out_shape=[jax.ShapeDtypeStruct((m, n), BF16),
                   jax.ShapeDtypeStruct((m, LANES), F32),
                   jax.ShapeDtypeStruct((seq, C_WIDTH), F32), jax.ShapeDtypeStruct((seq, C_WIDTH), F32),
                   jax.ShapeDtypeStruct((m - seq, C_WIDTH), F32), jax.ShapeDtypeStruct((m - seq, C_WIDTH), F32)],
        compiler_params=_cparams("arbitrary"),
        name=name,
    )(x, w, b_f)


def _mm_ln_kernel(*refs, n_groups, n_first):
    x_refs = refs[:2 * n_groups]
    w_ref, res_ref, g_ref, b_ref, o_ref = refs[2 * n_groups:]
    i = pl.program_id(0)

    def run(xs):
        y, k0 = None, 0
        for x_ref in xs:
            kg = x_ref.shape[1]
            part = jnp.dot(x_ref[...].astype(BF16), w_ref[k0:k0 + kg, :], preferred_element_type=F32)
            y = part if y is None else y + part
            k0 += kg
        o_ref[...] = _deepnorm_ln(res_ref[...], y, g_ref[...], b_ref[...])

    @pl.when(i < n_first)
    def _():
        run(x_refs[0::2])

    @pl.when(i >= n_first)
    def _():
        run(x_refs[1::2])


def matmul_ln(groups, w, res, g, b, *, tm, name):
    m, n = res.shape
    split = groups[0][0].shape[0]
    n_first = split // tm
    first_map = lambda i: (jnp.minimum(i, n_first - 1), 0)
    rest_map = lambda i: (jnp.maximum(i - n_first, 0), 0)
    x_specs, x_args = [], []
    for xf, xr in groups:
        x_specs += [pl.BlockSpec((tm, xf.shape[1]), first_map), pl.BlockSpec((tm, xr.shape[1]), rest_map)]
        x_args += [xf, xr]
    return pl.pallas_call(
        functools.partial(_mm_ln_kernel, n_groups=len(groups), n_first=n_first),
        grid=(m // tm,),
        in_specs=x_specs + [pl.BlockSpec(w.shape, lambda i: (0, 0)),
                            pl.BlockSpec((tm, n), lambda i: (i, 0)),
                            pl.BlockSpec((1, n), lambda i: (0, 0)),
                            pl.BlockSpec((1, n), lambda i: (0, 0))],
        out_specs=pl.BlockSpec((tm, n), lambda i: (i, 0)),
        out_shape=jax.ShapeDtypeStruct((m, n), F32),
        compiler_params=_cparams("arbitrary"),
        name=name,
    )(*x_args, w, res, g.reshape(1, n), b.reshape(1, n))


def _moe_kernel(x_ref, comb_ref, w1_ref, w3_ref, w2_ref, g_ref, b_ref, o_ref, acc_ref, *, ne, nf):
    e = pl.program_id(1)
    f = pl.program_id(2)

    @pl.when((e == 0) & (f == 0))
    def _():
        acc_ref[...] = jnp.zeros_like(acc_ref)

    x = x_ref[...].astype(BF16)
    h1 = jnp.dot(x, w1_ref[0], preferred_element_type=F32)
    h3 = jnp.dot(x, w3_ref[0], preferred_element_type=F32)
    a = (jax.nn.silu(h1) * h3).astype(BF16)
    y = jnp.dot(a, w2_ref[0], preferred_element_type=F32)
    acc_ref[...] += comb_ref[0] * y

    @pl.when((e == ne - 1) & (f == nf - 1))
    def _():
        o_ref[...] = _deepnorm_ln(x_ref[...], acc_ref[...], g_ref[...], b_ref[...])


def moe_ln(x, comb, w1, w3, w2, g, b, *, tm, tf, name):
    m, d = x.shape
    ne, _, dff = w1.shape
    nf = dff // tf
    return pl.pallas_call(
        functools.partial(_moe_kernel, ne=ne, nf=nf),
        grid=(m // tm, ne, nf),
        in_specs=[pl.BlockSpec((tm, d), lambda i, e, f: (i, 0)),
                  pl.BlockSpec((1, tm, 1), lambda i, e, f: (e, i, 0)),
                  pl.BlockSpec((1, d, tf), lambda i, e, f: (e, 0, f)),
                  pl.BlockSpec((1, d, tf), lambda i, e, f: (e, 0, f)),
                  pl.BlockSpec((1, tf, d), lambda i, e, f: (e, f, 0)),
                  pl.BlockSpec((1, d), lambda i, e, f: (0, 0)),
                  pl.BlockSpec((1, d), lambda i, e, f: (0, 0))],
        out_specs=pl.BlockSpec((tm, d), lambda i, e, f: (i, 0)),
        out_shape=jax.ShapeDtypeStruct((m, d), F32),
        scratch_shapes=[pltpu.VMEM((tm, d), F32)],
        compiler_params=_cparams("arbitrary", "arbitrary", "arbitrary"),
        name=name,
    )(x, comb, w1, w3, w2, g.reshape(1, d), b.reshape(1, d))


ROUTE_I1, ROUTE_I2, ROUTE_G1, ROUTE_G2, ROUTE_R1, ROUTE_R2 = range(6)
EXPERT_TILE = 256


def _router_kernel(x_ref, w_ref, b_ref, o_ref, cnt_ref, carry_ref):
    i = pl.program_id(0)

    @pl.when(i == 0)
    def _():
        carry_ref[...] = jnp.zeros_like(carry_ref)

    tm = x_ref.shape[0]
    logits = jnp.dot(x_ref[...], w_ref[...], preferred_element_type=F32,
                     precision=lax.Precision.HIGHEST) + b_ref[...]
    lane = lax.broadcasted_iota(jnp.int32, logits.shape, 1)
    big = jnp.int32(LANES)
    v1 = jnp.max(logits, axis=-1, keepdims=True)
    i1 = jnp.min(jnp.where(logits == v1, lane, big), axis=-1, keepdims=True)
    rest = jnp.where(lane == i1, NEG_INF, logits)
    v2 = jnp.max(rest, axis=-1, keepdims=True)
    i2 = jnp.min(jnp.where(rest == v2, lane, big), axis=-1, keepdims=True)
    e2 = jnp.exp(v2 - v1)
    den = 1.0 + e2
    oh1 = lane == i1
    oh2 = lane == i2
    cnt = jnp.where(oh1 | oh2, 1.0, 0.0)
    r = lax.broadcasted_iota(jnp.int32, (tm, tm), 0)
    c = lax.broadcasted_iota(jnp.int32, (tm, tm), 1)
    earlier = jnp.where(c < r, 1.0, 0.0).astype(BF16)
    before = jnp.dot(earlier, cnt.astype(BF16), preferred_element_type=F32) + carry_ref[...]
    r1 = jnp.sum(jnp.where(oh1, before, 0.0), axis=-1, keepdims=True)
    r2 = jnp.sum(jnp.where(oh2, before, 0.0), axis=-1, keepdims=True)
    carry_ref[...] += jnp.sum(cnt, axis=0, keepdims=True)
    cnt_ref[...] = carry_ref[...]
    out = jnp.zeros(logits.shape, F32)
    for ln, val in ((ROUTE_I1, i1.astype(F32)), (ROUTE_I2, i2.astype(F32)), (ROUTE_G1, 1.0 / den),
                    (ROUTE_G2, e2 / den), (ROUTE_R1, r1), (ROUTE_R2, r2)):
        out = jnp.where(lane == ln, val, out)
    o_ref[...] = out


def router(x, w_pad, b_pad, *, tm, name):
    m, d = x.shape
    return pl.pallas_call(
        _router_kernel,
        grid=(m // tm,),
        in_specs=[pl.BlockSpec((tm, d), lambda i: (i, 0)),
                  pl.BlockSpec((d, LANES), lambda i: (0, 0)),
                  pl.BlockSpec((1, LANES), lambda i: (0, 0))],
        out_specs=[pl.BlockSpec((tm, LANES), lambda i: (i, 0)),
                   pl.BlockSpec((1, LANES), lambda i: (0, 0))],
        out_shape=[jax.ShapeDtypeStruct((m, LANES), F32), jax.ShapeDtypeStruct((1, LANES), F32)],
        scratch_shapes=[pltpu.VMEM((1, LANES), F32)],
        compiler_params=_cparams("arbitrary"),
        name=name,
    )(x, w_pad, b_pad)


def _gather_rows_start(src_hbm, idx_ref, n, dst, sem):
    def issue(r, carry):
        pltpu.make_async_copy(src_hbm.at[pl.ds(idx_ref[0, 0, r], 1)], dst.at[pl.ds(r, 1)], sem).start()
        return carry

    lax.fori_loop(0, n, issue, 0, unroll=8)


def _gather_rows_wait(src_hbm, n, dst, sem):
    pltpu.make_async_copy(src_hbm.at[pl.ds(0, n)], dst, sem).wait()


def _expert_kernel(te_ref, nt_ref, idx0_ref, idxn_ref, x_hbm, w1_ref, w3_ref, w2_ref, o_ref, xbuf, sem, *, tm, nf):
    m = pl.program_id(0)
    nt = nt_ref[0]
    slot = m % 2

    @pl.when(m == 0)
    def _():
        _gather_rows_start(x_hbm, idx0_ref, tm, xbuf.at[0], sem.at[0])

    @pl.when(m + 1 < nt)
    def _():
        _gather_rows_start(x_hbm, idxn_ref, tm, xbuf.at[1 - slot], sem.at[1 - slot])

    @pl.when(m < nt)
    def _():
        _gather_rows_wait(x_hbm, tm, xbuf.at[slot], sem.at[slot])
        x = xbuf[slot].astype(BF16)
        tf = w1_ref.shape[2] // nf
        y = None
        for f in range(nf):
            sl = slice(f * tf, (f + 1) * tf)
            h1 = jnp.dot(x, w1_ref[0, :, sl], preferred_element_type=F32)
            h3 = jnp.dot(x, w3_ref[0, :, sl], preferred_element_type=F32)
            a = (jax.nn.silu(h1) * h3).astype(BF16)
            part = jnp.dot(a, w2_ref[0, sl, :], preferred_element_type=F32)
            y = part if y is None else y + part
        o_ref[...] = y

    @pl.when(m >= nt)
    def _():
        o_ref[...] = jnp.zeros_like(o_ref)


def expert_tiles(x, w1, w3, w2, tile_expert, n_tiles, src_rows, *, tm, nf, name):
    m, d = x.shape
    ne, _, dff = w1.shape
    nt_max = src_rows.shape[0]

    def wmap(i, te, nt):
        return (te[jnp.minimum(i, nt[0] - 1)], 0, 0)

    grid_spec = pltpu.PrefetchScalarGridSpec(
        num_scalar_prefetch=2,
        grid=(nt_max,),
        in_specs=[pl.BlockSpec((1, 1, tm), lambda i, te, nt: (0, 0, 0), memory_space=pltpu.SMEM),
                  pl.BlockSpec((1, 1, tm), lambda i, te, nt: (jnp.minimum(i + 1, nt_max - 1), 0, 0),
                               memory_space=pltpu.SMEM),
                  pl.BlockSpec(memory_space=pl.ANY),
                  pl.BlockSpec((1, d, dff), wmap),
                  pl.BlockSpec((1, d, dff), wmap),
                  pl.BlockSpec((1, dff, d), wmap)],
        out_specs=pl.BlockSpec((tm, d), lambda i, te, nt: (i, 0)),
        scratch_shapes=[pltpu.VMEM((2, tm, d), F32), pltpu.SemaphoreType.DMA((2,))],
    )
    return pl.pallas_call(
        functools.partial(_expert_kernel, tm=tm, nf=nf),
        grid_spec=grid_spec,
        out_shape=jax.ShapeDtypeStruct((nt_max * tm, d), F32),
        compiler_params=_cparams("arbitrary"),
        name=name,
    )(tile_expert, n_tiles, src_rows, src_rows, x, w1, w3, w2)


def _combine_ln_kernel(idx0_ref, idxn_ref, y_hbm, x_ref, route_ref, g_ref, b_ref, op_ref, os_ref, ybuf, sem,
                       *, tm, nsteps, n_first):
    i = pl.program_id(0)
    slot = i % 2

    @pl.when(i == 0)
    def _():
        _gather_rows_start(y_hbm, idx0_ref, 2 * tm, ybuf.at[0], sem.at[0])

    @pl.when(i + 1 < nsteps)
    def _():
        _gather_rows_start(y_hbm, idxn_ref, 2 * tm, ybuf.at[1 - slot], sem.at[1 - slot])

    _gather_rows_wait(y_hbm, 2 * tm, ybuf.at[slot], sem.at[slot])
    route = route_ref[...]
    g1 = route[:, ROUTE_G1:ROUTE_G1 + 1]
    g2 = route[:, ROUTE_G2:ROUTE_G2 + 1]
    y = g1 * ybuf[slot, 0:tm, :] + g2 * ybuf[slot, tm:2 * tm, :]
    out = _deepnorm_ln(x_ref[...], y, g_ref[...], b_ref[...])

    @pl.when(i < n_first)
    def _():
        op_ref[...] = out

    @pl.when(i >= n_first)
    def _():
        os_ref[...] = out


def combine_ln(x, route, y_sorted, pos_rows, g, b, *, tm, split, name):
    m, d = x.shape
    nsteps = m // tm
    n_first = split // tm
    return pl.pallas_call(
        functools.partial(_combine_ln_kernel, tm=tm, nsteps=nsteps, n_first=n_first),
        grid=(nsteps,),
        in_specs=[pl.BlockSpec((1, 1, 2 * tm), lambda i: (0, 0, 0), memory_space=pltpu.SMEM),
                  pl.BlockSpec((1, 1, 2 * tm), lambda i: (jnp.minimum(i + 1, nsteps - 1), 0, 0),
                               memory_space=pltpu.SMEM),
                  pl.BlockSpec(memory_space=pl.ANY),
                  pl.BlockSpec((tm, d), lambda i: (i, 0)),
                  pl.BlockSpec((tm, LANES), lambda i: (i, 0)),
                  pl.BlockSpec((1, d), lambda i: (0, 0)),
                  pl.BlockSpec((1, d), lambda i: (0, 0))],
        out_specs=[pl.BlockSpec((tm, d), lambda i: (jnp.minimum(i, n_first - 1), 0)),
                   pl.BlockSpec((tm, d), lambda i: (jnp.maximum(i - n_first, 0), 0))],
        out_shape=[jax.ShapeDtypeStruct((split, d), F32), jax.ShapeDtypeStruct((m - split, d), F32)],
        scratch_shapes=[pltpu.VMEM((2, 2 * tm, d), F32), pltpu.SemaphoreType.DMA((2,))],
        compiler_params=_cparams("arbitrary"),
        name=name,
    )(pos_rows, pos_rows, y_sorted, x, route, g.reshape(1, d), b.reshape(1, d))


def moe_routed_ln(x, w_router_pad, b_router_pad, w1, w3, w2, g, b, *, tm, split, name):
    m, d = x.shape
    ne = w1.shape[0]
    te_rows = EXPERT_TILE
    route, counts = router(x, w_router_pad, b_router_pad, tm=tm, name=name + "_router")
    counts = counts[0, :ne].astype(jnp.int32)
    padded = (counts + te_rows - 1) // te_rows * te_rows
    upto = jnp.arange(ne)[None, :] <= jnp.arange(ne)[:, None]
    ends = jnp.sum(jnp.where(upto, padded[None, :], 0), axis=1)
    base = ends - padded
    i1 = route[:, ROUTE_I1].astype(jnp.int32)
    i2 = route[:, ROUTE_I2].astype(jnp.int32)
    pos1 = base[i1] + route[:, ROUTE_R1].astype(jnp.int32)
    pos2 = base[i2] + route[:, ROUTE_R2].astype(jnp.int32)
    nt_max = (2 * m + ne * (te_rows - 1)) // te_rows + 1
    tok = jnp.arange(m, dtype=jnp.int32)
    src = jnp.zeros((nt_max * te_rows,), jnp.int32).at[pos1].set(tok).at[pos2].set(tok)
    n_tiles = (ends[-1] // te_rows).astype(jnp.int32).reshape(1)
    tile_start = jnp.arange(nt_max, dtype=jnp.int32) * te_rows
    tile_expert = jnp.minimum(jnp.sum(tile_start[:, None] >= ends[None, :], axis=1), ne - 1).astype(jnp.int32)
    y_sorted = expert_tiles(x, w1, w3, w2, tile_expert, n_tiles, src.reshape(nt_max, 1, te_rows),
                            tm=te_rows, nf=2, name=name + "_experts")
    pos_rows = jnp.concatenate([pos1.reshape(m // tm, 1, tm), pos2.reshape(m // tm, 1, tm)], axis=-1)
    return combine_ln(x, route, y_sorted, pos_rows, g, b, tm=tm, split=split, name=name + "_combine")


def _xattn_kernel(q_ref, mk_ref, mv_ref, o_ref):
    scale = X_HEAD_DIM ** -0.5
    for h in range(X_HEADS):
        sl = slice(h * X_HEAD_DIM, (h + 1) * X_HEAD_DIM)
        q = q_ref[:, sl]
        k = mk_ref[0, :, sl].astype(BF16)
        v = mv_ref[0, :, sl].astype(BF16)
        s = lax.dot_general(q, k, NT_DIMS, preferred_element_type=F32) * scale
        m = jnp.max(s, axis=-1, keepdims=True)
        p = jnp.exp(s - m)
        l = jnp.sum(p, axis=-1, keepdims=True)
        o = jnp.dot(p.astype(BF16), v, preferred_element_type=F32) / l
        o_ref[:, sl] = o.astype(o_ref.dtype)


def cross_attention(q, mem_k, mem_v, *, row0, rows, tq, name):
    d = q.shape[1]
    nb = mem_k.shape[0]
    per_b = rows // nb
    nq = per_b // tq
    base = row0 // tq
    return pl.pallas_call(
        _xattn_kernel,
        grid=(nb, nq),
        in_specs=[pl.BlockSpec((tq, d), lambda b, i: (base + b * nq + i, 0)),
                  pl.BlockSpec((1, N_MEM, d), lambda b, i: (b, 0, 0)),
                  pl.BlockSpec((1, N_MEM, d), lambda b, i: (b, 0, 0))],
        out_specs=pl.BlockSpec((tq, d), lambda b, i: (b * nq + i, 0)),
        out_shape=jax.ShapeDtypeStruct((rows, d), BF16),
        compiler_params=_cparams("arbitrary", "arbitrary"),
        name=name,
    )(q, mem_k, mem_v)


A_BLOCK = A_WINDOW
A_KEYS = A_WINDOW + A_BLOCK


def _head_lane_masks():
    lane = lax.broadcasted_iota(jnp.int32, (1, LANES), 1)
    return [(lane >= a * HEAD_DIM) & (lane < (a + 1) * HEAD_DIM) for a in range(PAIR)]


def _band_prompt_kernel(q_ref, kp_ref, kc_ref, vp_ref, vc_ref, bias_ref, o_ref):
    kk = jnp.concatenate([kp_ref[...].astype(BF16), kc_ref[...].astype(BF16)], axis=0)
    vv = jnp.concatenate([vp_ref[...].astype(BF16), vc_ref[...].astype(BF16)], axis=0)
    masks = _head_lane_masks()
    q = q_ref[...] * (HEAD_DIM ** -0.5 * LOG2E)
    out = None
    for a in range(PAIR):
        qm = jnp.where(masks[a], q, 0.0).astype(BF16)
        s = lax.dot_general(qm, kk, NT_DIMS, preferred_element_type=F32) + bias_ref[0, a]
        m = jnp.max(s, axis=-1, keepdims=True)
        p = jnp.exp2(s - m)
        l = jnp.sum(p, axis=-1, keepdims=True)
        o = jnp.dot(p.astype(BF16), vv, preferred_element_type=F32) / l
        out = o if out is None else jnp.where(masks[a], o, out)
    o_ref[...] = out.astype(o_ref.dtype)


def band_attention_prompt(h, bias, *, seq, name):
    nblk = seq // A_BLOCK
    npair = A_WIDTH // LANES
    qcol, kcol, vcol = 0, npair, 2 * npair
    return pl.pallas_call(
        _band_prompt_kernel,
        grid=(npair, nblk),
        in_specs=[pl.BlockSpec((A_BLOCK, LANES), lambda p, i: (i, qcol + p)),
                  pl.BlockSpec((A_BLOCK, LANES), lambda p, i: (jnp.maximum(i - 1, 0), kcol + p)),
                  pl.BlockSpec((A_BLOCK, LANES), lambda p, i: (i, kcol + p)),
                  pl.BlockSpec((A_BLOCK, LANES), lambda p, i: (jnp.maximum(i - 1, 0), vcol + p)),
                  pl.BlockSpec((A_BLOCK, LANES), lambda p, i: (i, vcol + p)),
                  pl.BlockSpec((1, PAIR, A_BLOCK, A_KEYS), lambda p, i: (jnp.minimum(i, 1), p, 0, 0))],
        out_specs=pl.BlockSpec((A_BLOCK, LANES), lambda p, i: (i, p)),
        out_shape=jax.ShapeDtypeStruct((seq, A_WIDTH), BF16),
        compiler_params=_cparams("arbitrary", "arbitrary"),
        name=name,
    )(h, h, h, h, h, bias)


def _head_rows(x, nheads):
    t, w = x.shape
    x3 = jnp.broadcast_to(x[None], (nheads, t, w))
    hh = lax.broadcasted_iota(jnp.int32, (nheads, 1, w), 0)
    cc = lax.broadcasted_iota(jnp.int32, (nheads, 1, w), 2) // HEAD_DIM
    return jnp.where(hh == cc, x3, jnp.zeros_like(x3)).reshape(nheads * t, w)


def _head_diag(r, nheads, t):
    w = r.shape[1]
    r3 = r.reshape(nheads, t, w)
    hh = lax.broadcasted_iota(jnp.int32, (nheads, 1, w), 0)
    cc = lax.broadcasted_iota(jnp.int32, (nheads, 1, w), 2) // HEAD_DIM
    return jnp.sum(jnp.where(hh == cc, r3, 0.0), axis=0)


def _band_sample_kernel(q_ref, kn_ref, vn_ref, kc_ref, vc_ref, bc_ref, bn_ref, o_ref):
    scale = HEAD_DIM ** -0.5
    t = q_ref.shape[0]
    qr = _head_rows(q_ref[...], A_HEADS).astype(BF16)
    kc = kc_ref[0].astype(BF16)
    vc = vc_ref[0].astype(BF16)
    kn = kn_ref[...].astype(BF16)
    vn = vn_ref[...].astype(BF16)
    s_c = lax.dot_general(qr, kc, NT_DIMS, preferred_element_type=F32) * scale + bc_ref[...]
    s_n = lax.dot_general(qr, kn, NT_DIMS, preferred_element_type=F32) * scale + bn_ref[...]
    m = jnp.maximum(jnp.max(s_c, axis=-1, keepdims=True), jnp.max(s_n, axis=-1, keepdims=True))
    p_c = jnp.exp(s_c - m)
    p_n = jnp.exp(s_n - m)
    l = jnp.sum(p_c, axis=-1, keepdims=True) + jnp.sum(p_n, axis=-1, keepdims=True)
    r = (jnp.dot(p_c.astype(BF16), vc, preferred_element_type=F32)
         + jnp.dot(p_n.astype(BF16), vn, preferred_element_type=F32)) / l
    o_ref[...] = _head_diag(r, A_HEADS, t).astype(o_ref.dtype)


def band_attention_sample(h, k_cache, v_cache, bias_c, bias_n, *, row0, nb, t, name):
    base = row0 // t
    w = k_cache.shape[1]
    return pl.pallas_call(
        _band_sample_kernel,
        grid=(nb,),
        in_specs=[pl.BlockSpec((t, A_WIDTH), lambda b: (base + b, 0)),
                  pl.BlockSpec((t, A_WIDTH), lambda b: (base + b, 1)),
                  pl.BlockSpec((t, A_WIDTH), lambda b: (base + b, 2)),
                  pl.BlockSpec((1, w, A_WIDTH), lambda b: (b, 0, 0)),
                  pl.BlockSpec((1, w, A_WIDTH), lambda b: (b, 0, 0)),
                  pl.BlockSpec((A_HEADS * t, w), lambda b: (0, 0)),
                  pl.BlockSpec((A_HEADS * t, t), lambda b: (0, 0))],
        out_specs=pl.BlockSpec((t, A_WIDTH), lambda b: (b, 0)),
        out_shape=jax.ShapeDtypeStruct((nb * t, A_WIDTH), BF16),
        compiler_params=_cparams("arbitrary"),
        name=name,
    )(h, h, h, k_cache, v_cache, bias_c, bias_n)


POOL_HALO = POOL_STATE + 1


def _pool_core(ext_ref, rows, pos0, pw_ref, sc_ref):
    pos = (pos0 + lax.broadcasted_iota(jnp.int32, (rows, 1), 0)).astype(F32)
    outs = []
    for g, w in enumerate(POOL_WINDOWS):
        sl = slice(g * B_GROUP, (g + 1) * B_GROUP)
        cur = ext_ref[POOL_HALO:POOL_HALO + rows, sl]
        win = cur
        for r in range(1, w):
            win = win + ext_ref[POOL_HALO - r:POOL_HALO - r + rows, sl]
        cnt = jnp.minimum(float(w), pos + 1.0)
        d = (win / cnt - cur).astype(BF16)
        outs.append(jnp.dot(d, pw_ref[g], preferred_element_type=F32))
    return jnp.concatenate(outs, axis=-1) * sc_ref[...]


def _pool_prompt_kernel(up_ref, uc_ref, pw_ref, sc_ref, o_ref, ext_ref, *, tm):
    i = pl.program_id(0)
    ext_ref[0:POOL_HALO, :] = jnp.where(i > 0, up_ref[...], 0.0)
    ext_ref[POOL_HALO:POOL_HALO + tm, :] = uc_ref[...]
    o_ref[...] = _pool_core(ext_ref, tm, i * tm, pw_ref, sc_ref).astype(o_ref.dtype)


def pool_prompt(h, pool_w, pool_scale, *, seq, tm, name):
    ucol = 3 * A_WIDTH // B_WIDTH
    per = tm // POOL_HALO
    return pl.pallas_call(
        functools.partial(_pool_prompt_kernel, tm=tm),
        grid=(seq // tm,),
        in_specs=[pl.BlockSpec((POOL_HALO, B_WIDTH), lambda i: (jnp.maximum(i * per - 1, 0), ucol)),
                  pl.BlockSpec((tm, B_WIDTH), lambda i: (i, ucol)),
                  pl.BlockSpec((len(POOL_WINDOWS), B_GROUP, B_GROUP), lambda i: (0, 0, 0)),
                  pl.BlockSpec((1, B_WIDTH), lambda i: (0, 0))],
        out_specs=pl.BlockSpec((tm, B_WIDTH), lambda i: (i, 0)),
        out_shape=jax.ShapeDtypeStruct((seq, B_WIDTH), BF16),
        scratch_shapes=[pltpu.VMEM((POOL_HALO + tm, B_WIDTH), F32)],
        compiler_params=_cparams("arbitrary"),
        name=name,
    )(h, h, pool_w, pool_scale.reshape(1, B_WIDTH))


def _pool_sample_kernel(hist_ref, u_ref, pw_ref, sc_ref, o_ref, ext_ref, *, t, pos0):
    ext_ref[0:POOL_HALO, :] = hist_ref[0]
    ext_ref[POOL_HALO:POOL_HALO + t, :] = u_ref[...]
    o_ref[...] = _pool_core(ext_ref, t, pos0, pw_ref, sc_ref).astype(o_ref.dtype)


def pool_sample(h, hist, pool_w, pool_scale, *, row0, nb, t, pos0, name):
    ucol = 3 * A_WIDTH // B_WIDTH
    base = row0 // t
    return pl.pallas_call(
        functools.partial(_pool_sample_kernel, t=t, pos0=pos0),
        grid=(nb,),
        in_specs=[pl.BlockSpec((1, POOL_HALO, B_WIDTH), lambda b: (b, 0, 0)),
                  pl.BlockSpec((t, B_WIDTH), lambda b: (base + b, ucol)),
                  pl.BlockSpec((len(POOL_WINDOWS), B_GROUP, B_GROUP), lambda b: (0, 0, 0)),
                  pl.BlockSpec((1, B_WIDTH), lambda b: (0, 0))],
        out_specs=pl.BlockSpec((t, B_WIDTH), lambda b: (b, 0)),
        out_shape=jax.ShapeDtypeStruct((nb * t, B_WIDTH), BF16),
        scratch_shapes=[pltpu.VMEM((POOL_HALO + t, B_WIDTH), F32)],
        compiler_params=_cparams("arbitrary"),
        name=name,
    )(hist, h, pool_w, pool_scale.reshape(1, B_WIDTH))


CUMSUM_ALIGN = 8 * LANES


def _cumsum_kernel(x_ref, before_ref, o_ref):
    rows = x_ref.shape[1]
    r = lax.broadcasted_iota(jnp.int32, (LANES, LANES), 0)
    c = lax.broadcasted_iota(jnp.int32, (LANES, LANES), 1)
    upper = (r <= c).astype(F32)
    local = jnp.dot(x_ref[0], upper, preferred_element_type=F32, precision=lax.Precision.HIGHEST)
    tot = jnp.broadcast_to(local[:, LANES - 1:LANES], (rows, LANES))
    o_ref[0] = local + jnp.dot(before_ref[...], tot, preferred_element_type=F32, precision=lax.Precision.HIGHEST)


def cumsum_lanes(x, *, name):
    nb, nh, ln = x.shape
    nblk = ln // LANES
    rows = nh * nblk
    rid = jnp.arange(rows)
    before = ((rid[:, None] // nblk == rid[None, :] // nblk) & (rid[None, :] < rid[:, None])).astype(F32)
    out = pl.pallas_call(
        _cumsum_kernel,
        grid=(nb,),
        in_specs=[pl.BlockSpec((1, rows, LANES), lambda b: (b, 0, 0)),
                  pl.BlockSpec((rows, rows), lambda b: (0, 0))],
        out_specs=pl.BlockSpec((1, rows, LANES), lambda b: (b, 0, 0)),
        out_shape=jax.ShapeDtypeStruct((nb, rows, LANES), F32),
        compiler_params=_cparams("arbitrary"),
        name=name,
    )(x.reshape(nb, rows, LANES), before)
    return out.reshape(nb, nh, ln)


def _cumsum_rows_kernel(x_ref, o_ref, offs_ref, *, nblk):
    r = lax.broadcasted_iota(jnp.int32, (LANES, LANES), 0)
    c = lax.broadcasted_iota(jnp.int32, (LANES, LANES), 1)
    lower = (c <= r).astype(F32)

    def local(b, carry):
        off = pl.multiple_of(b * LANES, LANES)
        o_ref[pl.ds(off, LANES), :] = jnp.dot(lower, x_ref[pl.ds(off, LANES), :], preferred_element_type=F32,
                                              precision=lax.Precision.HIGHEST)
        return carry

    lax.fori_loop(0, nblk, local, 0)
    totals = o_ref[pl.ds(LANES - 1, nblk, stride=LANES), :]
    rb = lax.broadcasted_iota(jnp.int32, (nblk, nblk), 0)
    cb = lax.broadcasted_iota(jnp.int32, (nblk, nblk), 1)
    offs_ref[...] = jnp.dot((cb < rb).astype(F32), totals, preferred_element_type=F32,
                            precision=lax.Precision.HIGHEST)

    def shift(b, carry):
        off = pl.multiple_of(b * LANES, LANES)
        o_ref[pl.ds(off, LANES), :] = o_ref[pl.ds(off, LANES), :] + offs_ref[pl.ds(b, 1), :]
        return carry

    lax.fori_loop(0, nblk, shift, 0)


def cumsum_rows(x, *, name):
    ln, nh = x.shape
    nblk = ln // LANES
    return pl.pallas_call(
        functools.partial(_cumsum_rows_kernel, nblk=nblk),
        grid=(1,),
        in_specs=[pl.BlockSpec((ln, nh), lambda i: (0, 0))],
        out_specs=pl.BlockSpec((ln, nh), lambda i: (0, 0)),
        out_shape=jax.ShapeDtypeStruct((ln, nh), F32),
        scratch_shapes=[pltpu.VMEM((nblk, nh), F32)],
        compiler_params=_cparams("arbitrary"),
        name=name,
    )(x)


FA_CHUNK = 64


def _split3(x):
    hi = x.astype(BF16)
    r = x - hi.astype(F32)
    mid = r.astype(BF16)
    lo = (r - mid.astype(F32)).astype(BF16)
    return hi, mid, lo


def _spare_lane_columns(lane, base, first, second):
    out = jnp.zeros(jnp.broadcast_shapes(lane.shape, first[0].shape, second[0].shape), F32)
    for j, val in enumerate(tuple(first) + tuple(second)):
        out = jnp.where(lane == base + j, val.astype(F32), out)
    return out


def _fa_prep_kernel(k_ref, v_ref, d_ref, ka_ref, va_ref):
    lane = lax.broadcasted_iota(jnp.int32, (1, LANES), 1)
    masks = _head_lane_masks()
    d = d_ref[...] * (-LOG2E)
    one = jnp.ones((1, 1), BF16)
    for h in range(C_HEADS):
        p, a = divmod(h, PAIR)
        sl = slice(p * LANES, (p + 1) * LANES)
        ext = _spare_lane_columns(lane, HEAD_DIM * (1 - a), _split3(d[:, h:h + 1]), (one, one, one))
        ka_ref[h] = jnp.where(masks[a], k_ref[:, sl].astype(F32), ext).astype(BF16)
        if a == 0:
            va_ref[p] = v_ref[:, sl].T.astype(BF16)


def fa_prep(qkv16, v32, dcum, *, seq, tr, name):
    kcol = 1
    npair = C_HEADS // PAIR
    return pl.pallas_call(
        _fa_prep_kernel,
        grid=(seq // tr,),
        in_specs=[pl.BlockSpec((tr, C_WIDTH), lambda r: (r, kcol)),
                  pl.BlockSpec((tr, C_WIDTH), lambda r: (r, 0)),
                  pl.BlockSpec((tr, C_HEADS), lambda r: (r, 0))],
        out_specs=[pl.BlockSpec((C_HEADS, tr, LANES), lambda r: (0, r, 0)),
                   pl.BlockSpec((npair, LANES, tr), lambda r: (0, 0, r))],
        out_shape=[jax.ShapeDtypeStruct((C_HEADS, seq, LANES), BF16),
                   jax.ShapeDtypeStruct((npair, LANES, seq), BF16)],
        compiler_params=_cparams("arbitrary"),
        name=name,
    )(qkv16, v32, dcum)


def _fa_prompt_kernel(q_ref, ka_ref, va_ref, d_ref, o_ref, m_ref, l_ref, acc_ref, s0_ref, s1_ref, p_ref, *, tq, tk):
    p = pl.program_id(0)
    qi = pl.program_id(1)
    lane = lax.broadcasted_iota(jnp.int32, (1, LANES), 1)
    head_col = lax.broadcasted_iota(jnp.int32, (1, C_HEADS), 1)
    masks = _head_lane_masks()
    q = q_ref[...].astype(F32) * (HEAD_DIM ** -0.5 * LOG2E)
    d0 = d_ref[0:1, :] * LOG2E
    one = jnp.ones((1, 1), BF16)
    qa = []
    for a in range(PAIR):
        dref = jnp.sum(jnp.where(head_col == PAIR * p + a, d0, 0.0), axis=-1, keepdims=True)
        ext = _spare_lane_columns(lane, HEAD_DIM * (1 - a), (one, one, one), _split3(dref))
        qa.append(jnp.where(masks[a], q, ext).astype(BF16))

    m_ref[...] = jnp.full(m_ref.shape, NEG_INF, F32)
    l_ref[...] = jnp.zeros(l_ref.shape, F32)
    acc_ref[...] = jnp.zeros(acc_ref.shape, F32)

    jd = (qi * tq) // tk
    nchunk = tk // FA_CHUNK
    s_refs = (s0_ref, s1_ref)

    def scores(j, slot, diagonal):
        off = pl.multiple_of(j * tk, tk)
        s_ref = s_refs[slot]
        for a in range(PAIR):
            st = lax.dot_general(ka_ref[a, pl.ds(off, tk), :], qa[a], NT_DIMS, preferred_element_type=F32)
            if diagonal:
                krow = lax.broadcasted_iota(jnp.int32, (tk, tq), 0)
                qcol = lax.broadcasted_iota(jnp.int32, (tk, tq), 1)
                st = jnp.where(krow - qcol <= qi * tq - j * tk, st, NEG_INF)
            s_ref[a] = st

    def absorb(j, slot):
        off = pl.multiple_of(j * tk, tk)
        s_ref = s_refs[slot]
        for a in range(PAIR):
            cm = None
            for c in range(nchunk):
                x = s_ref[a, c * FA_CHUNK:(c + 1) * FA_CHUNK, :].reshape(FA_CHUNK // 8, 8, tq)
                part = jnp.max(x, axis=0)
                cm = part if cm is None else jnp.maximum(cm, part)
            m_old = m_ref[a]
            m_new = jnp.maximum(m_old, jnp.max(cm, axis=0, keepdims=True))
            alpha = jnp.exp2(m_old - m_new)
            cs = None
            for c in range(nchunk):
                rows = slice(c * FA_CHUNK, (c + 1) * FA_CHUNK)
                pc = jnp.exp2(s_ref[a, rows, :] - m_new)
                p_ref[a, rows, :] = pc.astype(BF16)
                part = jnp.sum(pc.reshape(FA_CHUNK // 8, 8, tq), axis=0)
                cs = part if cs is None else cs + part
            vt = va_ref[0, a * HEAD_DIM:(a + 1) * HEAD_DIM, pl.ds(off, tk)]
            acc_ref[a] = alpha * acc_ref[a] + jnp.dot(vt, p_ref[a], preferred_element_type=F32)
            l_ref[a] = alpha * l_ref[a] + jnp.sum(cs, axis=0, keepdims=True)
            m_ref[a] = m_new

    scores(jd, 0, True)

    def body(u, carry):
        scores(2 * u, 1, False)
        absorb(jnp.where(u == 0, jd, 2 * u - 1), 0)
        scores(2 * u + 1, 0, False)
        absorb(2 * u, 1)
        return carry

    npairs = jd // 2
    lax.fori_loop(0, npairs, body, 0)
    last0 = jnp.where(npairs == 0, jd, 2 * npairs - 1)

    @pl.when(jd % 2 == 1)
    def _():
        scores(jd - 1, 1, False)
        absorb(last0, 0)
        absorb(jd - 1, 1)

    @pl.when(jd % 2 == 0)
    def _():
        absorb(last0, 0)

    out = jnp.concatenate([acc_ref[a] / l_ref[a] for a in range(PAIR)], axis=0)
    o_ref[...] = out.T.astype(o_ref.dtype)


def forgetting_attention_prompt(qkv16, ka, va, dcum, *, seq, tq, tk, name):
    npair = C_WIDTH // LANES
    return pl.pallas_call(
        functools.partial(_fa_prompt_kernel, tq=tq, tk=tk),
        grid=(npair, seq // tq),
        in_specs=[pl.BlockSpec((tq, LANES), lambda p, i: (i, p)),
                  pl.BlockSpec((PAIR, seq, LANES), lambda p, i: (p, 0, 0)),
                  pl.BlockSpec((1, LANES, seq), lambda p, i: (p, 0, 0)),
                  pl.BlockSpec((8, C_HEADS), lambda p, i: (i * (tq // 8), 0))],
        out_specs=pl.BlockSpec((tq, LANES), lambda p, i: (i, p)),
        out_shape=jax.ShapeDtypeStruct((seq, C_WIDTH), BF16),
        scratch_shapes=[pltpu.VMEM((PAIR, 1, tq), F32),
                        pltpu.VMEM((PAIR, 1, tq), F32),
                        pltpu.VMEM((PAIR, HEAD_DIM, tq), F32),
                        pltpu.VMEM((PAIR, tk, tq), F32),
                        pltpu.VMEM((PAIR, tk, tq), F32),
                        pltpu.VMEM((PAIR, tk, tq), BF16)],
        compiler_params=_cparams("arbitrary", "arbitrary"),
        name=name,
    )(qkv16, ka, va, dcum)


def _fa_sample_kernel(q_ref, kn_ref, vn_ref, kc_ref, vc_ref, dc_ref, dp_ref, dn_ref, o_ref,
                      qr_ref, m_ref, l_ref, acc_ref, *, t, nkc):
    kc_i = pl.program_id(1)
    scale = HEAD_DIM ** -0.5
    rows = C_HEADS * t

    @pl.when(kc_i == 0)
    def _():
        qr_ref[...] = _head_rows(q_ref[...], C_HEADS)
        m_ref[...] = jnp.full(m_ref.shape, NEG_INF, F32)
        l_ref[...] = jnp.zeros(l_ref.shape, F32)
        acc_ref[...] = jnp.zeros(acc_ref.shape, F32)

    qr = qr_ref[...]
    d_tot = dp_ref[0][:, :, LANES - 1:LANES]

    def update(s, v):
        m = m_ref[...]
        m_new = jnp.maximum(m, jnp.max(s, axis=-1, keepdims=True))
        alpha = jnp.exp(m - m_new)
        p = jnp.exp(s - m_new)
        l_ref[...] = alpha * l_ref[...] + jnp.sum(p, axis=-1, keepdims=True)
        acc_ref[...] = alpha * acc_ref[...] + jnp.dot(p.astype(BF16), v, preferred_element_type=F32)
        m_ref[...] = m_new

    tk = kc_ref.shape[1]
    bias_c = jnp.broadcast_to(d_tot - dc_ref[0], (C_HEADS, t, tk)).reshape(rows, tk)
    s_c = lax.dot_general(qr, kc_ref[0].astype(BF16), NT_DIMS, preferred_element_type=F32) * scale + bias_c
    update(s_c, vc_ref[0].astype(BF16))

    @pl.when(kc_i == nkc - 1)
    def _():
        bias_n = jnp.broadcast_to(d_tot - dn_ref[0][:, :, 0:t], (C_HEADS, t, t)).reshape(rows, t)
        s_n = lax.dot_general(qr, kn_ref[...], NT_DIMS, preferred_element_type=F32) * scale + bias_n
        qpos = lax.broadcasted_iota(jnp.int32, (C_HEADS, t, t), 1).reshape(rows, t)
        kpos = lax.broadcasted_iota(jnp.int32, (C_HEADS, t, t), 2).reshape(rows, t)
        s_n = jnp.where(kpos <= qpos, s_n, NEG_INF)
        update(s_n, vn_ref[...])
        r = acc_ref[...] / l_ref[...]
        o_ref[...] = _head_diag(r, C_HEADS, t).astype(o_ref.dtype)


def forgetting_attention_sample(qkv, k_cache, v_cache, dall, *, row0, nb, t, tk, name):
    base = row0 // t
    past = k_cache.shape[1]
    nkc = past // tk
    rows = C_HEADS * t
    return pl.pallas_call(
        functools.partial(_fa_sample_kernel, t=t, nkc=nkc),
        grid=(nb, nkc),
        in_specs=[pl.BlockSpec((t, C_WIDTH), lambda b, c: (base + b, 0)),
                  pl.BlockSpec((t, C_WIDTH), lambda b, c: (base + b, 1)),
                  pl.BlockSpec((t, C_WIDTH), lambda b, c: (base + b, 2)),
                  pl.BlockSpec((1, tk, C_WIDTH), lambda b, c: (b, c, 0)),
                  pl.BlockSpec((1, tk, C_WIDTH), lambda b, c: (b, c, 0)),
                  pl.BlockSpec((1, C_HEADS, 1, tk), lambda b, c: (b, 0, 0, c)),
                  pl.BlockSpec((1, C_HEADS, 1, LANES), lambda b, c: (b, 0, 0, past // LANES - 1)),
                  pl.BlockSpec((1, C_HEADS, 1, LANES), lambda b, c: (b, 0, 0, past // LANES))],
        out_specs=pl.BlockSpec((t, C_WIDTH), lambda b, c: (b, 0)),
        out_shape=jax.ShapeDtypeStruct((nb * t, C_WIDTH), BF16),
        scratch_shapes=[pltpu.VMEM((rows, C_WIDTH), BF16),
                        pltpu.VMEM((rows, 1), F32),
                        pltpu.VMEM((rows, 1), F32),
                        pltpu.VMEM((rows, C_WIDTH), F32)],
        compiler_params=_cparams("arbitrary", "arbitrary"),
        name=name,
    )(qkv, qkv, qkv, k_cache, v_cache, dall, dall, dall)


def _rel_bias_matrix(rel_table, nq, nk):
    diag = jnp.arange(nq + nk - 1) - (nq - 1)
    vals = rel_table[:, jnp.clip(A_WINDOW - diag, -REL_CLIP, REL_CLIP) + REL_CLIP].astype(F32)
    rows = [lax.slice_in_dim(vals, nq - 1 - i, nq - 1 - i + nk, axis=1) for i in range(nq)]
    return jnp.stack(rows, axis=1)


def _band_block_bias(bias_chunk):
    nchunk = A_BLOCK // CHUNK
    parts = [jnp.pad(bias_chunk * LOG2E, ((0, 0), (0, 0), (CHUNK * c, A_KEYS - A_BAND - CHUNK * c)),
                     constant_values=NEG_INF) for c in range(nchunk)]
    later = jnp.stack(parts, axis=1).reshape(A_HEADS, A_BLOCK, A_KEYS)
    col = jnp.arange(A_KEYS)
    first = jnp.where(col >= A_WINDOW, later, NEG_INF)
    return jnp.stack([first, later])


def kernel(x_prompt, x_sample, cache_a_k, cache_a_v, state_pool, cache_c_k, cache_c_v, cache_c_logf,
           cache_mem_k, cache_mem_v, mem_prompt, w_in_ab, rel_bias_a, pool_w, pool_scale, w_out_ab,
           w_in_c, b_f, w_out_c, w_xq, w_xk, w_xv, w_xo, ln_g, ln_b, ffn_w1, ffn_w3, ffn_w2,
           w_router, b_router, moe_w1, moe_w3, moe_w2):
    bp, seq, d = x_prompt.shape
    nb, t, _ = x_sample.shape
    assert bp == 1 and d == D_MODEL
    ns = nb * t
    tot = seq + ns
    past = cache_c_k.shape[2]
    tm = 512
    assert tot % tm == 0 and seq % tm == 0

    x = jnp.concatenate([x_prompt.reshape(seq, d), x_sample.reshape(ns, d)], axis=0)

    mem = mem_prompt.reshape(N_MEM, d)
    p_mem_k, p_mem_v = [], []
    for layer in range(DEPTH):
        (mk,) = matmul(mem, w_xk[layer].astype(BF16), [F32], tm=N_MEM, tn=d, name=f"mem_k{layer}")
        (mv,) = matmul(mem, w_xv[layer].astype(BF16), [F32], tm=N_MEM, tn=d, name=f"mem_v{layer}")
        p_mem_k.append(mk)
        p_mem_v.append(mv)

    def cross_block(x, layer):
        (q,) = matmul(x, w_xq[layer].astype(BF16), [BF16], tm=tm, tn=d, name=f"xq{layer}")
        o_p = cross_attention(q, p_mem_k[layer].reshape(1, N_MEM, d), p_mem_v[layer].reshape(1, N_MEM, d),
                              row0=0, rows=seq, tq=tm, name=f"xattn_p{layer}")
        o_s = cross_attention(q, cache_mem_k[layer].reshape(nb, N_MEM, d).astype(BF16),
                              cache_mem_v[layer].reshape(nb, N_MEM, d).astype(BF16),
                              row0=seq, rows=ns, tq=t, name=f"xattn_s{layer}")
        return matmul_ln([(o_p, o_s)], w_xo[layer].astype(BF16), x, ln_g[layer, 1], ln_b[layer, 1], tm=tm,
                         name=f"xo{layer}")

    pr = 0
    (h,) = matmul(x, w_in_ab[pr].astype(BF16), [F32], tm=tm, tn=w_in_ab.shape[-1], name="in_ab")
    bias = _rel_bias_matrix(rel_bias_a[pr], CHUNK, A_BAND)
    oa_p = band_attention_prompt(h, _band_block_bias(bias), seq=seq, name="band_p")
    bias_s = bias[:, :t, :A_WINDOW + t].reshape(A_HEADS * t, A_WINDOW + t)
    oa_s = band_attention_sample(h, cache_a_k[pr].reshape(nb, A_WINDOW, A_WIDTH).astype(BF16),
                                 cache_a_v[pr].reshape(nb, A_WINDOW, A_WIDTH).astype(BF16),
                                 bias_s[:, :A_WINDOW], bias_s[:, A_WINDOW:], row0=seq, nb=nb, t=t, name="band_s")
    pw = pool_w[pr].astype(BF16)
    ob_p = pool_prompt(h, pw, pool_scale[pr], seq=seq, tm=tm, name="pool_p")
    hist = jnp.concatenate([jnp.zeros((nb, 1, B_WIDTH), F32), state_pool[pr]], axis=1)
    ob_s = pool_sample(h, hist, pw, pool_scale[pr], row0=seq, nb=nb, t=t, pos0=past, name="pool_s")
    x = matmul_ln([(oa_p, oa_s), (ob_p, ob_s)], w_out_ab[pr].astype(BF16), x, ln_g[0, 0], ln_b[0, 0], tm=tm,
                  name="out_ab")
    x = cross_block(x, 0)
    ones = jnp.ones((1, tot, 1), F32)
    x = moe_ln(x, ones, ffn_w1.astype(BF16), ffn_w3.astype(BF16), ffn_w2.astype(BF16),
               ln_g[0, 2], ln_b[0, 2], tm=tm, tf=D_FF // 2, name="ffn")

    k_a = h[:, A_WIDTH:2 * A_WIDTH]
    v_a = h[:, 2 * A_WIDTH:3 * A_WIDTH]
    u_b = h[:, 3 * A_WIDTH:]
    keep = min(A_WINDOW, seq)
    p_a_k = k_a[seq - keep:seq].reshape(1, 1, keep, A_HEADS, HEAD_DIM)
    p_a_v = v_a[seq - keep:seq].reshape(1, 1, keep, A_HEADS, HEAD_DIM)
    p_pool = u_b[seq - POOL_STATE:seq].reshape(1, 1, POOL_STATE, B_WIDTH)
    s_a_k = k_a[seq:].reshape(1, nb, t, A_HEADS, HEAD_DIM)
    s_a_v = v_a[seq:].reshape(1, nb, t, A_HEADS, HEAD_DIM)
    s_pool = jnp.concatenate([state_pool[pr], u_b[seq:].reshape(nb, t, B_WIDTH)], axis=1)[:, -POOL_STATE:][None]

    w_c = w_in_c[pr]
    w_cf = jnp.pad(w_c, ((0, 0), (0, LANES - C_HEADS))).astype(BF16)
    bf_pad = jnp.pad(b_f[pr].astype(F32), (0, LANES - C_HEADS)).reshape(1, LANES)
    qkv16, logf, k_cp, v_cp, k_cs, v_cs = matmul_qkv(x, w_cf, bf_pad, seq=seq, tm=tm, name="in_c")
    logf = logf[:, :C_HEADS]
    dcum_p = cumsum_rows(logf[:seq], name="cumsum_p")
    ka, va = fa_prep(qkv16, v_cp, dcum_p, seq=seq, tr=tm, name="fa_prep")
    oc_p = forgetting_attention_prompt(qkv16, ka, va, dcum_p, seq=seq, tq=tm, tk=tm, name="fa_p")
    lf_s = logf[seq:].reshape(nb, t, C_HEADS)
    lf_all = jnp.concatenate([cache_c_logf[pr].astype(F32), lf_s,
                              jnp.zeros((nb, CUMSUM_ALIGN - t, C_HEADS), F32)], axis=1)
    dall = cumsum_lanes(jnp.swapaxes(lf_all, 1, 2), name="cumsum_s").reshape(nb, C_HEADS, 1, past + CUMSUM_ALIGN)
    oc_s = forgetting_attention_sample(qkv16, cache_c_k[pr].reshape(nb, past, C_WIDTH).astype(BF16),
                                       cache_c_v[pr].reshape(nb, past, C_WIDTH).astype(BF16), dall,
                                       row0=seq, nb=nb, t=t, tk=2048, name="fa_s")
    x = matmul_ln([(oc_p, oc_s)], w_out_c[pr].astype(BF16), x, ln_g[1, 0], ln_b[1, 0], tm=tm, name="out_c")
    x = cross_block(x, 1)
    w_r = jnp.pad(w_router[pr], ((0, 0), (0, LANES - N_EXPERTS)))
    b_r = jnp.concatenate([b_router[pr].astype(F32), jnp.full((LANES - N_EXPERTS,), NEG_INF, F32)]).reshape(1, LANES)
    y_p, y_s = moe_routed_ln(x, w_r, b_r, moe_w1[pr].astype(BF16), moe_w3[pr].astype(BF16),
                             moe_w2[pr].astype(BF16), ln_g[1, 2], ln_b[1, 2], tm=tm, split=seq, name="moe")

    p_c_k = k_cp.reshape(1, 1, seq, C_HEADS, HEAD_DIM)
    p_c_v = v_cp.reshape(1, 1, seq, C_HEADS, HEAD_DIM)
    p_c_logf = logf[:seq].reshape(1, 1, seq, C_HEADS)
    s_c_k = k_cs.reshape(1, nb, t, C_HEADS, HEAD_DIM)
    s_c_v = v_cs.reshape(1, nb, t, C_HEADS, HEAD_DIM)
    s_c_logf = lf_s[None]

    y_prompt = y_p.reshape(1, seq, d)
    y_sample = y_s.reshape(nb, t, d)
    pmk = jnp.stack(p_mem_k).reshape(DEPTH, 1, N_MEM, X_HEADS, X_HEAD_DIM)
    pmv = jnp.stack(p_mem_v).reshape(DEPTH, 1, N_MEM, X_HEADS, X_HEAD_DIM)
    return (y_prompt, y_sample, p_a_k, p_a_v, p_pool, p_c_k, p_c_v, p_c_logf, pmk, pmv,
            s_a_k, s_a_v, s_pool, s_c_k, s_c_v, s_c_logf)
```

```python
import functools

import jax
import jax.numpy as jnp
from jax import lax
from jax.experimental import pallas as pl
from jax.experimental.pallas import tpu as pltpu

F32 = jnp.float32
BF16 = jnp.bfloat16

D_MODEL = 1024
DEPTH = 2
CHUNK = 64
N_MEM = 256
HEAD_DIM = 64
A_HEADS = 8
A_WIDTH = A_HEADS * HEAD_DIM
A_PREV_CHUNKS = 8
A_WINDOW = A_PREV_CHUNKS * CHUNK
A_BAND = (A_PREV_CHUNKS + 1) * CHUNK
REL_CLIP = 128
POOL_WINDOWS = (2, 4, 8, 16)
B_WIDTH = D_MODEL - A_WIDTH
B_GROUP = B_WIDTH // len(POOL_WINDOWS)
POOL_STATE = max(POOL_WINDOWS) - 1
C_HEADS = D_MODEL // HEAD_DIM
C_WIDTH = C_HEADS * HEAD_DIM
X_HEADS = 4
X_HEAD_DIM = D_MODEL // X_HEADS
D_FF = 2816
N_EXPERTS = 8
ALPHA = (2.0 * DEPTH) ** 0.25
LN_EPS = 1e-5
NEG_INF = -1e30
LOG2E = 1.4426950408889634

LANES = 128
PAIR = LANES // HEAD_DIM
VMEM_LIMIT = 56 * 1024 * 1024

NT_DIMS = (((1,), (1,)), ((), ()))


def _cparams(*sem):
    return pltpu.CompilerParams(dimension_semantics=sem, vmem_limit_bytes=VMEM_LIMIT)


def _deepnorm_ln(res, sub, g, b):
    z = ALPHA * res + sub
    mu = jnp.mean(z, axis=-1, keepdims=True)
    zc = z - mu
    var = jnp.mean(zc * zc, axis=-1, keepdims=True)
    return zc * lax.rsqrt(var + LN_EPS) * g + b


def _mm_kernel(x_ref, w_ref, *o_refs):
    y = jnp.dot(x_ref[...].astype(BF16), w_ref[...], preferred_element_type=F32)
    for o_ref in o_refs:
        o_ref[...] = y.astype(o_ref.dtype)


def matmul(x, w, out_dtypes, *, tm, tn, name):
    m, k = x.shape
    n = w.shape[1]
    outs = pl.pallas_call(
        _mm_kernel,
        grid=(n // tn, m // tm),
        in_specs=[pl.BlockSpec((tm, k), lambda j, i: (i, 0)),
                  pl.BlockSpec((k, tn), lambda j, i: (0, j))],
        out_specs=[pl.BlockSpec((tm, tn), lambda j, i: (i, j)) for _ in out_dtypes],
        out_shape=[jax.ShapeDtypeStruct((m, n), dt) for dt in out_dtypes],
        compiler_params=_cparams("arbitrary", "arbitrary"),
        name=name,
    )(x, w)
    return outs


def _mm_qkv_kernel(x_ref, w_ref, bf_ref, qkv_ref, lf_ref, kp_ref, vp_ref, ks_ref, vs_ref, *, n_prompt):
    i = pl.program_id(0)
    y = jnp.dot(x_ref[...].astype(BF16), w_ref[...], preferred_element_type=F32)
    qkv_ref[...] = y[:, :3 * C_WIDTH].astype(BF16)
    lf_ref[...] = jax.nn.log_sigmoid(y[:, 3 * C_WIDTH:] + bf_ref[...])
    k = y[:, C_WIDTH:2 * C_WIDTH]
    v = y[:, 2 * C_WIDTH:3 * C_WIDTH]

    @pl.when(i < n_prompt)
    def _():
        kp_ref[...] = k
        vp_ref[...] = v

    @pl.when(i >= n_prompt)
    def _():
        ks_ref[...] = k
        vs_ref[...] = v


def matmul_qkv(x, w, b_f, *, seq, tm, name):
    m, d = x.shape
    n = 3 * C_WIDTH
    n_prompt = seq // tm
    n_sample = (m - seq) // tm
    prompt_map = lambda i: (jnp.minimum(i, n_prompt - 1), 0)
    sample_map = lambda i: (jnp.maximum(i - n_prompt, 0), 0)
    return pl.pallas_call(
        functools.partial(_mm_qkv_kernel, n_prompt=n_prompt),
        grid=(n_prompt + n_sample,),
        in_specs=[pl.BlockSpec((tm, d), lambda i: (i, 0)),
                  pl.BlockSpec((d, n + LANES), lambda i: (0, 0)),
                  pl.BlockSpec((1, LANES), lambda i: (0, 0))],
        out_specs=[pl.BlockSpec((tm, n), lambda i: (i, 0)),
                   pl.BlockSpec((tm, LANES), lambda i: (i, 0)),
                   pl.BlockSpec((tm, C_WIDTH), prompt_map), pl.BlockSpec((tm, C_WIDTH), prompt_map),
                   pl.BlockSpec((tm, C_WIDTH), sample_map), pl.BlockSpec((tm, C_WIDTH), sample_map)],
        out_shape=[jax.ShapeDtypeStruct((m, n), BF16),
                   jax.ShapeDtypeStruct((m, LANES), F32),
                   jax.ShapeDtypeStruct((seq, C_WIDTH), F32), jax.ShapeDtypeStruct((seq, C_WIDTH), F32),
                   jax.ShapeDtypeStruct((m - seq, C_WIDTH), F32), jax.ShapeDtypeStruct((m - seq, C_WIDTH), F32)],
        compiler_params=_cparams("arbitrary"),
        name=name,
    )(x, w, b_f)


def _mm_ln_kernel(*refs, n_groups, n_first):
    x_refs = refs[:2 * n_groups]
    w_ref, res_ref, g_ref, b_ref, o_ref = refs[2 * n_groups:]
    i = pl.program_id(0)

    def run(xs):
        y, k0 = None, 0
        for x_ref in xs:
            kg = x_ref.shape[1]
            part = jnp.dot(x_ref[...].astype(BF16), w_ref[k0:k0 + kg, :], preferred_element_type=F32)
            y = part if y is None else y + part
            k0 += kg
        o_ref[...] = _deepnorm_ln(res_ref[...], y, g_ref[...], b_ref[...])

    @pl.when(i < n_first)
    def _():
        run(x_refs[0::2])

    @pl.when(i >= n_first)
    def _():
        run(x_refs[1::2])


def matmul_ln(groups, w, res, g, b, *, tm, name):
    m, n = res.shape
    split = groups[0][0].shape[0]
    n_first = split // tm
    first_map = lambda i: (jnp.minimum(i, n_first - 1), 0)
    rest_map = lambda i: (jnp.maximum(i - n_first, 0), 0)
    x_specs, x_args = [], []
    for xf, xr in groups:
        x_specs += [pl.BlockSpec((tm, xf.shape[1]), first_map), pl.BlockSpec((tm, xr.shape[1]), rest_map)]
        x_args += [xf, xr]
    return pl.pallas_call(
        functools.partial(_mm_ln_kernel, n_groups=len(groups), n_first=n_first),
        grid=(m // tm,),
        in_specs=x_specs + [pl.BlockSpec(w.shape, lambda i: (0, 0)),
                            pl.BlockSpec((tm, n), lambda i: (i, 0)),
                            pl.BlockSpec((1, n), lambda i: (0, 0)),
                            pl.BlockSpec((1, n), lambda i: (0, 0))],
        out_specs=pl.BlockSpec((tm, n), lambda i: (i, 0)),
        out_shape=jax.ShapeDtypeStruct((m, n), F32),
        compiler_params=_cparams("arbitrary"),
        name=name,
    )(*x_args, w, res, g.reshape(1, n), b.reshape(1, n))


def _moe_kernel(x_ref, comb_ref, w1_ref, w3_ref, w2_ref, g_ref, b_ref, o_ref, acc_ref, *, ne, nf):
    e = pl.program_id(1)
    f = pl.program_id(2)

    @pl.when((e == 0) & (f == 0))
    def _():
        acc_ref[...] = jnp.zeros_like(acc_ref)

    x = x_ref[...].astype(BF16)
    h1 = jnp.dot(x, w1_ref[0], preferred_element_type=F32)
    h3 = jnp.dot(x, w3_ref[0], preferred_element_type=F32)
    a = (jax.nn.silu(h1) * h3).astype(BF16)
    y = jnp.dot(a, w2_ref[0], preferred_element_type=F32)
    acc_ref[...] += comb_ref[0] * y

    @pl.when((e == ne - 1) & (f == nf - 1))
    def _():
        o_ref[...] = _deepnorm_ln(x_ref[...], acc_ref[...], g_ref[...], b_ref[...])


def moe_ln(x, comb, w1, w3, w2, g, b, *, tm, tf, name):
    m, d = x.shape
    ne, _, dff = w1.shape
    nf = dff // tf
    return pl.pallas_call(
        functools.partial(_moe_kernel, ne=ne, nf=nf),
        grid=(m // tm, ne, nf),
        in_specs=[pl.BlockSpec((tm, d), lambda i, e, f: (i, 0)),
                  pl.BlockSpec((1, tm, 1), lambda i, e, f: (e, i, 0)),
                  pl.BlockSpec((1, d, tf), lambda i, e, f: (e, 0, f)),
                  pl.BlockSpec((1, d, tf), lambda i, e, f: (e, 0, f)),
                  pl.BlockSpec((1, tf, d), lambda i, e, f: (e, f, 0)),
                  pl.BlockSpec((1, d), lambda i, e, f: (0, 0)),
                  pl.BlockSpec((1, d), lambda i, e, f: (0, 0))],
        out_specs=pl.BlockSpec((tm, d), lambda i, e, f: (i, 0)),
        out_shape=jax.ShapeDtypeStruct((m, d), F32),
        scratch_shapes=[pltpu.VMEM((tm, d), F32)],
        compiler_params=_cparams("arbitrary", "arbitrary", "arbitrary"),
        name=name,
    )(x, comb, w1, w3, w2, g.reshape(1, d), b.reshape(1, d))


ROUTE_I1, ROUTE_I2, ROUTE_G1, ROUTE_G2, ROUTE_R1, ROUTE_R2 = range(6)
EXPERT_TILE = 256


def _router_kernel(x_ref, w_ref, b_ref, o_ref, cnt_ref, carry_ref):
    i = pl.program_id(0)

    @pl.when(i == 0)
    def _():
        carry_ref[...] = jnp.zeros_like(carry_ref)

    tm = x_ref.shape[0]
    logits = jnp.dot(x_ref[...], w_ref[...], preferred_element_type=F32,
                     precision=lax.Precision.HIGHEST) + b_ref[...]
    lane = lax.broadcasted_iota(jnp.int32, logits.shape, 1)
    big = jnp.int32(LANES)
    v1 = jnp.max(logits, axis=-1, keepdims=True)
    i1 = jnp.min(jnp.where(logits == v1, lane, big), axis=-1, keepdims=True)
    rest = jnp.where(lane == i1, NEG_INF, logits)
    v2 = jnp.max(rest, axis=-1, keepdims=True)
    i2 = jnp.min(jnp.where(rest == v2, lane, big), axis=-1, keepdims=True)
    e2 = jnp.exp(v2 - v1)
    den = 1.0 + e2
    oh1 = lane == i1
    oh2 = lane == i2
    cnt = jnp.where(oh1 | oh2, 1.0, 0.0)
    r = lax.broadcasted_iota(jnp.int32, (tm, tm), 0)
    c = lax.broadcasted_iota(jnp.int32, (tm, tm), 1)
    earlier = jnp.where(c < r, 1.0, 0.0).astype(BF16)
    before = jnp.dot(earlier, cnt.astype(BF16), preferred_element_type=F32) + carry_ref[...]
    r1 = jnp.sum(jnp.where(oh1, before, 0.0), axis=-1, keepdims=True)
    r2 = jnp.sum(jnp.where(oh2, before, 0.0), axis=-1, keepdims=True)
    carry_ref[...] += jnp.sum(cnt, axis=0, keepdims=True)
    cnt_ref[...] = carry_ref[...]
    out = jnp.zeros(logits.shape, F32)
    for ln, val in ((ROUTE_I1, i1.astype(F32)), (ROUTE_I2, i2.astype(F32)), (ROUTE_G1, 1.0 / den),
                    (ROUTE_G2, e2 / den), (ROUTE_R1, r1), (ROUTE_R2, r2)):
        out = jnp.where(lane == ln, val, out)
    o_ref[...] = out


def router(x, w_pad, b_pad, *, tm, name):
    m, d = x.shape
    return pl.pallas_call(
        _router_kernel,
        grid=(m // tm,),
        in_specs=[pl.BlockSpec((tm, d), lambda i: (i, 0)),
                  pl.BlockSpec((d, LANES), lambda i: (0, 0)),
                  pl.BlockSpec((1, LANES), lambda i: (0, 0))],
        out_specs=[pl.BlockSpec((tm, LANES), lambda i: (i, 0)),
                   pl.BlockSpec((1, LANES), lambda i: (0, 0))],
        out_shape=[jax.ShapeDtypeStruct((m, LANES), F32), jax.ShapeDtypeStruct((1, LANES), F32)],
        scratch_shapes=[pltpu.VMEM((1, LANES), F32)],
        compiler_params=_cparams("arbitrary"),
        name=name,
    )(x, w_pad, b_pad)


def _gather_rows_start(src_hbm, idx_ref, n, dst, sem):
    def issue(r, carry):
        pltpu.make_async_copy(src_hbm.at[pl.ds(idx_ref[0, 0, r], 1)], dst.at[pl.ds(r, 1)], sem).start()
        return carry

    lax.fori_loop(0, n, issue, 0, unroll=8)


def _gather_rows_wait(src_hbm, n, dst, sem):
    pltpu.make_async_copy(src_hbm.at[pl.ds(0, n)], dst, sem).wait()


def _expert_kernel(te_ref, nt_ref, idx0_ref, idxn_ref, x_hbm, w1_ref, w3_ref, w2_ref, o_ref, xbuf, sem, *, tm, nf):
    m = pl.program_id(0)
    nt = nt_ref[0]
    slot = m % 2

    @pl.when(m == 0)
    def _():
        _gather_rows_start(x_hbm, idx0_ref, tm, xbuf.at[0], sem.at[0])

    @pl.when(m + 1 < nt)
    def _():
        _gather_rows_start(x_hbm, idxn_ref, tm, xbuf.at[1 - slot], sem.at[1 - slot])

    @pl.when(m < nt)
    def _():
        _gather_rows_wait(x_hbm, tm, xbuf.at[slot], sem.at[slot])
        x = xbuf[slot].astype(BF16)
        tf = w1_ref.shape[2] // nf
        y = None
        for f in range(nf):
            sl = slice(f * tf, (f + 1) * tf)
            h1 = jnp.dot(x, w1_ref[0, :, sl], preferred_element_type=F32)
            h3 = jnp.dot(x, w3_ref[0, :, sl], preferred_element_type=F32)
            a = (jax.nn.silu(h1) * h3).astype(BF16)
            part = jnp.dot(a, w2_ref[0, sl, :], preferred_element_type=F32)
            y = part if y is None else y + part
        o_ref[...] = y

    @pl.when(m >= nt)
    def _():
        o_ref[...] = jnp.zeros_like(o_ref)


def expert_tiles(x, w1, w3, w2, tile_expert, n_tiles, src_rows, *, tm, nf, name):
    m, d = x.shape
    ne, _, dff = w1.shape
    nt_max = src_rows.shape[0]

    def wmap(i, te, nt):
        return (te[jnp.minimum(i, nt[0] - 1)], 0, 0)

    grid_spec = pltpu.PrefetchScalarGridSpec(
        num_scalar_prefetch=2,
        grid=(nt_max,),
        in_specs=[pl.BlockSpec((1, 1, tm), lambda i, te, nt: (0, 0, 0), memory_space=pltpu.SMEM),
                  pl.BlockSpec((1, 1, tm), lambda i, te, nt: (jnp.minimum(i + 1, nt_max - 1), 0, 0),
                               memory_space=pltpu.SMEM),
                  pl.BlockSpec(memory_space=pl.ANY),
                  pl.BlockSpec((1, d, dff), wmap),
                  pl.BlockSpec((1, d, dff), wmap),
                  pl.BlockSpec((1, dff, d), wmap)],
        out_specs=pl.BlockSpec((tm, d), lambda i, te, nt: (i, 0)),
        scratch_shapes=[pltpu.VMEM((2, tm, d), F32), pltpu.SemaphoreType.DMA((2,))],
    )
    return pl.pallas_call(
        functools.partial(_expert_kernel, tm=tm, nf=nf),
        grid_spec=grid_spec,
        out_shape=jax.ShapeDtypeStruct((nt_max * tm, d), F32),
        compiler_params=_cparams("arbitrary"),
        name=name,
    )(tile_expert, n_tiles, src_rows, src_rows, x, w1, w3, w2)


def _combine_ln_kernel(idx0_ref, idxn_ref, y_hbm, x_ref, route_ref, g_ref, b_ref, op_ref, os_ref, ybuf, sem,
                       *, tm, nsteps, n_first):
    i = pl.program_id(0)
    slot = i % 2

    @pl.when(i == 0)
    def _():
        _gather_rows_start(y_hbm, idx0_ref, 2 * tm, ybuf.at[0], sem.at[0])

    @pl.when(i + 1 < nsteps)
    def _():
        _gather_rows_start(y_hbm, idxn_ref, 2 * tm, ybuf.at[1 - slot], sem.at[1 - slot])

    _gather_rows_wait(y_hbm, 2 * tm, ybuf.at[slot], sem.at[slot])
    route = route_ref[...]
    g1 = route[:, ROUTE_G1:ROUTE_G1 + 1]
    g2 = route[:, ROUTE_G2:ROUTE_G2 + 1]
    y = g1 * ybuf[slot, 0:tm, :] + g2 * ybuf[slot, tm:2 * tm, :]
    out = _deepnorm_ln(x_ref[...], y, g_ref[...], b_ref[...])

    @pl.when(i < n_first)
    def _():
        op_ref[...] = out

    @pl.when(i >= n_first)
    def _():
        os_ref[...] = out


def combine_ln(x, route, y_sorted, pos_rows, g, b, *, tm, split, name):
    m, d = x.shape
    nsteps = m // tm
    n_first = split // tm
    return pl.pallas_call(
        functools.partial(_combine_ln_kernel, tm=tm, nsteps=nsteps, n_first=n_first),
        grid=(nsteps,),
        in_specs=[pl.BlockSpec((1, 1, 2 * tm), lambda i: (0, 0, 0), memory_space=pltpu.SMEM),
                  pl.BlockSpec((1, 1, 2 * tm), lambda i: (jnp.minimum(i + 1, nsteps - 1), 0, 0),
                               memory_space=pltpu.SMEM),
                  pl.BlockSpec(memory_space=pl.ANY),
                  pl.BlockSpec((tm, d), lambda i: (i, 0)),
                  pl.BlockSpec((tm, LANES), lambda i: (i, 0)),
                  pl.BlockSpec((1, d), lambda i: (0, 0)),
                  pl.BlockSpec((1, d), lambda i: (0, 0))],
        out_specs=[pl.BlockSpec((tm, d), lambda i: (jnp.minimum(i, n_first - 1), 0)),
                   pl.BlockSpec((tm, d), lambda i: (jnp.maximum(i - n_first, 0), 0))],
        out_shape=[jax.ShapeDtypeStruct((split, d), F32), jax.ShapeDtypeStruct((m - split, d), F32)],
        scratch_shapes=[pltpu.VMEM((2, 2 * tm, d), F32), pltpu.SemaphoreType.DMA((2,))],
        compiler_params=_cparams("arbitrary"),
        name=name,
    )(pos_rows, pos_rows, y_sorted, x, route, g.reshape(1, d), b.reshape(1, d))


def moe_routed_ln(x, w_router_pad, b_router_pad, w1, w3, w2, g, b, *, tm, split, name):
    m, d = x.shape
    ne = w1.shape[0]
    te_rows = EXPERT_TILE
    route, counts = router(x, w_router_pad, b_router_pad, tm=tm, name=name + "_router")
    counts = counts[0, :ne].astype(jnp.int32)
    padded = (counts + te_rows - 1) // te_rows * te_rows
    upto = jnp.arange(ne)[None, :] <= jnp.arange(ne)[:, None]
    ends = jnp.sum(jnp.where(upto, padded[None, :], 0), axis=1)
    base = ends - padded
    i1 = route[:, ROUTE_I1].astype(jnp.int32)
    i2 = route[:, ROUTE_I2].astype(jnp.int32)
    pos1 = base[i1] + route[:, ROUTE_R1].astype(jnp.int32)
    pos2 = base[i2] + route[:, ROUTE_R2].astype(jnp.int32)
    nt_max = (2 * m + ne * (te_rows - 1)) // te_rows + 1
    tok = jnp.arange(m, dtype=jnp.int32)
    src = jnp.zeros((nt_max * te_rows,), jnp.int32).at[pos1].set(tok).at[pos2].set(tok)
    n_tiles = (ends[-1] // te_rows).astype(jnp.int32).reshape(1)
    tile_start = jnp.arange(nt_max, dtype=jnp.int32) * te_rows
    tile_expert = jnp.minimum(jnp.sum(tile_start[:, None] >= ends[None, :], axis=1), ne - 1).astype(jnp.int32)
    y_sorted = expert_tiles(x, w1, w3, w2, tile_expert, n_tiles, src.reshape(nt_max, 1, te_rows),
                            tm=te_rows, nf=2, name=name + "_experts")
    pos_rows = jnp.concatenate([pos1.reshape(m // tm, 1, tm), pos2.reshape(m // tm, 1, tm)], axis=-1)
    return combine_ln(x, route, y_sorted, pos_rows, g, b, tm=tm, split=split, name=name + "_combine")


def _xattn_kernel(q_ref, mk_ref, mv_ref, o_ref):
    scale = X_HEAD_DIM ** -0.5
    for h in range(X_HEADS):
        sl = slice(h * X_HEAD_DIM, (h + 1) * X_HEAD_DIM)
        q = q_ref[:, sl]
        k = mk_ref[0, :, sl].astype(BF16)
        v = mv_ref[0, :, sl].astype(BF16)
        s = lax.dot_general(q, k, NT_DIMS, preferred_element_type=F32) * scale
        m = jnp.max(s, axis=-1, keepdims=True)
        p = jnp.exp(s - m)
        l = jnp.sum(p, axis=-1, keepdims=True)
        o = jnp.dot(p.astype(BF16), v, preferred_element_type=F32) / l
        o_ref[:, sl] = o.astype(o_ref.dtype)


def cross_attention(q, mem_k, mem_v, *, row0, rows, tq, name):
    d = q.shape[1]
    nb = mem_k.shape[0]
    per_b = rows // nb
    nq = per_b // tq
    base = row0 // tq
    return pl.pallas_call(
        _xattn_kernel,
        grid=(nb, nq),
        in_specs=[pl.BlockSpec((tq, d), lambda b, i: (base + b * nq + i, 0)),
                  pl.BlockSpec((1, N_MEM, d), lambda b, i: (b, 0, 0)),
                  pl.BlockSpec((1, N_MEM, d), lambda b, i: (b, 0, 0))],
        out_specs=pl.BlockSpec((tq, d), lambda b, i: (b * nq + i, 0)),
        out_shape=jax.ShapeDtypeStruct((rows, d), BF16),
        compiler_params=_cparams("arbitrary", "arbitrary"),
        name=name,
    )(q, mem_k, mem_v)


A_BLOCK = A_WINDOW
A_KEYS = A_WINDOW + A_BLOCK


def _head_lane_masks():
    lane = lax.broadcasted_iota(jnp.int32, (1, LANES), 1)
    return [(lane >= a * HEAD_DIM) & (lane < (a + 1) * HEAD_DIM) for a in range(PAIR)]


def _band_prompt_kernel(q_ref, kp_ref, kc_ref, vp_ref, vc_ref, bias_ref, o_ref):
    kk = jnp.concatenate([kp_ref[...].astype(BF16), kc_ref[...].astype(BF16)], axis=0)
    vv = jnp.concatenate([vp_ref[...].astype(BF16), vc_ref[...].astype(BF16)], axis=0)
    masks = _head_lane_masks()
    q = q_ref[...] * (HEAD_DIM ** -0.5 * LOG2E)
    out = None
    for a in range(PAIR):
        qm = jnp.where(masks[a], q, 0.0).astype(BF16)
        s = lax.dot_general(qm, kk, NT_DIMS, preferred_element_type=F32) + bias_ref[0, a]
        m = jnp.max(s, axis=-1, keepdims=True)
        p = jnp.exp2(s - m)
        l = jnp.sum(p, axis=-1, keepdims=True)
        o = jnp.dot(p.astype(BF16), vv, preferred_element_type=F32) / l
        out = o if out is None else jnp.where(masks[a], o, out)
    o_ref[...] = out.astype(o_ref.dtype)


def band_attention_prompt(h, bias, *, seq, name):
    nblk = seq // A_BLOCK
    npair = A_WIDTH // LANES
    qcol, kcol, vcol = 0, npair, 2 * npair
    return pl.pallas_call(
        _band_prompt_kernel,
        grid=(npair, nblk),
        in_specs=[pl.BlockSpec((A_BLOCK, LANES), lambda p, i: (i, qcol + p)),
                  pl.BlockSpec((A_BLOCK, LANES), lambda p, i: (jnp.maximum(i - 1, 0), kcol + p)),
                  pl.BlockSpec((A_BLOCK, LANES), lambda p, i: (i, kcol + p)),
                  pl.BlockSpec((A_BLOCK, LANES), lambda p, i: (jnp.maximum(i - 1, 0), vcol + p)),
                  pl.BlockSpec((A_BLOCK, LANES), lambda p, i: (i, vcol + p)),
                  pl.BlockSpec((1, PAIR, A_BLOCK, A_KEYS), lambda p, i: (jnp.minimum(i, 1), p, 0, 0))],
        out_specs=pl.BlockSpec((A_BLOCK, LANES), lambda p, i: (i, p)),
        out_shape=jax.ShapeDtypeStruct((seq, A_WIDTH), BF16),
        compiler_params=_cparams("arbitrary", "arbitrary"),
        name=name,
    )(h, h, h, h, h, bias)


def _head_rows(x, nheads):
    t, w = x.shape
    x3 = jnp.broadcast_to(x[None], (nheads, t, w))
    hh = lax.broadcasted_iota(jnp.int32, (nheads, 1, w), 0)
    cc = lax.broadcasted_iota(jnp.int32, (nheads, 1, w), 2) // HEAD_DIM
    return jnp.where(hh == cc, x3, jnp.zeros_like(x3)).reshape(nheads * t, w)


def _head_diag(r, nheads, t):
    w = r.shape[1]
    r3 = r.reshape(nheads, t, w)
    hh = lax.broadcasted_iota(jnp.int32, (nheads, 1, w), 0)
    cc = lax.broadcasted_iota(jnp.int32, (nheads, 1, w), 2) // HEAD_DIM
    return jnp.sum(jnp.where(hh == cc, r3, 0.0), axis=0)


def _band_sample_kernel(q_ref, kn_ref, vn_ref, kc_ref, vc_ref, bc_ref, bn_ref, o_ref):
    scale = HEAD_DIM ** -0.5
    t = q_ref.shape[0]
    qr = _head_rows(q_ref[...], A_HEADS).astype(BF16)
    kc = kc_ref[0].astype(BF16)
    vc = vc_ref[0].astype(BF16)
    kn = kn_ref[...].astype(BF16)
    vn = vn_ref[...].astype(BF16)
    s_c = lax.dot_general(qr, kc, NT_DIMS, preferred_element_type=F32) * scale + bc_ref[...]
    s_n = lax.dot_general(qr, kn, NT_DIMS, preferred_element_type=F32) * scale + bn_ref[...]
    m = jnp.maximum(jnp.max(s_c, axis=-1, keepdims=True), jnp.max(s_n, axis=-1, keepdims=True))
    p_c = jnp.exp(s_c - m)
    p_n = jnp.exp(s_n - m)
    l = jnp.sum(p_c, axis=-1, keepdims=True) + jnp.sum(p_n, axis=-1, keepdims=True)
    r = (jnp.dot(p_c.astype(BF16), vc, preferred_element_type=F32)
         + jnp.dot(p_n.astype(BF16), vn, preferred_element_type=F32)) / l
    o_ref[...] = _head_diag(r, A_HEADS, t).astype(o_ref.dtype)


def band_attention_sample(h, k_cache, v_cache, bias_c, bias_n, *, row0, nb, t, name):
    base = row0 // t
    w = k_cache.shape[1]
    return pl.pallas_call(
        _band_sample_kernel,
        grid=(nb,),
        in_specs=[pl.BlockSpec((t, A_WIDTH), lambda b: (base + b, 0)),
                  pl.BlockSpec((t, A_WIDTH), lambda b: (base + b, 1)),
                  pl.BlockSpec((t, A_WIDTH), lambda b: (base + b, 2)),
                  pl.BlockSpec((1, w, A_WIDTH), lambda b: (b, 0, 0)),
                  pl.BlockSpec((1, w, A_WIDTH), lambda b: (b, 0, 0)),
                  pl.BlockSpec((A_HEADS * t, w), lambda b: (0, 0)),
                  pl.BlockSpec((A_HEADS * t, t), lambda b: (0, 0))],
        out_specs=pl.BlockSpec((t, A_WIDTH), lambda b: (b, 0)),
        out_shape=jax.ShapeDtypeStruct((nb * t, A_WIDTH), BF16),
        compiler_params=_cparams("arbitrary"),
        name=name,
    )(h, h, h, k_cache, v_cache, bias_c, bias_n)


POOL_HALO = POOL_STATE + 1


def _pool_core(ext_ref, rows, pos0, pw_ref, sc_ref):
    pos = (pos0 + lax.broadcasted_iota(jnp.int32, (rows, 1), 0)).astype(F32)
    outs = []
    for g, w in enumerate(POOL_WINDOWS):
        sl = slice(g * B_GROUP, (g + 1) * B_GROUP)
        cur = ext_ref[POOL_HALO:POOL_HALO + rows, sl]
        win = cur
        for r in range(1, w):
            win = win + ext_ref[POOL_HALO - r:POOL_HALO - r + rows, sl]
        cnt = jnp.minimum(float(w), pos + 1.0)
        d = (win / cnt - cur).astype(BF16)
        outs.append(jnp.dot(d, pw_ref[g], preferred_element_type=F32))
    return jnp.concatenate(outs, axis=-1) * sc_ref[...]


def _pool_prompt_kernel(up_ref, uc_ref, pw_ref, sc_ref, o_ref, ext_ref, *, tm):
    i = pl.program_id(0)
    ext_ref[0:POOL_HALO, :] = jnp.where(i > 0, up_ref[...], 0.0)
    ext_ref[POOL_HALO:POOL_HALO + tm, :] = uc_ref[...]
    o_ref[...] = _pool_core(ext_ref, tm, i * tm, pw_ref, sc_ref).astype(o_ref.dtype)


def pool_prompt(h, pool_w, pool_scale, *, seq, tm, name):
    ucol = 3 * A_WIDTH // B_WIDTH
    per = tm // POOL_HALO
    return pl.pallas_call(
        functools.partial(_pool_prompt_kernel, tm=tm),
        grid=(seq // tm,),
        in_specs=[pl.BlockSpec((POOL_HALO, B_WIDTH), lambda i: (jnp.maximum(i * per - 1, 0), ucol)),
                  pl.BlockSpec((tm, B_WIDTH), lambda i: (i, ucol)),
                  pl.BlockSpec((len(POOL_WINDOWS), B_GROUP, B_GROUP), lambda i: (0, 0, 0)),
                  pl.BlockSpec((1, B_WIDTH), lambda i: (0, 0))],
        out_specs=pl.BlockSpec((tm, B_WIDTH), lambda i: (i, 0)),
        out_shape=jax.ShapeDtypeStruct((seq, B_WIDTH), BF16),
        scratch_shapes=[pltpu.VMEM((POOL_HALO + tm, B_WIDTH), F32)],
        compiler_params=_cparams("arbitrary"),
        name=name,
    )(h, h, pool_w, pool_scale.reshape(1, B_WIDTH))


def _pool_sample_kernel(hist_ref, u_ref, pw_ref, sc_ref, o_ref, ext_ref, *, t, pos0):
    ext_ref[0:POOL_HALO, :] = hist_ref[0]
    ext_ref[POOL_HALO:POOL_HALO + t, :] = u_ref[...]
    o_ref[...] = _pool_core(ext_ref, t, pos0, pw_ref, sc_ref).astype(o_ref.dtype)


def pool_sample(h, hist, pool_w, pool_scale, *, row0, nb, t, pos0, name):
    ucol = 3 * A_WIDTH // B_WIDTH
    base = row0 // t
    return pl.pallas_call(
        functools.partial(_pool_sample_kernel, t=t, pos0=pos0),
        grid=(nb,),
        in_specs=[pl.BlockSpec((1, POOL_HALO, B_WIDTH), lambda b: (b, 0, 0)),
                  pl.BlockSpec((t, B_WIDTH), lambda b: (base + b, ucol)),
                  pl.BlockSpec((len(POOL_WINDOWS), B_GROUP, B_GROUP), lambda b: (0, 0, 0)),
                  pl.BlockSpec((1, B_WIDTH), lambda b: (0, 0))],
        out_specs=pl.BlockSpec((t, B_WIDTH), lambda b: (b, 0)),
        out_shape=jax.ShapeDtypeStruct((nb * t, B_WIDTH), BF16),
        scratch_shapes=[pltpu.VMEM((POOL_HALO + t, B_WIDTH), F32)],
        compiler_params=_cparams("arbitrary"),
        name=name,
    )(hist, h, pool_w, pool_scale.reshape(1, B_WIDTH))


CUMSUM_ALIGN = 8 * LANES


def _cumsum_kernel(x_ref, before_ref, o_ref):
    rows = x_ref.shape[1]
    r = lax.broadcasted_iota(jnp.int32, (LANES, LANES), 0)
    c = lax.broadcasted_iota(jnp.int32, (LANES, LANES), 1)
    upper = (r <= c).astype(F32)
    local = jnp.dot(x_ref[0], upper, preferred_element_type=F32, precision=lax.Precision.HIGHEST)
    tot = jnp.broadcast_to(local[:, LANES - 1:LANES], (rows, LANES))
    o_ref[0] = local + jnp.dot(before_ref[...], tot, preferred_element_type=F32, precision=lax.Precision.HIGHEST)


def cumsum_lanes(x, *, name):
    nb, nh, ln = x.shape
    nblk = ln // LANES
    rows = nh * nblk
    rid = jnp.arange(rows)
    before = ((rid[:, None] // nblk == rid[None, :] // nblk) & (rid[None, :] < rid[:, None])).astype(F32)
    out = pl.pallas_call(
        _cumsum_kernel,
        grid=(nb,),
        in_specs=[pl.BlockSpec((1, rows, LANES), lambda b: (b, 0, 0)),
                  pl.BlockSpec((rows, rows), lambda b: (0, 0))],
        out_specs=pl.BlockSpec((1, rows, LANES), lambda b: (b, 0, 0)),
        out_shape=jax.ShapeDtypeStruct((nb, rows, LANES), F32),
        compiler_params=_cparams("arbitrary"),
        name=name,
    )(x.reshape(nb, rows, LANES), before)
    return out.reshape(nb, nh, ln)


def _cumsum_rows_kernel(x_ref, o_ref, offs_ref, *, nblk):
    r = lax.broadcasted_iota(jnp.int32, (LANES, LANES), 0)
    c = lax.broadcasted_iota(jnp.int32, (LANES, LANES), 1)
    lower = (c <= r).astype(F32)

    def local(b, carry):
        off = pl.multiple_of(b * LANES, LANES)
        o_ref[pl.ds(off, LANES), :] = jnp.dot(lower, x_ref[pl.ds(off, LANES), :], preferred_element_type=F32,
                                              precision=lax.Precision.HIGHEST)
        return carry

    lax.fori_loop(0, nblk, local, 0)
    totals = o_ref[pl.ds(LANES - 1, nblk, stride=LANES), :]
    rb = lax.broadcasted_iota(jnp.int32, (nblk, nblk), 0)
    cb = lax.broadcasted_iota(jnp.int32, (nblk, nblk), 1)
    offs_ref[...] = jnp.dot((cb < rb).astype(F32), totals, preferred_element_type=F32,
                            precision=lax.Precision.HIGHEST)

    def shift(b, carry):
        off = pl.multiple_of(b * LANES, LANES)
        o_ref[pl.ds(off, LANES), :] = o_ref[pl.ds(off, LANES), :] + offs_ref[pl.ds(b, 1), :]
        return carry

    lax.fori_loop(0, nblk, shift, 0)


def cumsum_rows(x, *, name):
    ln, nh = x.shape
    nblk = ln // LANES
    return pl.pallas_call(
        functools.partial(_cumsum_rows_kernel, nblk=nblk),
        grid=(1,),
        in_specs=[pl.BlockSpec((ln, nh), lambda i: (0, 0))],
        out_specs=pl.BlockSpec((ln, nh), lambda i: (0, 0)),
        out_shape=jax.ShapeDtypeStruct((ln, nh), F32),
        scratch_shapes=[pltpu.VMEM((nblk, nh), F32)],
        compiler_params=_cparams("arbitrary"),
        name=name,
    )(x)


FA_CHUNK = 64
FA_QCOLS = 256


def _split3(x):
    hi = x.astype(BF16)
    r = x - hi.astype(F32)
    mid = r.astype(BF16)
    lo = (r - mid.astype(F32)).astype(BF16)
    return hi, mid, lo


def _spare_lane_columns(lane, base, first, second):
    out = jnp.zeros(jnp.broadcast_shapes(lane.shape, first[0].shape, second[0].shape), F32)
    for j, val in enumerate(tuple(first) + tuple(second)):
        out = jnp.where(lane == base + j, val.astype(F32), out)
    return out


def _fa_prep_kernel(k_ref, v_ref, d_ref, ka_ref, va_ref):
    lane = lax.broadcasted_iota(jnp.int32, (1, LANES), 1)
    masks = _head_lane_masks()
    d = d_ref[...] * (-LOG2E)
    one = jnp.ones((1, 1), BF16)
    for h in range(C_HEADS):
        p, a = divmod(h, PAIR)
        sl = slice(p * LANES, (p + 1) * LANES)
        ext = _spare_lane_columns(lane, HEAD_DIM * (1 - a), _split3(d[:, h:h + 1]), (one, one, one))
        ka_ref[h] = jnp.where(masks[a], k_ref[:, sl].astype(F32), ext).astype(BF16)
        if a == 0:
            va_ref[p] = v_ref[:, sl].T.astype(BF16)


def fa_prep(qkv16, v32, dcum, *, seq, tr, name):
    kcol = 1
    npair = C_HEADS // PAIR
    return pl.pallas_call(
        _fa_prep_kernel,
        grid=(seq // tr,),
        in_specs=[pl.BlockSpec((tr, C_WIDTH), lambda r: (r, kcol)),
                  pl.BlockSpec((tr, C_WIDTH), lambda r: (r, 0)),
                  pl.BlockSpec((tr, C_HEADS), lambda r: (r, 0))],
        out_specs=[pl.BlockSpec((C_HEADS, tr, LANES), lambda r: (0, r, 0)),
                   pl.BlockSpec((npair, LANES, tr), lambda r: (0, 0, r))],
        out_shape=[jax.ShapeDtypeStruct((C_HEADS, seq, LANES), BF16),
                   jax.ShapeDtypeStruct((npair, LANES, seq), BF16)],
        compiler_params=_cparams("arbitrary"),
        name=name,
    )(qkv16, v32, dcum)


def _fa_prompt_kernel(q_ref, ka_ref, va_ref, d_ref, o_ref, m_ref, l_ref, acc_ref, s0_ref, s1_ref, p_ref, *, tq, tk):
    p = pl.program_id(0)
    qi = pl.program_id(1)
    lane = lax.broadcasted_iota(jnp.int32, (1, LANES), 1)
    head_col = lax.broadcasted_iota(jnp.int32, (1, C_HEADS), 1)
    masks = _head_lane_masks()
    q = q_ref[...].astype(F32) * (HEAD_DIM ** -0.5 * LOG2E)
    d0 = d_ref[0:1, :] * LOG2E
    one = jnp.ones((1, 1), BF16)
    qa = []
    for a in range(PAIR):
        dref = jnp.sum(jnp.where(head_col == PAIR * p + a, d0, 0.0), axis=-1, keepdims=True)
        ext = _spare_lane_columns(lane, HEAD_DIM * (1 - a), (one, one, one), _split3(dref))
        qa.append(jnp.where(masks[a], q, ext).astype(BF16))

    m_ref[...] = jnp.full(m_ref.shape, NEG_INF, F32)
    l_ref[...] = jnp.zeros(l_ref.shape, F32)
    acc_ref[...] = jnp.zeros(acc_ref.shape, F32)

    jd = (qi * tq) // tk
    nchunk = tk // FA_CHUNK
    s_refs = (s0_ref, s1_ref)

    def scores(j, slot, diagonal):
        off = pl.multiple_of(j * tk, tk)
        s_ref = s_refs[slot]
        for a in range(PAIR):
            st = lax.dot_general(ka_ref[a, pl.ds(off, tk), :], qa[a], NT_DIMS, preferred_element_type=F32)
            if diagonal:
                krow = lax.broadcasted_iota(jnp.int32, (tk, tq), 0)
                qcol = lax.broadcasted_iota(jnp.int32, (tk, tq), 1)
                st = jnp.where(krow - qcol <= qi * tq - j * tk, st, NEG_INF)
            s_ref[a] = st

    def absorb(j, slot):
        off = pl.multiple_of(j * tk, tk)
        s_ref = s_refs[slot]
        for a in range(PAIR):
            vt = va_ref[0, a * HEAD_DIM:(a + 1) * HEAD_DIM, pl.ds(off, tk)]
            for g in range(tq // FA_QCOLS):
                cols = slice(g * FA_QCOLS, (g + 1) * FA_QCOLS)
                cm = None
                for c in range(nchunk):
                    x = s_ref[a, c * FA_CHUNK:(c + 1) * FA_CHUNK, cols].reshape(FA_CHUNK // 8, 8, FA_QCOLS)
                    part = jnp.max(x, axis=0)
                    cm = part if cm is None else jnp.maximum(cm, part)
                m_old = m_ref[a, :, cols]
                m_new = jnp.maximum(m_old, jnp.max(cm, axis=0, keepdims=True))
                alpha = jnp.exp2(m_old - m_new)
                cs = None
                for c in range(nchunk):
                    rows = slice(c * FA_CHUNK, (c + 1) * FA_CHUNK)
                    pc = jnp.exp2(s_ref[a, rows, cols] - m_new)
                    p_ref[a, rows, cols] = pc.astype(BF16)
                    part = jnp.sum(pc.reshape(FA_CHUNK // 8, 8, FA_QCOLS), axis=0)
                    cs = part if cs is None else cs + part
                pv = jnp.dot(vt, p_ref[a, :, cols], preferred_element_type=F32)
                acc_ref[a, :, cols] = alpha * acc_ref[a, :, cols] + pv
                l_ref[a, :, cols] = alpha * l_ref[a, :, cols] + jnp.sum(cs, axis=0, keepdims=True)
                m_ref[a, :, cols] = m_new

    scores(jd, 0, True)

    def body(u, carry):
        scores(2 * u, 1, False)
        absorb(jnp.where(u == 0, jd, 2 * u - 1), 0)
        scores(2 * u + 1, 0, False)
        absorb(2 * u, 1)
        return carry

    npairs = jd // 2
    lax.fori_loop(0, npairs, body, 0)
    last0 = jnp.where(npairs == 0, jd, 2 * npairs - 1)

    @pl.when(jd % 2 == 1)
    def _():
        scores(jd - 1, 1, False)
        absorb(last0, 0)
        absorb(jd - 1, 1)

    @pl.when(jd % 2 == 0)
    def _():
        absorb(last0, 0)

    out = jnp.concatenate([acc_ref[a] / l_ref[a] for a in range(PAIR)], axis=0)
    o_ref[...] = out.T.astype(o_ref.dtype)


def forgetting_attention_prompt(qkv16, ka, va, dcum, *, seq, tq, tk, name):
    npair = C_WIDTH // LANES
    return pl.pallas_call(
        functools.partial(_fa_prompt_kernel, tq=tq, tk=tk),
        grid=(npair, seq // tq),
        in_specs=[pl.BlockSpec((tq, LANES), lambda p, i: (i, p)),
                  pl.BlockSpec((PAIR, seq, LANES), lambda p, i: (p, 0, 0)),
                  pl.BlockSpec((1, LANES, seq), lambda p, i: (p, 0, 0)),
                  pl.BlockSpec((8, C_HEADS), lambda p, i: (i * (tq // 8), 0))],
        out_specs=pl.BlockSpec((tq, LANES), lambda p, i: (i, p)),
        out_shape=jax.ShapeDtypeStruct((seq, C_WIDTH), BF16),
        scratch_shapes=[pltpu.VMEM((PAIR, 1, tq), F32),
                        pltpu.VMEM((PAIR, 1, tq), F32),
                        pltpu.VMEM((PAIR, HEAD_DIM, tq), F32),
                        pltpu.VMEM((PAIR, tk, tq), F32),
                        pltpu.VMEM((PAIR, tk, tq), F32),
                        pltpu.VMEM((PAIR, tk, tq), BF16)],
        compiler_params=_cparams("arbitrary", "arbitrary"),
        name=name,
    )(qkv16, ka, va, dcum)


def _fa_sample_kernel(q_ref, kn_ref, vn_ref, kc_ref, vc_ref, dc_ref, dp_ref, dn_ref, o_ref,
                      qr_ref, m_ref, l_ref, acc_ref, *, t, nkc):
    kc_i = pl.program_id(1)
    scale = HEAD_DIM ** -0.5
    rows = C_HEADS * t

    @pl.when(kc_i == 0)
    def _():
        qr_ref[...] = _head_rows(q_ref[...], C_HEADS)
        m_ref[...] = jnp.full(m_ref.shape, NEG_INF, F32)
        l_ref[...] = jnp.zeros(l_ref.shape, F32)
        acc_ref[...] = jnp.zeros(acc_ref.shape, F32)

    qr = qr_ref[...]
    d_tot = dp_ref[0][:, :, LANES - 1:LANES]

    def update(s, v):
        m = m_ref[...]
        m_new = jnp.maximum(m, jnp.max(s, axis=-1, keepdims=True))
        alpha = jnp.exp(m - m_new)
        p = jnp.exp(s - m_new)
        l_ref[...] = alpha * l_ref[...] + jnp.sum(p, axis=-1, keepdims=True)
        acc_ref[...] = alpha * acc_ref[...] + jnp.dot(p.astype(BF16), v, preferred_element_type=F32)
        m_ref[...] = m_new

    tk = kc_ref.shape[1]
    bias_c = jnp.broadcast_to(d_tot - dc_ref[0], (C_HEADS, t, tk)).reshape(rows, tk)
    s_c = lax.dot_general(qr, kc_ref[0].astype(BF16), NT_DIMS, preferred_element_type=F32) * scale + bias_c
    update(s_c, vc_ref[0].astype(BF16))

    @pl.when(kc_i == nkc - 1)
    def _():
        bias_n = jnp.broadcast_to(d_tot - dn_ref[0][:, :, 0:t], (C_HEADS, t, t)).reshape(rows, t)
        s_n = lax.dot_general(qr, kn_ref[...], NT_DIMS, preferred_element_type=F32) * scale + bias_n
        qpos = lax.broadcasted_iota(jnp.int32, (C_HEADS, t, t), 1).reshape(rows, t)
        kpos = lax.broadcasted_iota(jnp.int32, (C_HEADS, t, t), 2).reshape(rows, t)
        s_n = jnp.where(kpos <= qpos, s_n, NEG_INF)
        update(s_n, vn_ref[...])
        r = acc_ref[...] / l_ref[...]
        o_ref[...] = _head_diag(r, C_HEADS, t).astype(o_ref.dtype)


def forgetting_attention_sample(qkv, k_cache, v_cache, dall, *, row0, nb, t, tk, name):
    base = row0 // t
    past = k_cache.shape[1]
    nkc = past // tk
    rows = C_HEADS * t
    return pl.pallas_call(
        functools.partial(_fa_sample_kernel, t=t, nkc=nkc),
        grid=(nb, nkc),
        in_specs=[pl.BlockSpec((t, C_WIDTH), lambda b, c: (base + b, 0)),
                  pl.BlockSpec((t, C_WIDTH), lambda b, c: (base + b, 1)),
                  pl.BlockSpec((t, C_WIDTH), lambda b, c: (base + b, 2)),
                  pl.BlockSpec((1, tk, C_WIDTH), lambda b, c: (b, c, 0)),
                  pl.BlockSpec((1, tk, C_WIDTH), lambda b, c: (b, c, 0)),
                  pl.BlockSpec((1, C_HEADS, 1, tk), lambda b, c: (b, 0, 0, c)),
                  pl.BlockSpec((1, C_HEADS, 1, LANES), lambda b, c: (b, 0, 0, past // LANES - 1)),
                  pl.BlockSpec((1, C_HEADS, 1, LANES), lambda b, c: (b, 0, 0, past // LANES))],
        out_specs=pl.BlockSpec((t, C_WIDTH), lambda b, c: (b, 0)),
        out_shape=jax.ShapeDtypeStruct((nb * t, C_WIDTH), BF16),
        scratch_shapes=[pltpu.VMEM((rows, C_WIDTH), BF16),
                        pltpu.VMEM((rows, 1), F32),
                        pltpu.VMEM((rows, 1), F32),
                        pltpu.VMEM((rows, C_WIDTH), F32)],
        compiler_params=_cparams("arbitrary", "arbitrary"),
        name=name,
    )(qkv, qkv, qkv, k_cache, v_cache, dall, dall, dall)


def _rel_bias_matrix(rel_table, nq, nk):
    span = nq + nk - 1
    diag = jnp.arange(span) - (nq - 1)
    vals = rel_table[:, jnp.clip(A_WINDOW - diag, -REL_CLIP, REL_CLIP) + REL_CLIP].astype(F32)
    nh = vals.shape[0]
    skew = jnp.tile(vals, (1, nq + 1))[:, :nq * (span + 1)].reshape(nh, nq, span + 1)
    return skew[:, ::-1, :nk]


def _band_block_bias(bias_chunk):
    nchunk = A_BLOCK // CHUNK
    parts = [jnp.pad(bias_chunk * LOG2E, ((0, 0), (0, 0), (CHUNK * c, A_KEYS - A_BAND - CHUNK * c)),
                     constant_values=NEG_INF) for c in range(nchunk)]
    later = jnp.stack(parts, axis=1).reshape(A_HEADS, A_BLOCK, A_KEYS)
    col = jnp.arange(A_KEYS)
    first = jnp.where(col >= A_WINDOW, later, NEG_INF)
    return jnp.stack([first, later])


def kernel(x_prompt, x_sample, cache_a_k, cache_a_v, state_pool, cache_c_k, cache_c_v, cache_c_logf,
           cache_mem_k, cache_mem_v, mem_prompt, w_in_ab, rel_bias_a, pool_w, pool_scale, w_out_ab,
           w_in_c, b_f, w_out_c, w_xq, w_xk, w_xv, w_xo, ln_g, ln_b, ffn_w1, ffn_w3, ffn_w2,
           w_router, b_router, moe_w1, moe_w3, moe_w2):
    bp, seq, d = x_prompt.shape
    nb, t, _ = x_sample.shape
    assert bp == 1 and d == D_MODEL
    ns = nb * t
    tot = seq + ns
    past = cache_c_k.shape[2]
    tm = 512
    assert tot % tm == 0 and seq % tm == 0

    x = jnp.concatenate([x_prompt.reshape(seq, d), x_sample.reshape(ns, d)], axis=0)

    mem = mem_prompt.reshape(N_MEM, d)
    p_mem_k, p_mem_v = [], []
    for layer in range(DEPTH):
        (mk,) = matmul(mem, w_xk[layer].astype(BF16), [F32], tm=N_MEM, tn=d, name=f"mem_k{layer}")
        (mv,) = matmul(mem, w_xv[layer].astype(BF16), [F32], tm=N_MEM, tn=d, name=f"mem_v{layer}")
        p_mem_k.append(mk)
        p_mem_v.append(mv)

    def cross_block(x, layer):
        (q,) = matmul(x, w_xq[layer].astype(BF16), [BF16], tm=tm, tn=d, name=f"xq{layer}")
        o_p = cross_attention(q, p_mem_k[layer].reshape(1, N_MEM, d), p_mem_v[layer].reshape(1, N_MEM, d),
                              row0=0, rows=seq, tq=tm, name=f"xattn_p{layer}")
        o_s = cross_attention(q, cache_mem_k[layer].reshape(nb, N_MEM, d).astype(BF16),
                              cache_mem_v[layer].reshape(nb, N_MEM, d).astype(BF16),
                              row0=seq, rows=ns, tq=t, name=f"xattn_s{layer}")
        return matmul_ln([(o_p, o_s)], w_xo[layer].astype(BF16), x, ln_g[layer, 1], ln_b[layer, 1], tm=tm,
                         name=f"xo{layer}")

    pr = 0
    (h,) = matmul(x, w_in_ab[pr].astype(BF16), [F32], tm=tm, tn=w_in_ab.shape[-1], name="in_ab")
    bias = _rel_bias_matrix(rel_bias_a[pr], CHUNK, A_BAND)
    oa_p = band_attention_prompt(h, _band_block_bias(bias), seq=seq, name="band_p")
    bias_s = bias[:, :t, :A_WINDOW + t].reshape(A_HEADS * t, A_WINDOW + t)
    oa_s = band_attention_sample(h, cache_a_k[pr].reshape(nb, A_WINDOW, A_WIDTH).astype(BF16),
                                 cache_a_v[pr].reshape(nb, A_WINDOW, A_WIDTH).astype(BF16),
                                 bias_s[:, :A_WINDOW], bias_s[:, A_WINDOW:], row0=seq, nb=nb, t=t, name="band_s")
    pw = pool_w[pr].astype(BF16)
    ob_p = pool_prompt(h, pw, pool_scale[pr], seq=seq, tm=tm, name="pool_p")
    hist = jnp.concatenate([jnp.zeros((nb, 1, B_WIDTH), F32), state_pool[pr]], axis=1)
    ob_s = pool_sample(h, hist, pw, pool_scale[pr], row0=seq, nb=nb, t=t, pos0=past, name="pool_s")
    x = matmul_ln([(oa_p, oa_s), (ob_p, ob_s)], w_out_ab[pr].astype(BF16), x, ln_g[0, 0], ln_b[0, 0], tm=tm,
                  name="out_ab")
    x = cross_block(x, 0)
    ones = jnp.ones((1, tot, 1), F32)
    x = moe_ln(x, ones, ffn_w1.astype(BF16), ffn_w3.astype(BF16), ffn_w2.astype(BF16),
               ln_g[0, 2], ln_b[0, 2], tm=tm, tf=D_FF // 2, name="ffn")

    k_a = h[:, A_WIDTH:2 * A_WIDTH]
    v_a = h[:, 2 * A_WIDTH:3 * A_WIDTH]
    u_b = h[:, 3 * A_WIDTH:]
    keep = min(A_WINDOW, seq)
    p_a_k = k_a[seq - keep:seq].reshape(1, 1, keep, A_HEADS, HEAD_DIM)
    p_a_v = v_a[seq - keep:seq].reshape(1, 1, keep, A_HEADS, HEAD_DIM)
    p_pool = u_b[seq - POOL_STATE:seq].reshape(1, 1, POOL_STATE, B_WIDTH)
    s_a_k = k_a[seq:].reshape(1, nb, t, A_HEADS, HEAD_DIM)
    s_a_v = v_a[seq:].reshape(1, nb, t, A_HEADS, HEAD_DIM)
    s_pool = jnp.concatenate([state_pool[pr], u_b[seq:].reshape(nb, t, B_WIDTH)], axis=1)[:, -POOL_STATE:][None]

    w_c = w_in_c[pr]
    w_cf = jnp.pad(w_c, ((0, 0), (0, LANES - C_HEADS))).astype(BF16)
    bf_pad = jnp.pad(b_f[pr].astype(F32), (0, LANES - C_HEADS)).reshape(1, LANES)
    qkv16, logf, k_cp, v_cp, k_cs, v_cs = matmul_qkv(x, w_cf, bf_pad, seq=seq, tm=tm, name="in_c")
    logf = logf[:, :C_HEADS]
    dcum_p = cumsum_rows(logf[:seq], name="cumsum_p")
    ka, va = fa_prep(qkv16, v_cp, dcum_p, seq=seq, tr=tm, name="fa_prep")
    oc_p = forgetting_attention_prompt(qkv16, ka, va, dcum_p, seq=seq, tq=tm, tk=tm, name="fa_p")
    lf_s = logf[seq:].reshape(nb, t, C_HEADS)
    lf_all = jnp.concatenate([cache_c_logf[pr].astype(F32), lf_s,
                              jnp.zeros((nb, CUMSUM_ALIGN - t, C_HEADS), F32)], axis=1)
    dall = cumsum_lanes(jnp.swapaxes(lf_all, 1, 2), name="cumsum_s").reshape(nb, C_HEADS, 1, past + CUMSUM_ALIGN)
    oc_s = forgetting_attention_sample(qkv16, cache_c_k[pr].reshape(nb, past, C_WIDTH),
                                       cache_c_v[pr].reshape(nb, past, C_WIDTH), dall,
                                       row0=seq, nb=nb, t=t, tk=2048, name="fa_s")
    x = matmul_ln([(oc_p, oc_s)], w_out_c[pr].astype(BF16), x, ln_g[1, 0], ln_b[1, 0], tm=tm, name="out_c")
    x = cross_block(x, 1)
    w_r = jnp.pad(w_router[pr], ((0, 0), (0, LANES - N_EXPERTS)))
    b_r = jnp.concatenate([b_router[pr].astype(F32), jnp.full((LANES - N_EXPERTS,), NEG_INF, F32)]).reshape(1, LANES)
    y_p, y_s = moe_routed_ln(x, w_r, b_r, moe_w1[pr].astype(BF16), moe_w3[pr].astype(BF16),
                             moe_w2[pr].astype(BF16), ln_g[1, 2], ln_b[1, 2], tm=tm, split=seq, name="moe")

    p_c_k = k_cp.reshape(1, 1, seq, C_HEADS, HEAD_DIM)
    p_c_v = v_cp.reshape(1, 1, seq, C_HEADS, HEAD_DIM)
    p_c_logf = logf[:seq].reshape(1, 1, seq, C_HEADS)
    s_c_k = k_cs.reshape(1, nb, t, C_HEADS, HEAD_DIM)
    s_c_v = v_cs.reshape(1, nb, t, C_HEADS, HEAD_DIM)
    s_c_logf = lf_s[None]

    y_prompt = y_p.reshape(1, seq, d)
    y_sample = y_s.reshape(nb, t, d)
    pmk = jnp.stack(p_mem_k).reshape(DEPTH, 1, N_MEM, X_HEADS, X_HEAD_DIM)
    pmv = jnp.stack(p_mem_v).reshape(DEPTH, 1, N_MEM, X_HEADS, X_HEAD_DIM)
    return (y_prompt, y_sample, p_a_k, p_a_v, p_pool, p_c_k, p_c_v, p_c_logf, pmk, pmv,
            s_a_k, s_a_v, s_pool, s_c_k, s_c_v, s_c_logf)
```

```python
import functools

import jax
import jax.numpy as jnp
from jax import lax
from jax.experimental import pallas as pl
from jax.experimental.pallas import tpu as pltpu

F32 = jnp.float32
BF16 = jnp.bfloat16

D_MODEL = 1024
DEPTH = 2
CHUNK = 64
N_MEM = 256
HEAD_DIM = 64
A_HEADS = 8
A_WIDTH = A_HEADS * HEAD_DIM
A_PREV_CHUNKS = 8
A_WINDOW = A_PREV_CHUNKS * CHUNK
A_BAND = (A_PREV_CHUNKS + 1) * CHUNK
REL_CLIP = 128
POOL_WINDOWS = (2, 4, 8, 16)
B_WIDTH = D_MODEL - A_WIDTH
B_GROUP = B_WIDTH // len(POOL_WINDOWS)
POOL_STATE = max(POOL_WINDOWS) - 1
C_HEADS = D_MODEL // HEAD_DIM
C_WIDTH = C_HEADS * HEAD_DIM
X_HEADS = 4
X_HEAD_DIM = D_MODEL // X_HEADS
D_FF = 2816
N_EXPERTS = 8
ALPHA = (2.0 * DEPTH) ** 0.25
LN_EPS = 1e-5
NEG_INF = -1e30
LOG2E = 1.4426950408889634

LANES = 128
PAIR = LANES // HEAD_DIM
VMEM_LIMIT = 56 * 1024 * 1024

NT_DIMS = (((1,), (1,)), ((), ()))


def _cparams(*sem):
    return pltpu.CompilerParams(dimension_semantics=sem, vmem_limit_bytes=VMEM_LIMIT)


def _deepnorm_ln(res, sub, g, b):
    z = ALPHA * res + sub
    mu = jnp.mean(z, axis=-1, keepdims=True)
    zc = z - mu
    var = jnp.mean(zc * zc, axis=-1, keepdims=True)
    return zc * lax.rsqrt(var + LN_EPS) * g + b


def _mm_kernel(x_ref, w_ref, *o_refs):
    y = jnp.dot(x_ref[...].astype(BF16), w_ref[...], preferred_element_type=F32)
    for o_ref in o_refs:
        o_ref[...] = y.astype(o_ref.dtype)


def matmul(x, w, out_dtypes, *, tm, tn, name):
    m, k = x.shape
    n = w.shape[1]
    outs = pl.pallas_call(
        _mm_kernel,
        grid=(n // tn, m // tm),
        in_specs=[pl.BlockSpec((tm, k), lambda j, i: (i, 0)),
                  pl.BlockSpec((k, tn), lambda j, i: (0, j))],
        out_specs=[pl.BlockSpec((tm, tn), lambda j, i: (i, j)) for _ in out_dtypes],
        out_shape=[jax.ShapeDtypeStruct((m, n), dt) for dt in out_dtypes],
        compiler_params=_cparams("arbitrary", "arbitrary"),
        name=name,
    )(x, w)
    return outs


def _mm_qkv_kernel(x_ref, w_ref, bf_ref, qkv_ref, lf_ref, kp_ref, vp_ref, ks_ref, vs_ref, *, n_prompt):
    i = pl.program_id(0)
    y = jnp.dot(x_ref[...].astype(BF16), w_ref[...], preferred_element_type=F32)
    qkv_ref[...] = y[:, :3 * C_WIDTH].astype(BF16)
    lf_ref[...] = jax.nn.log_sigmoid(y[:, 3 * C_WIDTH:] + bf_ref[...])
    k = y[:, C_WIDTH:2 * C_WIDTH]
    v = y[:, 2 * C_WIDTH:3 * C_WIDTH]

    @pl.when(i < n_prompt)
    def _():
        kp_ref[...] = k
        vp_ref[...] = v

    @pl.when(i >= n_prompt)
    def _():
        ks_ref[...] = k
        vs_ref[...] = v


def matmul_qkv(x, w, b_f, *, seq, tm, name):
    m, d = x.shape
    n = 3 * C_WIDTH
    n_prompt = seq // tm
    n_sample = (m - seq) // tm
    prompt_map = lambda i: (jnp.minimum(i, n_prompt - 1), 0)
    sample_map = lambda i: (jnp.maximum(i - n_prompt, 0), 0)
    return pl.pallas_call(
        functools.partial(_mm_qkv_kernel, n_prompt=n_prompt),
        grid=(n_prompt + n_sample,),
        in_specs=[pl.BlockSpec((tm, d), lambda i: (i, 0)),
                  pl.BlockSpec((d, n + LANES), lambda i: (0, 0)),
                  pl.BlockSpec((1, LANES), lambda i: (0, 0))],
        out_specs=[pl.BlockSpec((tm, n), lambda i: (i, 0)),
                   pl.BlockSpec((tm, LANES), lambda i: (i, 0)),
                   pl.BlockSpec((tm, C_WIDTH), prompt_map), pl.BlockSpec((tm, C_WIDTH), prompt_map),
                   pl.BlockSpec((tm, C_WIDTH), sample_map), pl.BlockSpec((tm, C_WIDTH), sample_map)],
        out_shape=[jax.ShapeDtypeStruct((m, n), BF16),
                   jax.ShapeDtypeStruct((m, LANES), F32),
                   jax.ShapeDtypeStruct((seq, C_WIDTH), F32), jax.ShapeDtypeStruct((seq, C_WIDTH), F32),
                   jax.ShapeDtypeStruct((m - seq, C_WIDTH), F32), jax.ShapeDtypeStruct((m - seq, C_WIDTH), F32)],
        compiler_params=_cparams("arbitrary"),
        name=name,
    )(x, w, b_f)


def _mm_ln_kernel(*refs, n_groups, n_first):
    x_refs = refs[:2 * n_groups]
    w_ref, res_ref, g_ref, b_ref, o_ref = refs[2 * n_groups:]
    i = pl.program_id(0)

    def run(xs):
        y, k0 = None, 0
        for x_ref in xs:
            kg = x_ref.shape[1]
            part = jnp.dot(x_ref[...].astype(BF16), w_ref[k0:k0 + kg, :], preferred_element_type=F32)
            y = part if y is None else y + part
            k0 += kg
        o_ref[...] = _deepnorm_ln(res_ref[...], y, g_ref[...], b_ref[...])

    @pl.when(i < n_first)
    def _():
        run(x_refs[0::2])

    @pl.when(i >= n_first)
    def _():
        run(x_refs[1::2])


def matmul_ln(groups, w, res, g, b, *, tm, name):
    m, n = res.shape
    split = groups[0][0].shape[0]
    n_first = split // tm
    first_map = lambda i: (jnp.minimum(i, n_first - 1), 0)
    rest_map = lambda i: (jnp.maximum(i - n_first, 0), 0)
    x_specs, x_args = [], []
    for xf, xr in groups:
        x_specs += [pl.BlockSpec((tm, xf.shape[1]), first_map), pl.BlockSpec((tm, xr.shape[1]), rest_map)]
        x_args += [xf, xr]
    return pl.pallas_call(
        functools.partial(_mm_ln_kernel, n_groups=len(groups), n_first=n_first),
        grid=(m // tm,),
        in_specs=x_specs + [pl.BlockSpec(w.shape, lambda i: (0, 0)),
                            pl.BlockSpec((tm, n), lambda i: (i, 0)),
                            pl.BlockSpec((1, n), lambda i: (0, 0)),
                            pl.BlockSpec((1, n), lambda i: (0, 0))],
        out_specs=pl.BlockSpec((tm, n), lambda i: (i, 0)),
        out_shape=jax.ShapeDtypeStruct((m, n), F32),
        compiler_params=_cparams("arbitrary"),
        name=name,
    )(*x_args, w, res, g.reshape(1, n), b.reshape(1, n))


def _moe_kernel(x_ref, comb_ref, w1_ref, w3_ref, w2_ref, g_ref, b_ref, o_ref, acc_ref, *, ne, nf):
    e = pl.program_id(1)
    f = pl.program_id(2)

    @pl.when((e == 0) & (f == 0))
    def _():
        acc_ref[...] = jnp.zeros_like(acc_ref)

    x = x_ref[...].astype(BF16)
    h1 = jnp.dot(x, w1_ref[0], preferred_element_type=F32)
    h3 = jnp.dot(x, w3_ref[0], preferred_element_type=F32)
    a = (jax.nn.silu(h1) * h3).astype(BF16)
    y = jnp.dot(a, w2_ref[0], preferred_element_type=F32)
    acc_ref[...] += comb_ref[0] * y

    @pl.when((e == ne - 1) & (f == nf - 1))
    def _():
        o_ref[...] = _deepnorm_ln(x_ref[...], acc_ref[...], g_ref[...], b_ref[...])


def moe_ln(x, comb, w1, w3, w2, g, b, *, tm, tf, name):
    m, d = x.shape
    ne, _, dff = w1.shape
    nf = dff // tf
    return pl.pallas_call(
        functools.partial(_moe_kernel, ne=ne, nf=nf),
        grid=(m // tm, ne, nf),
        in_specs=[pl.BlockSpec((tm, d), lambda i, e, f: (i, 0)),
                  pl.BlockSpec((1, tm, 1), lambda i, e, f: (e, i, 0)),
                  pl.BlockSpec((1, d, tf), lambda i, e, f: (e, 0, f)),
                  pl.BlockSpec((1, d, tf), lambda i, e, f: (e, 0, f)),
                  pl.BlockSpec((1, tf, d), lambda i, e, f: (e, f, 0)),
                  pl.BlockSpec((1, d), lambda i, e, f: (0, 0)),
                  pl.BlockSpec((1, d), lambda i, e, f: (0, 0))],
        out_specs=pl.BlockSpec((tm, d), lambda i, e, f: (i, 0)),
        out_shape=jax.ShapeDtypeStruct((m, d), F32),
        scratch_shapes=[pltpu.VMEM((tm, d), F32)],
        compiler_params=_cparams("arbitrary", "arbitrary", "arbitrary"),
        name=name,
    )(x, comb, w1, w3, w2, g.reshape(1, d), b.reshape(1, d))


ROUTE_I1, ROUTE_I2, ROUTE_G1, ROUTE_G2, ROUTE_R1, ROUTE_R2 = range(6)
EXPERT_TILE = 256


def _router_kernel(x_ref, w_ref, b_ref, o_ref, cnt_ref, carry_ref):
    i = pl.program_id(0)

    @pl.when(i == 0)
    def _():
        carry_ref[...] = jnp.zeros_like(carry_ref)

    tm = x_ref.shape[0]
    logits = jnp.dot(x_ref[...], w_ref[...], preferred_element_type=F32,
                     precision=lax.Precision.HIGHEST) + b_ref[...]
    lane = lax.broadcasted_iota(jnp.int32, logits.shape, 1)
    big = jnp.int32(LANES)
    v1 = jnp.max(logits, axis=-1, keepdims=True)
    i1 = jnp.min(jnp.where(logits == v1, lane, big), axis=-1, keepdims=True)
    rest = jnp.where(lane == i1, NEG_INF, logits)
    v2 = jnp.max(rest, axis=-1, keepdims=True)
    i2 = jnp.min(jnp.where(rest == v2, lane, big), axis=-1, keepdims=True)
    e2 = jnp.exp(v2 - v1)
    den = 1.0 + e2
    oh1 = lane == i1
    oh2 = lane == i2
    cnt = jnp.where(oh1 | oh2, 1.0, 0.0)
    r = lax.broadcasted_iota(jnp.int32, (tm, tm), 0)
    c = lax.broadcasted_iota(jnp.int32, (tm, tm), 1)
    earlier = jnp.where(c < r, 1.0, 0.0).astype(BF16)
    before = jnp.dot(earlier, cnt.astype(BF16), preferred_element_type=F32) + carry_ref[...]
    r1 = jnp.sum(jnp.where(oh1, before, 0.0), axis=-1, keepdims=True)
    r2 = jnp.sum(jnp.where(oh2, before, 0.0), axis=-1, keepdims=True)
    carry_ref[...] += jnp.sum(cnt, axis=0, keepdims=True)
    cnt_ref[...] = carry_ref[...]
    out = jnp.zeros(logits.shape, F32)
    for ln, val in ((ROUTE_I1, i1.astype(F32)), (ROUTE_I2, i2.astype(F32)), (ROUTE_G1, 1.0 / den),
                    (ROUTE_G2, e2 / den), (ROUTE_R1, r1), (ROUTE_R2, r2)):
        out = jnp.where(lane == ln, val, out)
    o_ref[...] = out


def router(x, w_pad, b_pad, *, tm, name):
    m, d = x.shape
    return pl.pallas_call(
        _router_kernel,
        grid=(m // tm,),
        in_specs=[pl.BlockSpec((tm, d), lambda i: (i, 0)),
                  pl.BlockSpec((d, LANES), lambda i: (0, 0)),
                  pl.BlockSpec((1, LANES), lambda i: (0, 0))],
        out_specs=[pl.BlockSpec((tm, LANES), lambda i: (i, 0)),
                   pl.BlockSpec((1, LANES), lambda i: (0, 0))],
        out_shape=[jax.ShapeDtypeStruct((m, LANES), F32), jax.ShapeDtypeStruct((1, LANES), F32)],
        scratch_shapes=[pltpu.VMEM((1, LANES), F32)],
        compiler_params=_cparams("arbitrary"),
        name=name,
    )(x, w_pad, b_pad)


def _gather_rows_start(src_hbm, idx_ref, n, dst, sem):
    def issue(r, carry):
        pltpu.make_async_copy(src_hbm.at[pl.ds(idx_ref[0, 0, r], 1)], dst.at[pl.ds(r, 1)], sem).start()
        return carry

    lax.fori_loop(0, n, issue, 0, unroll=8)


def _gather_rows_wait(src_hbm, n, dst, sem):
    pltpu.make_async_copy(src_hbm.at[pl.ds(0, n)], dst, sem).wait()


def _expert_kernel(te_ref, nt_ref, idx0_ref, idxn_ref, x_hbm, w1_ref, w3_ref, w2_ref, o_ref, xbuf, sem, *, tm, nf):
    m = pl.program_id(0)
    nt = nt_ref[0]
    slot = m % 2

    @pl.when(m == 0)
    def _():
        _gather_rows_start(x_hbm, idx0_ref, tm, xbuf.at[0], sem.at[0])

    @pl.when(m + 1 < nt)
    def _():
        _gather_rows_start(x_hbm, idxn_ref, tm, xbuf.at[1 - slot], sem.at[1 - slot])

    @pl.when(m < nt)
    def _():
        _gather_rows_wait(x_hbm, tm, xbuf.at[slot], sem.at[slot])
        x = xbuf[slot].astype(BF16)
        tf = w1_ref.shape[2] // nf
        y = None
        for f in range(nf):
            sl = slice(f * tf, (f + 1) * tf)
            h1 = jnp.dot(x, w1_ref[0, :, sl], preferred_element_type=F32)
            h3 = jnp.dot(x, w3_ref[0, :, sl], preferred_element_type=F32)
            a = (jax.nn.silu(h1) * h3).astype(BF16)
            part = jnp.dot(a, w2_ref[0, sl, :], preferred_element_type=F32)
            y = part if y is None else y + part
        o_ref[...] = y

    @pl.when(m >= nt)
    def _():
        o_ref[...] = jnp.zeros_like(o_ref)


def expert_tiles(x, w1, w3, w2, tile_expert, n_tiles, src_rows, *, tm, nf, name):
    m, d = x.shape
    ne, _, dff = w1.shape
    nt_max = src_rows.shape[0]

    def wmap(i, te, nt):
        return (te[jnp.minimum(i, nt[0] - 1)], 0, 0)

    grid_spec = pltpu.PrefetchScalarGridSpec(
        num_scalar_prefetch=2,
        grid=(nt_max,),
        in_specs=[pl.BlockSpec((1, 1, tm), lambda i, te, nt: (0, 0, 0), memory_space=pltpu.SMEM),
                  pl.BlockSpec((1, 1, tm), lambda i, te, nt: (jnp.minimum(i + 1, nt_max - 1), 0, 0),
                               memory_space=pltpu.SMEM),
                  pl.BlockSpec(memory_space=pl.ANY),
                  pl.BlockSpec((1, d, dff), wmap),
                  pl.BlockSpec((1, d, dff), wmap),
                  pl.BlockSpec((1, dff, d), wmap)],
        out_specs=pl.BlockSpec((tm, d), lambda i, te, nt: (i, 0)),
        scratch_shapes=[pltpu.VMEM((2, tm, d), F32), pltpu.SemaphoreType.DMA((2,))],
    )
    return pl.pallas_call(
        functools.partial(_expert_kernel, tm=tm, nf=nf),
        grid_spec=grid_spec,
        out_shape=jax.ShapeDtypeStruct((nt_max * tm, d), F32),
        compiler_params=_cparams("arbitrary"),
        name=name,
    )(tile_expert, n_tiles, src_rows, src_rows, x, w1, w3, w2)


def _combine_ln_kernel(idx0_ref, idxn_ref, y_hbm, x_ref, route_ref, g_ref, b_ref, op_ref, os_ref, ybuf, sem,
                       *, tm, nsteps, n_first):
    i = pl.program_id(0)
    slot = i % 2

    @pl.when(i == 0)
    def _():
        _gather_rows_start(y_hbm, idx0_ref, 2 * tm, ybuf.at[0], sem.at[0])

    @pl.when(i + 1 < nsteps)
    def _():
        _gather_rows_start(y_hbm, idxn_ref, 2 * tm, ybuf.at[1 - slot], sem.at[1 - slot])

    _gather_rows_wait(y_hbm, 2 * tm, ybuf.at[slot], sem.at[slot])
    route = route_ref[...]
    g1 = route[:, ROUTE_G1:ROUTE_G1 + 1]
    g2 = route[:, ROUTE_G2:ROUTE_G2 + 1]
    y = g1 * ybuf[slot, 0:tm, :] + g2 * ybuf[slot, tm:2 * tm, :]
    out = _deepnorm_ln(x_ref[...], y, g_ref[...], b_ref[...])

    @pl.when(i < n_first)
    def _():
        op_ref[...] = out

    @pl.when(i >= n_first)
    def _():
        os_ref[...] = out


def combine_ln(x, route, y_sorted, pos_rows, g, b, *, tm, split, name):
    m, d = x.shape
    nsteps = m // tm
    n_first = split // tm
    return pl.pallas_call(
        functools.partial(_combine_ln_kernel, tm=tm, nsteps=nsteps, n_first=n_first),
        grid=(nsteps,),
        in_specs=[pl.BlockSpec((1, 1, 2 * tm), lambda i: (0, 0, 0), memory_space=pltpu.SMEM),
                  pl.BlockSpec((1, 1, 2 * tm), lambda i: (jnp.minimum(i + 1, nsteps - 1), 0, 0),
                               memory_space=pltpu.SMEM),
                  pl.BlockSpec(memory_space=pl.ANY),
                  pl.BlockSpec((tm, d), lambda i: (i, 0)),
                  pl.BlockSpec((tm, LANES), lambda i: (i, 0)),
                  pl.BlockSpec((1, d), lambda i: (0, 0)),
                  pl.BlockSpec((1, d), lambda i: (0, 0))],
        out_specs=[pl.BlockSpec((tm, d), lambda i: (jnp.minimum(i, n_first - 1), 0)),
                   pl.BlockSpec((tm, d), lambda i: (jnp.maximum(i - n_first, 0), 0))],
        out_shape=[jax.ShapeDtypeStruct((split, d), F32), jax.ShapeDtypeStruct((m - split, d), F32)],
        scratch_shapes=[pltpu.VMEM((2, 2 * tm, d), F32), pltpu.SemaphoreType.DMA((2,))],
        compiler_params=_cparams("arbitrary"),
        name=name,
    )(pos_rows, pos_rows, y_sorted, x, route, g.reshape(1, d), b.reshape(1, d))


def moe_routed_ln(x, w_router_pad, b_router_pad, w1, w3, w2, g, b, *, tm, split, name):
    m, d = x.shape
    ne = w1.shape[0]
    te_rows = EXPERT_TILE
    route, counts = router(x, w_router_pad, b_router_pad, tm=tm, name=name + "_router")
    counts = counts[0, :ne].astype(jnp.int32)
    padded = (counts + te_rows - 1) // te_rows * te_rows
    upto = jnp.arange(ne)[None, :] <= jnp.arange(ne)[:, None]
    ends = jnp.sum(jnp.where(upto, padded[None, :], 0), axis=1)
    base = ends - padded
    i1 = route[:, ROUTE_I1].astype(jnp.int32)
    i2 = route[:, ROUTE_I2].astype(jnp.int32)
    pos1 = base[i1] + route[:, ROUTE_R1].astype(jnp.int32)
    pos2 = base[i2] + route[:, ROUTE_R2].astype(jnp.int32)
    nt_max = (2 * m + ne * (te_rows - 1)) // te_rows + 1
    tok = jnp.arange(m, dtype=jnp.int32)
    src = jnp.zeros((nt_max * te_rows,), jnp.int32).at[pos1].set(tok).at[pos2].set(tok)
    n_tiles = (ends[-1] // te_rows).astype(jnp.int32).reshape(1)
    tile_start = jnp.arange(nt_max, dtype=jnp.int32) * te_rows
    tile_expert = jnp.minimum(jnp.sum(tile_start[:, None] >= ends[None, :], axis=1), ne - 1).astype(jnp.int32)
    y_sorted = expert_tiles(x, w1, w3, w2, tile_expert, n_tiles, src.reshape(nt_max, 1, te_rows),
                            tm=te_rows, nf=2, name=name + "_experts")
    pos_rows = jnp.concatenate([pos1.reshape(m // tm, 1, tm), pos2.reshape(m // tm, 1, tm)], axis=-1)
    return combine_ln(x, route, y_sorted, pos_rows, g, b, tm=tm, split=split, name=name + "_combine")


def _xattn_kernel(q_ref, mk_ref, mv_ref, o_ref):
    scale = X_HEAD_DIM ** -0.5
    for h in range(X_HEADS):
        sl = slice(h * X_HEAD_DIM, (h + 1) * X_HEAD_DIM)
        q = q_ref[:, sl]
        k = mk_ref[0, :, sl].astype(BF16)
        v = mv_ref[0, :, sl].astype(BF16)
        s = lax.dot_general(q, k, NT_DIMS, preferred_element_type=F32) * scale
        m = jnp.max(s, axis=-1, keepdims=True)
        p = jnp.exp(s - m)
        l = jnp.sum(p, axis=-1, keepdims=True)
        o = jnp.dot(p.astype(BF16), v, preferred_element_type=F32) / l
        o_ref[:, sl] = o.astype(o_ref.dtype)


def cross_attention(q, mem_k, mem_v, *, row0, rows, tq, name):
    d = q.shape[1]
    nb = mem_k.shape[0]
    per_b = rows // nb
    nq = per_b // tq
    base = row0 // tq
    return pl.pallas_call(
        _xattn_kernel,
        grid=(nb, nq),
        in_specs=[pl.BlockSpec((tq, d), lambda b, i: (base + b * nq + i, 0)),
                  pl.BlockSpec((1, N_MEM, d), lambda b, i: (b, 0, 0)),
                  pl.BlockSpec((1, N_MEM, d), lambda b, i: (b, 0, 0))],
        out_specs=pl.BlockSpec((tq, d), lambda b, i: (b * nq + i, 0)),
        out_shape=jax.ShapeDtypeStruct((rows, d), BF16),
        compiler_params=_cparams("arbitrary", "arbitrary"),
        name=name,
    )(q, mem_k, mem_v)


A_BLOCK = A_WINDOW
A_KEYS = A_WINDOW + A_BLOCK


def _head_lane_masks():
    lane = lax.broadcasted_iota(jnp.int32, (1, LANES), 1)
    return [(lane >= a * HEAD_DIM) & (lane < (a + 1) * HEAD_DIM) for a in range(PAIR)]


def _band_prompt_kernel(q_ref, kp_ref, kc_ref, vp_ref, vc_ref, bias_ref, o_ref):
    kk = jnp.concatenate([kp_ref[...].astype(BF16), kc_ref[...].astype(BF16)], axis=0)
    vv = jnp.concatenate([vp_ref[...].astype(BF16), vc_ref[...].astype(BF16)], axis=0)
    masks = _head_lane_masks()
    q = q_ref[...] * (HEAD_DIM ** -0.5 * LOG2E)
    out = None
    for a in range(PAIR):
        qm = jnp.where(masks[a], q, 0.0).astype(BF16)
        s = lax.dot_general(qm, kk, NT_DIMS, preferred_element_type=F32) + bias_ref[0, a]
        m = jnp.max(s, axis=-1, keepdims=True)
        p = jnp.exp2(s - m)
        l = jnp.sum(p, axis=-1, keepdims=True)
        o = jnp.dot(p.astype(BF16), vv, preferred_element_type=F32) / l
        out = o if out is None else jnp.where(masks[a], o, out)
    o_ref[...] = out.astype(o_ref.dtype)


def band_attention_prompt(h, bias, *, seq, name):
    nblk = seq // A_BLOCK
    npair = A_WIDTH // LANES
    qcol, kcol, vcol = 0, npair, 2 * npair
    return pl.pallas_call(
        _band_prompt_kernel,
        grid=(npair, nblk),
        in_specs=[pl.BlockSpec((A_BLOCK, LANES), lambda p, i: (i, qcol + p)),
                  pl.BlockSpec((A_BLOCK, LANES), lambda p, i: (jnp.maximum(i - 1, 0), kcol + p)),
                  pl.BlockSpec((A_BLOCK, LANES), lambda p, i: (i, kcol + p)),
                  pl.BlockSpec((A_BLOCK, LANES), lambda p, i: (jnp.maximum(i - 1, 0), vcol + p)),
                  pl.BlockSpec((A_BLOCK, LANES), lambda p, i: (i, vcol + p)),
                  pl.BlockSpec((1, PAIR, A_BLOCK, A_KEYS), lambda p, i: (jnp.minimum(i, 1), p, 0, 0))],
        out_specs=pl.BlockSpec((A_BLOCK, LANES), lambda p, i: (i, p)),
        out_shape=jax.ShapeDtypeStruct((seq, A_WIDTH), BF16),
        compiler_params=_cparams("arbitrary", "arbitrary"),
        name=name,
    )(h, h, h, h, h, bias)


def _head_rows(x, nheads):
    t, w = x.shape
    x3 = jnp.broadcast_to(x[None], (nheads, t, w))
    hh = lax.broadcasted_iota(jnp.int32, (nheads, 1, w), 0)
    cc = lax.broadcasted_iota(jnp.int32, (nheads, 1, w), 2) // HEAD_DIM
    return jnp.where(hh == cc, x3, jnp.zeros_like(x3)).reshape(nheads * t, w)


def _head_diag(r, nheads, t):
    w = r.shape[1]
    r3 = r.reshape(nheads, t, w)
    hh = lax.broadcasted_iota(jnp.int32, (nheads, 1, w), 0)
    cc = lax.broadcasted_iota(jnp.int32, (nheads, 1, w), 2) // HEAD_DIM
    return jnp.sum(jnp.where(hh == cc, r3, 0.0), axis=0)


def _band_sample_kernel(q_ref, kn_ref, vn_ref, kc_ref, vc_ref, bc_ref, bn_ref, o_ref):
    scale = HEAD_DIM ** -0.5
    t = q_ref.shape[0]
    qr = _head_rows(q_ref[...], A_HEADS).astype(BF16)
    w = kc_ref.shape[4]
    kct = kc_ref[0, 0].reshape(A_WIDTH, w).astype(BF16)
    vct = vc_ref[0, 0].reshape(A_WIDTH, w).astype(BF16)
    kn = kn_ref[...].astype(BF16)
    vn = vn_ref[...].astype(BF16)
    s_c = jnp.dot(qr, kct, preferred_element_type=F32) * scale + bc_ref[...]
    s_n = lax.dot_general(qr, kn, NT_DIMS, preferred_element_type=F32) * scale + bn_ref[...]
    m = jnp.maximum(jnp.max(s_c, axis=-1, keepdims=True), jnp.max(s_n, axis=-1, keepdims=True))
    p_c = jnp.exp(s_c - m)
    p_n = jnp.exp(s_n - m)
    l = jnp.sum(p_c, axis=-1, keepdims=True) + jnp.sum(p_n, axis=-1, keepdims=True)
    r = (lax.dot_general(p_c.astype(BF16), vct, NT_DIMS, preferred_element_type=F32)
         + jnp.dot(p_n.astype(BF16), vn, preferred_element_type=F32)) / l
    o_ref[...] = _head_diag(r, A_HEADS, t).astype(o_ref.dtype)


def band_attention_sample(h, k_cache_t, v_cache_t, bias_c, bias_n, *, pair, row0, nb, t, name):
    base = row0 // t
    w = k_cache_t.shape[4]
    cache_spec = pl.BlockSpec((1, 1, A_HEADS, HEAD_DIM, w), lambda b: (pair, b, 0, 0, 0))
    return pl.pallas_call(
        _band_sample_kernel,
        grid=(nb,),
        in_specs=[pl.BlockSpec((t, A_WIDTH), lambda b: (base + b, 0)),
                  pl.BlockSpec((t, A_WIDTH), lambda b: (base + b, 1)),
                  pl.BlockSpec((t, A_WIDTH), lambda b: (base + b, 2)),
                  cache_spec,
                  cache_spec,
                  pl.BlockSpec((A_HEADS * t, w), lambda b: (0, 0)),
                  pl.BlockSpec((A_HEADS * t, t), lambda b: (0, 0))],
        out_specs=pl.BlockSpec((t, A_WIDTH), lambda b: (b, 0)),
        out_shape=jax.ShapeDtypeStruct((nb * t, A_WIDTH), BF16),
        compiler_params=_cparams("arbitrary"),
        name=name,
    )(h, h, h, k_cache_t, v_cache_t, bias_c, bias_n)


POOL_HALO = POOL_STATE + 1


def _pool_core(ext_ref, rows, pos0, pw_ref, sc_ref):
    pos = (pos0 + lax.broadcasted_iota(jnp.int32, (rows, 1), 0)).astype(F32)
    outs = []
    for g, w in enumerate(POOL_WINDOWS):
        sl = slice(g * B_GROUP, (g + 1) * B_GROUP)
        cur = ext_ref[POOL_HALO:POOL_HALO + rows, sl]
        win = cur
        for r in range(1, w):
            win = win + ext_ref[POOL_HALO - r:POOL_HALO - r + rows, sl]
        cnt = jnp.minimum(float(w), pos + 1.0)
        d = (win / cnt - cur).astype(BF16)
        outs.append(jnp.dot(d, pw_ref[g], preferred_element_type=F32))
    return jnp.concatenate(outs, axis=-1) * sc_ref[...]


def _pool_prompt_kernel(up_ref, uc_ref, pw_ref, sc_ref, o_ref, ext_ref, *, tm):
    i = pl.program_id(0)
    ext_ref[0:POOL_HALO, :] = jnp.where(i > 0, up_ref[...], 0.0)
    ext_ref[POOL_HALO:POOL_HALO + tm, :] = uc_ref[...]
    o_ref[...] = _pool_core(ext_ref, tm, i * tm, pw_ref, sc_ref).astype(o_ref.dtype)


def pool_prompt(h, pool_w, pool_scale, *, seq, tm, name):
    ucol = 3 * A_WIDTH // B_WIDTH
    per = tm // POOL_HALO
    return pl.pallas_call(
        functools.partial(_pool_prompt_kernel, tm=tm),
        grid=(seq // tm,),
        in_specs=[pl.BlockSpec((POOL_HALO, B_WIDTH), lambda i: (jnp.maximum(i * per - 1, 0), ucol)),
                  pl.BlockSpec((tm, B_WIDTH), lambda i: (i, ucol)),
                  pl.BlockSpec((len(POOL_WINDOWS), B_GROUP, B_GROUP), lambda i: (0, 0, 0)),
                  pl.BlockSpec((1, B_WIDTH), lambda i: (0, 0))],
        out_specs=pl.BlockSpec((tm, B_WIDTH), lambda i: (i, 0)),
        out_shape=jax.ShapeDtypeStruct((seq, B_WIDTH), BF16),
        scratch_shapes=[pltpu.VMEM((POOL_HALO + tm, B_WIDTH), F32)],
        compiler_params=_cparams("arbitrary"),
        name=name,
    )(h, h, pool_w, pool_scale.reshape(1, B_WIDTH))


def _pool_sample_kernel(hist_ref, u_ref, pw_ref, sc_ref, o_ref, ext_ref, *, t, pos0):
    ext_ref[0:POOL_HALO, :] = hist_ref[0]
    ext_ref[POOL_HALO:POOL_HALO + t, :] = u_ref[...]
    o_ref[...] = _pool_core(ext_ref, t, pos0, pw_ref, sc_ref).astype(o_ref.dtype)


def pool_sample(h, hist, pool_w, pool_scale, *, row0, nb, t, pos0, name):
    ucol = 3 * A_WIDTH // B_WIDTH
    base = row0 // t
    return pl.pallas_call(
        functools.partial(_pool_sample_kernel, t=t, pos0=pos0),
        grid=(nb,),
        in_specs=[pl.BlockSpec((1, POOL_HALO, B_WIDTH), lambda b: (b, 0, 0)),
                  pl.BlockSpec((t, B_WIDTH), lambda b: (base + b, ucol)),
                  pl.BlockSpec((len(POOL_WINDOWS), B_GROUP, B_GROUP), lambda b: (0, 0, 0)),
                  pl.BlockSpec((1, B_WIDTH), lambda b: (0, 0))],
        out_specs=pl.BlockSpec((t, B_WIDTH), lambda b: (b, 0)),
        out_shape=jax.ShapeDtypeStruct((nb * t, B_WIDTH), BF16),
        scratch_shapes=[pltpu.VMEM((POOL_HALO + t, B_WIDTH), F32)],
        compiler_params=_cparams("arbitrary"),
        name=name,
    )(hist, h, pool_w, pool_scale.reshape(1, B_WIDTH))


CUMSUM_ALIGN = 8 * LANES


def _cumsum_kernel(x_ref, before_ref, o_ref):
    rows = x_ref.shape[1]
    r = lax.broadcasted_iota(jnp.int32, (LANES, LANES), 0)
    c = lax.broadcasted_iota(jnp.int32, (LANES, LANES), 1)
    upper = (r <= c).astype(F32)
    local = jnp.dot(x_ref[0], upper, preferred_element_type=F32, precision=lax.Precision.HIGHEST)
    tot = jnp.broadcast_to(local[:, LANES - 1:LANES], (rows, LANES))
    o_ref[0] = local + jnp.dot(before_ref[...], tot, preferred_element_type=F32, precision=lax.Precision.HIGHEST)


def cumsum_lanes(x, *, name):
    nb, nh, ln = x.shape
    nblk = ln // LANES
    rows = nh * nblk
    rid = jnp.arange(rows)
    before = ((rid[:, None] // nblk == rid[None, :] // nblk) & (rid[None, :] < rid[:, None])).astype(F32)
    out = pl.pallas_call(
        _cumsum_kernel,
        grid=(nb,),
        in_specs=[pl.BlockSpec((1, rows, LANES), lambda b: (b, 0, 0)),
                  pl.BlockSpec((rows, rows), lambda b: (0, 0))],
        out_specs=pl.BlockSpec((1, rows, LANES), lambda b: (b, 0, 0)),
        out_shape=jax.ShapeDtypeStruct((nb, rows, LANES), F32),
        compiler_params=_cparams("arbitrary"),
        name=name,
    )(x.reshape(nb, rows, LANES), before)
    return out.reshape(nb, nh, ln)


def _cumsum_rows_kernel(x_ref, o_ref, offs_ref, *, nblk):
    r = lax.broadcasted_iota(jnp.int32, (LANES, LANES), 0)
    c = lax.broadcasted_iota(jnp.int32, (LANES, LANES), 1)
    lower = (c <= r).astype(F32)

    def local(b, carry):
        off = pl.multiple_of(b * LANES, LANES)
        o_ref[pl.ds(off, LANES), :] = jnp.dot(lower, x_ref[pl.ds(off, LANES), :], preferred_element_type=F32,
                                              precision=lax.Precision.HIGHEST)
        return carry

    lax.fori_loop(0, nblk, local, 0)
    totals = o_ref[pl.ds(LANES - 1, nblk, stride=LANES), :]
    rb = lax.broadcasted_iota(jnp.int32, (nblk, nblk), 0)
    cb = lax.broadcasted_iota(jnp.int32, (nblk, nblk), 1)
    offs_ref[...] = jnp.dot((cb < rb).astype(F32), totals, preferred_element_type=F32,
                            precision=lax.Precision.HIGHEST)

    def shift(b, carry):
        off = pl.multiple_of(b * LANES, LANES)
        o_ref[pl.ds(off, LANES), :] = o_ref[pl.ds(off, LANES), :] + offs_ref[pl.ds(b, 1), :]
        return carry

    lax.fori_loop(0, nblk, shift, 0)


def cumsum_rows(x, *, name):
    ln, nh = x.shape
    nblk = ln // LANES
    return pl.pallas_call(
        functools.partial(_cumsum_rows_kernel, nblk=nblk),
        grid=(1,),
        in_specs=[pl.BlockSpec((ln, nh), lambda i: (0, 0))],
        out_specs=pl.BlockSpec((ln, nh), lambda i: (0, 0)),
        out_shape=jax.ShapeDtypeStruct((ln, nh), F32),
        scratch_shapes=[pltpu.VMEM((nblk, nh), F32)],
        compiler_params=_cparams("arbitrary"),
        name=name,
    )(x)


FA_CHUNK = 64
FA_QCOLS = 256


def _split3(x):
    hi = x.astype(BF16)
    r = x - hi.astype(F32)
    mid = r.astype(BF16)
    lo = (r - mid.astype(F32)).astype(BF16)
    return hi, mid, lo


def _spare_lane_columns(lane, base, first, second):
    out = jnp.zeros(jnp.broadcast_shapes(lane.shape, first[0].shape, second[0].shape), F32)
    for j, val in enumerate(tuple(first) + tuple(second)):
        out = jnp.where(lane == base + j, val.astype(F32), out)
    return out


def _fa_prep_kernel(k_ref, v_ref, d_ref, ka_ref, va_ref):
    lane = lax.broadcasted_iota(jnp.int32, (1, LANES), 1)
    masks = _head_lane_masks()
    d = d_ref[...] * (-LOG2E)
    one = jnp.ones((1, 1), BF16)
    for h in range(C_HEADS):
        p, a = divmod(h, PAIR)
        sl = slice(p * LANES, (p + 1) * LANES)
        ext = _spare_lane_columns(lane, HEAD_DIM * (1 - a), _split3(d[:, h:h + 1]), (one, one, one))
        ka_ref[h] = jnp.where(masks[a], k_ref[:, sl].astype(F32), ext).astype(BF16)
        if a == 0:
            va_ref[p] = v_ref[:, sl].T.astype(BF16)


def fa_prep(qkv16, v32, dcum, *, seq, tr, name):
    kcol = 1
    npair = C_HEADS // PAIR
    return pl.pallas_call(
        _fa_prep_kernel,
        grid=(seq // tr,),
        in_specs=[pl.BlockSpec((tr, C_WIDTH), lambda r: (r, kcol)),
                  pl.BlockSpec((tr, C_WIDTH), lambda r: (r, 0)),
                  pl.BlockSpec((tr, C_HEADS), lambda r: (r, 0))],
        out_specs=[pl.BlockSpec((C_HEADS, tr, LANES), lambda r: (0, r, 0)),
                   pl.BlockSpec((npair, LANES, tr), lambda r: (0, 0, r))],
        out_shape=[jax.ShapeDtypeStruct((C_HEADS, seq, LANES), BF16),
                   jax.ShapeDtypeStruct((npair, LANES, seq), BF16)],
        compiler_params=_cparams("arbitrary"),
        name=name,
    )(qkv16, v32, dcum)


def _fa_prompt_kernel(q_ref, ka_ref, va_ref, d_ref, o_ref, m_ref, l_ref, acc_ref, s0_ref, s1_ref, p_ref, *, tq, tk):
    p = pl.program_id(0)
    qi = pl.program_id(1)
    lane = lax.broadcasted_iota(jnp.int32, (1, LANES), 1)
    head_col = lax.broadcasted_iota(jnp.int32, (1, C_HEADS), 1)
    masks = _head_lane_masks()
    q = q_ref[...].astype(F32) * (HEAD_DIM ** -0.5 * LOG2E)
    d0 = d_ref[0:1, :] * LOG2E
    one = jnp.ones((1, 1), BF16)
    qa = []
    for a in range(PAIR):
        dref = jnp.sum(jnp.where(head_col == PAIR * p + a, d0, 0.0), axis=-1, keepdims=True)
        ext = _spare_lane_columns(lane, HEAD_DIM * (1 - a), (one, one, one), _split3(dref))
        qa.append(jnp.where(masks[a], q, ext).astype(BF16))

    m_ref[...] = jnp.full(m_ref.shape, NEG_INF, F32)
    l_ref[...] = jnp.zeros(l_ref.shape, F32)
    acc_ref[...] = jnp.zeros(acc_ref.shape, F32)

    jd = (qi * tq) // tk
    nchunk = tk // FA_CHUNK
    s_refs = (s0_ref, s1_ref)

    def scores(j, slot, diagonal):
        off = pl.multiple_of(j * tk, tk)
        s_ref = s_refs[slot]
        for a in range(PAIR):
            st = lax.dot_general(ka_ref[a, pl.ds(off, tk), :], qa[a], NT_DIMS, preferred_element_type=F32)
            if diagonal:
                krow = lax.broadcasted_iota(jnp.int32, (tk, tq), 0)
                qcol = lax.broadcasted_iota(jnp.int32, (tk, tq), 1)
                st = jnp.where(krow - qcol <= qi * tq - j * tk, st, NEG_INF)
            s_ref[a] = st

    def absorb(j, slot):
        off = pl.multiple_of(j * tk, tk)
        s_ref = s_refs[slot]
        for a in range(PAIR):
            vt = va_ref[0, a * HEAD_DIM:(a + 1) * HEAD_DIM, pl.ds(off, tk)]
            for g in range(tq // FA_QCOLS):
                cols = slice(g * FA_QCOLS, (g + 1) * FA_QCOLS)
                cm = None
                for c in range(nchunk):
                    x = s_ref[a, c * FA_CHUNK:(c + 1) * FA_CHUNK, cols].reshape(FA_CHUNK // 8, 8, FA_QCOLS)
                    part = jnp.max(x, axis=0)
                    cm = part if cm is None else jnp.maximum(cm, part)
                m_old = m_ref[a, :, cols]
                m_new = jnp.maximum(m_old, jnp.max(cm, axis=0, keepdims=True))
                alpha = jnp.exp2(m_old - m_new)
                cs = None
                for c in range(nchunk):
                    rows = slice(c * FA_CHUNK, (c + 1) * FA_CHUNK)
                    pc = jnp.exp2(s_ref[a, rows, cols] - m_new)
                    p_ref[a, rows, cols] = pc.astype(BF16)
                    part = jnp.sum(pc.reshape(FA_CHUNK // 8, 8, FA_QCOLS), axis=0)
                    cs = part if cs is None else cs + part
                pv = jnp.dot(vt, p_ref[a, :, cols], preferred_element_type=F32)
                acc_ref[a, :, cols] = alpha * acc_ref[a, :, cols] + pv
                l_ref[a, :, cols] = alpha * l_ref[a, :, cols] + jnp.sum(cs, axis=0, keepdims=True)
                m_ref[a, :, cols] = m_new

    scores(jd, 0, True)

    def body(u, carry):
        scores(2 * u, 1, False)
        absorb(jnp.where(u == 0, jd, 2 * u - 1), 0)
        scores(2 * u + 1, 0, False)
        absorb(2 * u, 1)
        return carry

    npairs = jd // 2
    lax.fori_loop(0, npairs, body, 0)
    last0 = jnp.where(npairs == 0, jd, 2 * npairs - 1)

    @pl.when(jd % 2 == 1)
    def _():
        scores(jd - 1, 1, False)
        absorb(last0, 0)
        absorb(jd - 1, 1)

    @pl.when(jd % 2 == 0)
    def _():
        absorb(last0, 0)

    out = jnp.concatenate([acc_ref[a] / l_ref[a] for a in range(PAIR)], axis=0)
    o_ref[...] = out.T.astype(o_ref.dtype)


def forgetting_attention_prompt(qkv16, ka, va, dcum, *, seq, tq, tk, name):
    npair = C_WIDTH // LANES
    return pl.pallas_call(
        functools.partial(_fa_prompt_kernel, tq=tq, tk=tk),
        grid=(npair, seq // tq),
        in_specs=[pl.BlockSpec((tq, LANES), lambda p, i: (i, p)),
                  pl.BlockSpec((PAIR, seq, LANES), lambda p, i: (p, 0, 0)),
                  pl.BlockSpec((1, LANES, seq), lambda p, i: (p, 0, 0)),
                  pl.BlockSpec((8, C_HEADS), lambda p, i: (i * (tq // 8), 0))],
        out_specs=pl.BlockSpec((tq, LANES), lambda p, i: (i, p)),
        out_shape=jax.ShapeDtypeStruct((seq, C_WIDTH), BF16),
        scratch_shapes=[pltpu.VMEM((PAIR, 1, tq), F32),
                        pltpu.VMEM((PAIR, 1, tq), F32),
                        pltpu.VMEM((PAIR, HEAD_DIM, tq), F32),
                        pltpu.VMEM((PAIR, tk, tq), F32),
                        pltpu.VMEM((PAIR, tk, tq), F32),
                        pltpu.VMEM((PAIR, tk, tq), BF16)],
        compiler_params=_cparams("arbitrary", "arbitrary"),
        name=name,
    )(qkv16, ka, va, dcum)


def _fa_sample_kernel(q_ref, kn_ref, vn_ref, kc_ref, vc_ref, dc_ref, dp_ref, dn_ref, o_ref,
                      qr_ref, m_ref, l_ref, acc_ref, *, t, nkc):
    kc_i = pl.program_id(1)
    scale = HEAD_DIM ** -0.5
    rows = C_HEADS * t

    @pl.when(kc_i == 0)
    def _():
        qr_ref[...] = _head_rows(q_ref[...], C_HEADS)
        m_ref[...] = jnp.full(m_ref.shape, NEG_INF, F32)
        l_ref[...] = jnp.zeros(l_ref.shape, F32)
        acc_ref[...] = jnp.zeros(acc_ref.shape, F32)

    qr = qr_ref[...]
    d_tot = dp_ref[0][:, :, LANES - 1:LANES]

    def update(s, v, dims=(((1,), (0,)), ((), ()))):
        m = m_ref[...]
        m_new = jnp.maximum(m, jnp.max(s, axis=-1, keepdims=True))
        alpha = jnp.exp(m - m_new)
        p = jnp.exp(s - m_new)
        l_ref[...] = alpha * l_ref[...] + jnp.sum(p, axis=-1, keepdims=True)
        acc_ref[...] = alpha * acc_ref[...] + lax.dot_general(p.astype(BF16), v, dims, preferred_element_type=F32)
        m_ref[...] = m_new

    tk = kc_ref.shape[4]
    kct = kc_ref[0, 0].reshape(C_WIDTH, tk).astype(BF16)
    vct = vc_ref[0, 0].reshape(C_WIDTH, tk).astype(BF16)
    bias_c = jnp.broadcast_to(d_tot - dc_ref[0], (C_HEADS, t, tk)).reshape(rows, tk)
    s_c = jnp.dot(qr, kct, preferred_element_type=F32) * scale + bias_c
    update(s_c, vct, NT_DIMS)

    @pl.when(kc_i == nkc - 1)
    def _():
        bias_n = jnp.broadcast_to(d_tot - dn_ref[0][:, :, 0:t], (C_HEADS, t, t)).reshape(rows, t)
        s_n = lax.dot_general(qr, kn_ref[...], NT_DIMS, preferred_element_type=F32) * scale + bias_n
        qpos = lax.broadcasted_iota(jnp.int32, (C_HEADS, t, t), 1).reshape(rows, t)
        kpos = lax.broadcasted_iota(jnp.int32, (C_HEADS, t, t), 2).reshape(rows, t)
        s_n = jnp.where(kpos <= qpos, s_n, NEG_INF)
        update(s_n, vn_ref[...])
        r = acc_ref[...] / l_ref[...]
        o_ref[...] = _head_diag(r, C_HEADS, t).astype(o_ref.dtype)


def forgetting_attention_sample(qkv, k_cache_t, v_cache_t, dall, *, pair, row0, nb, t, tk, name):
    base = row0 // t
    past = k_cache_t.shape[4]
    nkc = past // tk
    rows = C_HEADS * t
    return pl.pallas_call(
        functools.partial(_fa_sample_kernel, t=t, nkc=nkc),
        grid=(nb, nkc),
        in_specs=[pl.BlockSpec((t, C_WIDTH), lambda b, c: (base + b, 0)),
                  pl.BlockSpec((t, C_WIDTH), lambda b, c: (base + b, 1)),
                  pl.BlockSpec((t, C_WIDTH), lambda b, c: (base + b, 2)),
                  pl.BlockSpec((1, 1, C_HEADS, HEAD_DIM, tk), lambda b, c: (pair, b, 0, 0, c)),
                  pl.BlockSpec((1, 1, C_HEADS, HEAD_DIM, tk), lambda b, c: (pair, b, 0, 0, c)),
                  pl.BlockSpec((1, C_HEADS, 1, tk), lambda b, c: (b, 0, 0, c)),
                  pl.BlockSpec((1, C_HEADS, 1, LANES), lambda b, c: (b, 0, 0, past // LANES - 1)),
                  pl.BlockSpec((1, C_HEADS, 1, LANES), lambda b, c: (b, 0, 0, past // LANES))],
        out_specs=pl.BlockSpec((t, C_WIDTH), lambda b, c: (b, 0)),
        out_shape=jax.ShapeDtypeStruct((nb * t, C_WIDTH), BF16),
        scratch_shapes=[pltpu.VMEM((rows, C_WIDTH), BF16),
                        pltpu.VMEM((rows, 1), F32),
                        pltpu.VMEM((rows, 1), F32),
                        pltpu.VMEM((rows, C_WIDTH), F32)],
        compiler_params=_cparams("arbitrary", "arbitrary"),
        name=name,
    )(qkv, qkv, qkv, k_cache_t, v_cache_t, dall, dall, dall)


def _rel_bias_matrix(rel_table, nq, nk):
    span = nq + nk - 1
    diag = jnp.arange(span) - (nq - 1)
    vals = rel_table[:, jnp.clip(A_WINDOW - diag, -REL_CLIP, REL_CLIP) + REL_CLIP].astype(F32)
    nh = vals.shape[0]
    skew = jnp.tile(vals, (1, nq + 1))[:, :nq * (span + 1)].reshape(nh, nq, span + 1)
    return skew[:, ::-1, :nk]


def _band_block_bias(bias_chunk):
    nchunk = A_BLOCK // CHUNK
    parts = [jnp.pad(bias_chunk * LOG2E, ((0, 0), (0, 0), (CHUNK * c, A_KEYS - A_BAND - CHUNK * c)),
                     constant_values=NEG_INF) for c in range(nchunk)]
    later = jnp.stack(parts, axis=1).reshape(A_HEADS, A_BLOCK, A_KEYS)
    col = jnp.arange(A_KEYS)
    first = jnp.where(col >= A_WINDOW, later, NEG_INF)
    return jnp.stack([first, later])


def kernel(x_prompt, x_sample, cache_a_k, cache_a_v, state_pool, cache_c_k, cache_c_v, cache_c_logf,
           cache_mem_k, cache_mem_v, mem_prompt, w_in_ab, rel_bias_a, pool_w, pool_scale, w_out_ab,
           w_in_c, b_f, w_out_c, w_xq, w_xk, w_xv, w_xo, ln_g, ln_b, ffn_w1, ffn_w3, ffn_w2,
           w_router, b_router, moe_w1, moe_w3, moe_w2):
    bp, seq, d = x_prompt.shape
    nb, t, _ = x_sample.shape
    assert bp == 1 and d == D_MODEL
    ns = nb * t
    tot = seq + ns
    past = cache_c_k.shape[2]
    tm = 512
    assert tot % tm == 0 and seq % tm == 0

    x = jnp.concatenate([x_prompt.reshape(seq, d), x_sample.reshape(ns, d)], axis=0)

    mem = mem_prompt.reshape(N_MEM, d)
    p_mem_k, p_mem_v = [], []
    for layer in range(DEPTH):
        (mk,) = matmul(mem, w_xk[layer].astype(BF16), [F32], tm=N_MEM, tn=d, name=f"mem_k{layer}")
        (mv,) = matmul(mem, w_xv[layer].astype(BF16), [F32], tm=N_MEM, tn=d, name=f"mem_v{layer}")
        p_mem_k.append(mk)
        p_mem_v.append(mv)

    def cross_block(x, layer):
        (q,) = matmul(x, w_xq[layer].astype(BF16), [BF16], tm=tm, tn=d, name=f"xq{layer}")
        o_p = cross_attention(q, p_mem_k[layer].reshape(1, N_MEM, d), p_mem_v[layer].reshape(1, N_MEM, d),
                              row0=0, rows=seq, tq=tm, name=f"xattn_p{layer}")
        o_s = cross_attention(q, cache_mem_k[layer].reshape(nb, N_MEM, d).astype(BF16),
                              cache_mem_v[layer].reshape(nb, N_MEM, d).astype(BF16),
                              row0=seq, rows=ns, tq=t, name=f"xattn_s{layer}")
        return matmul_ln([(o_p, o_s)], w_xo[layer].astype(BF16), x, ln_g[layer, 1], ln_b[layer, 1], tm=tm,
                         name=f"xo{layer}")

    pr = 0
    (h,) = matmul(x, w_in_ab[pr].astype(BF16), [F32], tm=tm, tn=w_in_ab.shape[-1], name="in_ab")
    bias = _rel_bias_matrix(rel_bias_a[pr], CHUNK, A_BAND)
    oa_p = band_attention_prompt(h, _band_block_bias(bias), seq=seq, name="band_p")
    bias_s = bias[:, :t, :A_WINDOW + t].reshape(A_HEADS * t, A_WINDOW + t)
    feature_major = (0, 1, 3, 4, 2)
    oa_s = band_attention_sample(h, jnp.transpose(cache_a_k, feature_major), jnp.transpose(cache_a_v, feature_major),
                                 bias_s[:, :A_WINDOW], bias_s[:, A_WINDOW:], pair=pr, row0=seq, nb=nb, t=t,
                                 name="band_s")
    pw = pool_w[pr].astype(BF16)
    ob_p = pool_prompt(h, pw, pool_scale[pr], seq=seq, tm=tm, name="pool_p")
    hist = jnp.concatenate([jnp.zeros((nb, 1, B_WIDTH), F32), state_pool[pr]], axis=1)
    ob_s = pool_sample(h, hist, pw, pool_scale[pr], row0=seq, nb=nb, t=t, pos0=past, name="pool_s")
    x = matmul_ln([(oa_p, oa_s), (ob_p, ob_s)], w_out_ab[pr].astype(BF16), x, ln_g[0, 0], ln_b[0, 0], tm=tm,
                  name="out_ab")
    x = cross_block(x, 0)
    ones = jnp.ones((1, tot, 1), F32)
    x = moe_ln(x, ones, ffn_w1.astype(BF16), ffn_w3.astype(BF16), ffn_w2.astype(BF16),
               ln_g[0, 2], ln_b[0, 2], tm=tm, tf=D_FF // 2, name="ffn")

    k_a = h[:, A_WIDTH:2 * A_WIDTH]
    v_a = h[:, 2 * A_WIDTH:3 * A_WIDTH]
    u_b = h[:, 3 * A_WIDTH:]
    keep = min(A_WINDOW, seq)
    p_a_k = k_a[seq - keep:seq].reshape(1, 1, keep, A_HEADS, HEAD_DIM)
    p_a_v = v_a[seq - keep:seq].reshape(1, 1, keep, A_HEADS, HEAD_DIM)
    p_pool = u_b[seq - POOL_STATE:seq].reshape(1, 1, POOL_STATE, B_WIDTH)
    s_a_k = k_a[seq:].reshape(1, nb, t, A_HEADS, HEAD_DIM)
    s_a_v = v_a[seq:].reshape(1, nb, t, A_HEADS, HEAD_DIM)
    s_pool = jnp.concatenate([state_pool[pr], u_b[seq:].reshape(nb, t, B_WIDTH)], axis=1)[:, -POOL_STATE:][None]

    w_c = w_in_c[pr]
    w_cf = jnp.pad(w_c, ((0, 0), (0, LANES - C_HEADS))).astype(BF16)
    bf_pad = jnp.pad(b_f[pr].astype(F32), (0, LANES - C_HEADS)).reshape(1, LANES)
    qkv16, logf, k_cp, v_cp, k_cs, v_cs = matmul_qkv(x, w_cf, bf_pad, seq=seq, tm=tm, name="in_c")
    logf = logf[:, :C_HEADS]
    dcum_p = cumsum_rows(logf[:seq], name="cumsum_p")
    ka, va = fa_prep(qkv16, v_cp, dcum_p, seq=seq, tr=tm, name="fa_prep")
    oc_p = forgetting_attention_prompt(qkv16, ka, va, dcum_p, seq=seq, tq=tm, tk=tm, name="fa_p")
    lf_s = logf[seq:].reshape(nb, t, C_HEADS)
    lf_all = jnp.concatenate([cache_c_logf[pr].astype(F32), lf_s,
                              jnp.zeros((nb, CUMSUM_ALIGN - t, C_HEADS), F32)], axis=1)
    dall = cumsum_lanes(jnp.swapaxes(lf_all, 1, 2), name="cumsum_s").reshape(nb, C_HEADS, 1, past + CUMSUM_ALIGN)
    oc_s = forgetting_attention_sample(qkv16, jnp.transpose(cache_c_k, feature_major),
                                       jnp.transpose(cache_c_v, feature_major), dall, pair=pr,
                                       row0=seq, nb=nb, t=t, tk=2048, name="fa_s")
    x = matmul_ln([(oc_p, oc_s)], w_out_c[pr].astype(BF16), x, ln_g[1, 0], ln_b[1, 0], tm=tm, name="out_c")
    x = cross_block(x, 1)
    w_r = jnp.pad(w_router[pr], ((0, 0), (0, LANES - N_EXPERTS)))
    b_r = jnp.concatenate([b_router[pr].astype(F32), jnp.full((LANES - N_EXPERTS,), NEG_INF, F32)]).reshape(1, LANES)
    y_p, y_s = moe_routed_ln(x, w_r, b_r, moe_w1[pr].astype(BF16), moe_w3[pr].astype(BF16),
                             moe_w2[pr].astype(BF16), ln_g[1, 2], ln_b[1, 2], tm=tm, split=seq, name="moe")

    p_c_k = k_cp.reshape(1, 1, seq, C_HEADS, HEAD_DIM)
    p_c_v = v_cp.reshape(1, 1, seq, C_HEADS, HEAD_DIM)
    p_c_logf = logf[:seq].reshape(1, 1, seq, C_HEADS)
    s_c_k = k_cs.reshape(1, nb, t, C_HEADS, HEAD_DIM)
    s_c_v = v_cs.reshape(1, nb, t, C_HEADS, HEAD_DIM)
    s_c_logf = lf_s[None]

    y_prompt = y_p.reshape(1, seq, d)
    y_sample = y_s.reshape(nb, t, d)
    pmk = jnp.stack(p_mem_k).reshape(DEPTH, 1, N_MEM, X_HEADS, X_HEAD_DIM)
    pmv = jnp.stack(p_mem_v).reshape(DEPTH, 1, N_MEM, X_HEADS, X_HEAD_DIM)
    return (y_prompt, y_sample, p_a_k, p_a_v, p_pool, p_c_k, p_c_v, p_c_logf, pmk, pmv,
            s_a_k, s_a_v, s_pool, s_c_k, s_c_v, s_c_logf)
```

```python
import functools

import jax
import jax.numpy as jnp
from jax import lax
from jax.experimental import pallas as pl
from jax.experimental.pallas import tpu as pltpu

F32 = jnp.float32
BF16 = jnp.bfloat16

D_MODEL = 1024
DEPTH = 2
CHUNK = 64
N_MEM = 256
HEAD_DIM = 64
A_HEADS = 8
A_WIDTH = A_HEADS * HEAD_DIM
A_PREV_CHUNKS = 8
A_WINDOW = A_PREV_CHUNKS * CHUNK
A_BAND = (A_PREV_CHUNKS + 1) * CHUNK
REL_CLIP = 128
POOL_WINDOWS = (2, 4, 8, 16)
B_WIDTH = D_MODEL - A_WIDTH
B_GROUP = B_WIDTH // len(POOL_WINDOWS)
POOL_STATE = max(POOL_WINDOWS) - 1
C_HEADS = D_MODEL // HEAD_DIM
C_WIDTH = C_HEADS * HEAD_DIM
X_HEADS = 4
X_HEAD_DIM = D_MODEL // X_HEADS
D_FF = 2816
N_EXPERTS = 8
ALPHA = (2.0 * DEPTH) ** 0.25
LN_EPS = 1e-5
NEG_INF = -1e30
LOG2E = 1.4426950408889634

LANES = 128
PAIR = LANES // HEAD_DIM
VMEM_LIMIT = 56 * 1024 * 1024

NT_DIMS = (((1,), (1,)), ((), ()))


def _cparams(*sem):
    return pltpu.CompilerParams(dimension_semantics=sem, vmem_limit_bytes=VMEM_LIMIT)


def _deepnorm_ln(res, sub, g, b):
    z = ALPHA * res + sub
    mu = jnp.mean(z, axis=-1, keepdims=True)
    zc = z - mu
    var = jnp.mean(zc * zc, axis=-1, keepdims=True)
    return zc * lax.rsqrt(var + LN_EPS) * g + b


def _mm_kernel(x_ref, w_ref, *o_refs):
    y = jnp.dot(x_ref[...].astype(BF16), w_ref[...], preferred_element_type=F32)
    for o_ref in o_refs:
        o_ref[...] = y.astype(o_ref.dtype)


def matmul(x, w, out_dtypes, *, tm, tn, name):
    m, k = x.shape
    n = w.shape[1]
    outs = pl.pallas_call(
        _mm_kernel,
        grid=(n // tn, m // tm),
        in_specs=[pl.BlockSpec((tm, k), lambda j, i: (i, 0)),
                  pl.BlockSpec((k, tn), lambda j, i: (0, j))],
        out_specs=[pl.BlockSpec((tm, tn), lambda j, i: (i, j)) for _ in out_dtypes],
        out_shape=[jax.ShapeDtypeStruct((m, n), dt) for dt in out_dtypes],
        compiler_params=_cparams("arbitrary", "arbitrary"),
        name=name,
    )(x, w)
    return outs


def _mm_qkv_kernel(x_ref, w_ref, bf_ref, qkv_ref, lf_ref, kp_ref, vp_ref, ks_ref, vs_ref, *, n_prompt):
    i = pl.program_id(0)
    y = jnp.dot(x_ref[...].astype(BF16), w_ref[...], preferred_element_type=F32)
    qkv_ref[...] = y[:, :3 * C_WIDTH].astype(BF16)
    lf_ref[...] = jax.nn.log_sigmoid(y[:, 3 * C_WIDTH:] + bf_ref[...])
    k = y[:, C_WIDTH:2 * C_WIDTH]
    v = y[:, 2 * C_WIDTH:3 * C_WIDTH]

    @pl.when(i < n_prompt)
    def _():
        kp_ref[...] = k
        vp_ref[...] = v

    @pl.when(i >= n_prompt)
    def _():
        ks_ref[...] = k
        vs_ref[...] = v


def matmul_qkv(x, w, b_f, *, seq, tm, name):
    m, d = x.shape
    n = 3 * C_WIDTH
    n_prompt = seq // tm
    n_sample = (m - seq) // tm
    prompt_map = lambda i: (jnp.minimum(i, n_prompt - 1), 0)
    sample_map = lambda i: (jnp.maximum(i - n_prompt, 0), 0)
    return pl.pallas_call(
        functools.partial(_mm_qkv_kernel, n_prompt=n_prompt),
        grid=(n_prompt + n_sample,),
        in_specs=[pl.BlockSpec((tm, d), lambda i: (i, 0)),
                  pl.BlockSpec((d, n + LANES), lambda i: (0, 0)),
                  pl.BlockSpec((1, LANES), lambda i: (0, 0))],
        out_specs=[pl.BlockSpec((tm, n), lambda i: (i, 0)),
                   pl.BlockSpec((tm, LANES), lambda i: (i, 0)),
                   pl.BlockSpec((tm, C_WIDTH), prompt_map), pl.BlockSpec((tm, C_WIDTH), prompt_map),
                   pl.BlockSpec((tm, C_WIDTH), sample_map), pl.BlockSpec((tm, C_WIDTH), sample_map)],
        out_shape=[jax.ShapeDtypeStruct((m, n), BF16),
                   jax.ShapeDtypeStruct((m, LANES), F32),
                   jax.ShapeDtypeStruct((seq, C_WIDTH), F32), jax.ShapeDtypeStruct((seq, C_WIDTH), F32),
                   jax.ShapeDtypeStruct((m - seq, C_WIDTH), F32), jax.ShapeDtypeStruct((m - seq, C_WIDTH), F32)],
        compiler_params=_cparams("arbitrary"),
        name=name,
    )(x, w, b_f)


def _mm_ln_kernel(*refs, n_groups, n_first):
    x_refs = refs[:2 * n_groups]
    w_ref, res_ref, g_ref, b_ref, o_ref = refs[2 * n_groups:]
    i = pl.program_id(0)

    def run(xs):
        y, k0 = None, 0
        for x_ref in xs:
            kg = x_ref.shape[1]
            part = jnp.dot(x_ref[...].astype(BF16), w_ref[k0:k0 + kg, :], preferred_element_type=F32)
            y = part if y is None else y + part
            k0 += kg
        o_ref[...] = _deepnorm_ln(res_ref[...], y, g_ref[...], b_ref[...])

    @pl.when(i < n_first)
    def _():
        run(x_refs[0::2])

    @pl.when(i >= n_first)
    def _():
        run(x_refs[1::2])


def matmul_ln(groups, w, res, g, b, *, tm, name):
    m, n = res.shape
    split = groups[0][0].shape[0]
    n_first = split // tm
    first_map = lambda i: (jnp.minimum(i, n_first - 1), 0)
    rest_map = lambda i: (jnp.maximum(i - n_first, 0), 0)
    x_specs, x_args = [], []
    for xf, xr in groups:
        x_specs += [pl.BlockSpec((tm, xf.shape[1]), first_map), pl.BlockSpec((tm, xr.shape[1]), rest_map)]
        x_args += [xf, xr]
    return pl.pallas_call(
        functools.partial(_mm_ln_kernel, n_groups=len(groups), n_first=n_first),
        grid=(m // tm,),
        in_specs=x_specs + [pl.BlockSpec(w.shape, lambda i: (0, 0)),
                            pl.BlockSpec((tm, n), lambda i: (i, 0)),
                            pl.BlockSpec((1, n), lambda i: (0, 0)),
                            pl.BlockSpec((1, n), lambda i: (0, 0))],
        out_specs=pl.BlockSpec((tm, n), lambda i: (i, 0)),
        out_shape=jax.ShapeDtypeStruct((m, n), F32),
        compiler_params=_cparams("arbitrary"),
        name=name,
    )(*x_args, w, res, g.reshape(1, n), b.reshape(1, n))


def _moe_kernel(x_ref, comb_ref, w1_ref, w3_ref, w2_ref, g_ref, b_ref, o_ref, acc_ref, *, ne, nf):
    e = pl.program_id(1)
    f = pl.program_id(2)

    @pl.when((e == 0) & (f == 0))
    def _():
        acc_ref[...] = jnp.zeros_like(acc_ref)

    x = x_ref[...].astype(BF16)
    h1 = jnp.dot(x, w1_ref[0], preferred_element_type=F32)
    h3 = jnp.dot(x, w3_ref[0], preferred_element_type=F32)
    a = (jax.nn.silu(h1) * h3).astype(BF16)
    y = jnp.dot(a, w2_ref[0], preferred_element_type=F32)
    acc_ref[...] += comb_ref[0] * y

    @pl.when((e == ne - 1) & (f == nf - 1))
    def _():
        o_ref[...] = _deepnorm_ln(x_ref[...], acc_ref[...], g_ref[...], b_ref[...])


def moe_ln(x, comb, w1, w3, w2, g, b, *, tm, tf, name):
    m, d = x.shape
    ne, _, dff = w1.shape
    nf = dff // tf
    return pl.pallas_call(
        functools.partial(_moe_kernel, ne=ne, nf=nf),
        grid=(m // tm, ne, nf),
        in_specs=[pl.BlockSpec((tm, d), lambda i, e, f: (i, 0)),
                  pl.BlockSpec((1, tm, 1), lambda i, e, f: (e, i, 0)),
                  pl.BlockSpec((1, d, tf), lambda i, e, f: (e, 0, f)),
                  pl.BlockSpec((1, d, tf), lambda i, e, f: (e, 0, f)),
                  pl.BlockSpec((1, tf, d), lambda i, e, f: (e, f, 0)),
                  pl.BlockSpec((1, d), lambda i, e, f: (0, 0)),
                  pl.BlockSpec((1, d), lambda i, e, f: (0, 0))],
        out_specs=pl.BlockSpec((tm, d), lambda i, e, f: (i, 0)),
        out_shape=jax.ShapeDtypeStruct((m, d), F32),
        scratch_shapes=[pltpu.VMEM((tm, d), F32)],
        compiler_params=_cparams("arbitrary", "arbitrary", "arbitrary"),
        name=name,
    )(x, comb, w1, w3, w2, g.reshape(1, d), b.reshape(1, d))


ROUTE_I1, ROUTE_I2, ROUTE_G1, ROUTE_G2, ROUTE_R1, ROUTE_R2 = range(6)
EXPERT_TILE = 256


def _router_kernel(x_ref, w_ref, b_ref, o_ref, cnt_ref, carry_ref):
    i = pl.program_id(0)

    @pl.when(i == 0)
    def _():
        carry_ref[...] = jnp.zeros_like(carry_ref)

    tm = x_ref.shape[0]
    logits = jnp.dot(x_ref[...], w_ref[...], preferred_element_type=F32,
                     precision=lax.Precision.HIGHEST) + b_ref[...]
    lane = lax.broadcasted_iota(jnp.int32, logits.shape, 1)
    big = jnp.int32(LANES)
    v1 = jnp.max(logits, axis=-1, keepdims=True)
    i1 = jnp.min(jnp.where(logits == v1, lane, big), axis=-1, keepdims=True)
    rest = jnp.where(lane == i1, NEG_INF, logits)
    v2 = jnp.max(rest, axis=-1, keepdims=True)
    i2 = jnp.min(jnp.where(rest == v2, lane, big), axis=-1, keepdims=True)
    e2 = jnp.exp(v2 - v1)
    den = 1.0 + e2
    oh1 = lane == i1
    oh2 = lane == i2
    cnt = jnp.where(oh1 | oh2, 1.0, 0.0)
    r = lax.broadcasted_iota(jnp.int32, (tm, tm), 0)
    c = lax.broadcasted_iota(jnp.int32, (tm, tm), 1)
    earlier = jnp.where(c < r, 1.0, 0.0).astype(BF16)
    before = jnp.dot(earlier, cnt.astype(BF16), preferred_element_type=F32) + carry_ref[...]
    r1 = jnp.sum(jnp.where(oh1, before, 0.0), axis=-1, keepdims=True)
    r2 = jnp.sum(jnp.where(oh2, before, 0.0), axis=-1, keepdims=True)
    carry_ref[...] += jnp.sum(cnt, axis=0, keepdims=True)
    cnt_ref[...] = carry_ref[...]
    out = jnp.zeros(logits.shape, F32)
    for ln, val in ((ROUTE_I1, i1.astype(F32)), (ROUTE_I2, i2.astype(F32)), (ROUTE_G1, 1.0 / den),
                    (ROUTE_G2, e2 / den), (ROUTE_R1, r1), (ROUTE_R2, r2)):
        out = jnp.where(lane == ln, val, out)
    o_ref[...] = out


def router(x, w_pad, b_pad, *, tm, name):
    m, d = x.shape
    return pl.pallas_call(
        _router_kernel,
        grid=(m // tm,),
        in_specs=[pl.BlockSpec((tm, d), lambda i: (i, 0)),
                  pl.BlockSpec((d, LANES), lambda i: (0, 0)),
                  pl.BlockSpec((1, LANES), lambda i: (0, 0))],
        out_specs=[pl.BlockSpec((tm, LANES), lambda i: (i, 0)),
                   pl.BlockSpec((1, LANES), lambda i: (0, 0))],
        out_shape=[jax.ShapeDtypeStruct((m, LANES), F32), jax.ShapeDtypeStruct((1, LANES), F32)],
        scratch_shapes=[pltpu.VMEM((1, LANES), F32)],
        compiler_params=_cparams("arbitrary"),
        name=name,
    )(x, w_pad, b_pad)


def _gather_rows_start(src_hbm, idx_ref, n, dst, sem):
    def issue(r, carry):
        pltpu.make_async_copy(src_hbm.at[pl.ds(idx_ref[0, 0, r], 1)], dst.at[pl.ds(r, 1)], sem).start()
        return carry

    lax.fori_loop(0, n, issue, 0, unroll=8)


def _gather_rows_wait(src_hbm, n, dst, sem):
    pltpu.make_async_copy(src_hbm.at[pl.ds(0, n)], dst, sem).wait()


def _expert_kernel(te_ref, nt_ref, idx0_ref, idxn_ref, x_hbm, w1_ref, w3_ref, w2_ref, o_ref, xbuf, sem, *, tm, nf):
    m = pl.program_id(0)
    nt = nt_ref[0]
    slot = m % 2

    @pl.when(m == 0)
    def _():
        _gather_rows_start(x_hbm, idx0_ref, tm, xbuf.at[0], sem.at[0])

    @pl.when(m + 1 < nt)
    def _():
        _gather_rows_start(x_hbm, idxn_ref, tm, xbuf.at[1 - slot], sem.at[1 - slot])

    @pl.when(m < nt)
    def _():
        _gather_rows_wait(x_hbm, tm, xbuf.at[slot], sem.at[slot])
        x = xbuf[slot].astype(BF16)
        tf = w1_ref.shape[2] // nf
        y = None
        for f in range(nf):
            sl = slice(f * tf, (f + 1) * tf)
            h1 = jnp.dot(x, w1_ref[0, :, sl], preferred_element_type=F32)
            h3 = jnp.dot(x, w3_ref[0, :, sl], preferred_element_type=F32)
            a = (jax.nn.silu(h1) * h3).astype(BF16)
            part = jnp.dot(a, w2_ref[0, sl, :], preferred_element_type=F32)
            y = part if y is None else y + part
        o_ref[...] = y

    @pl.when(m >= nt)
    def _():
        o_ref[...] = jnp.zeros_like(o_ref)


def expert_tiles(x, w1, w3, w2, tile_expert, n_tiles, src_rows, *, tm, nf, name):
    m, d = x.shape
    ne, _, dff = w1.shape
    nt_max = src_rows.shape[0]

    def wmap(i, te, nt):
        return (te[jnp.minimum(i, nt[0] - 1)], 0, 0)

    grid_spec = pltpu.PrefetchScalarGridSpec(
        num_scalar_prefetch=2,
        grid=(nt_max,),
        in_specs=[pl.BlockSpec((1, 1, tm), lambda i, te, nt: (0, 0, 0), memory_space=pltpu.SMEM),
                  pl.BlockSpec((1, 1, tm), lambda i, te, nt: (jnp.minimum(i + 1, nt_max - 1), 0, 0),
                               memory_space=pltpu.SMEM),
                  pl.BlockSpec(memory_space=pl.ANY),
                  pl.BlockSpec((1, d, dff), wmap),
                  pl.BlockSpec((1, d, dff), wmap),
                  pl.BlockSpec((1, dff, d), wmap)],
        out_specs=pl.BlockSpec((tm, d), lambda i, te, nt: (i, 0)),
        scratch_shapes=[pltpu.VMEM((2, tm, d), F32), pltpu.SemaphoreType.DMA((2,))],
    )
    return pl.pallas_call(
        functools.partial(_expert_kernel, tm=tm, nf=nf),
        grid_spec=grid_spec,
        out_shape=jax.ShapeDtypeStruct((nt_max * tm, d), F32),
        compiler_params=_cparams("arbitrary"),
        name=name,
    )(tile_expert, n_tiles, src_rows, src_rows, x, w1, w3, w2)


def _combine_ln_kernel(idx0_ref, idxn_ref, y_hbm, x_ref, route_ref, g_ref, b_ref, op_ref, os_ref, ybuf, sem,
                       *, tm, nsteps, n_first):
    i = pl.program_id(0)
    slot = i % 2

    @pl.when(i == 0)
    def _():
        _gather_rows_start(y_hbm, idx0_ref, 2 * tm, ybuf.at[0], sem.at[0])

    @pl.when(i + 1 < nsteps)
    def _():
        _gather_rows_start(y_hbm, idxn_ref, 2 * tm, ybuf.at[1 - slot], sem.at[1 - slot])

    _gather_rows_wait(y_hbm, 2 * tm, ybuf.at[slot], sem.at[slot])
    route = route_ref[...]
    g1 = route[:, ROUTE_G1:ROUTE_G1 + 1]
    g2 = route[:, ROUTE_G2:ROUTE_G2 + 1]
    y = g1 * ybuf[slot, 0:tm, :] + g2 * ybuf[slot, tm:2 * tm, :]
    out = _deepnorm_ln(x_ref[...], y, g_ref[...], b_ref[...])

    @pl.when(i < n_first)
    def _():
        op_ref[...] = out

    @pl.when(i >= n_first)
    def _():
        os_ref[...] = out


def combine_ln(x, route, y_sorted, pos_rows, g, b, *, tm, split, name):
    m, d = x.shape
    nsteps = m // tm
    n_first = split // tm
    return pl.pallas_call(
        functools.partial(_combine_ln_kernel, tm=tm, nsteps=nsteps, n_first=n_first),
        grid=(nsteps,),
        in_specs=[pl.BlockSpec((1, 1, 2 * tm), lambda i: (0, 0, 0), memory_space=pltpu.SMEM),
                  pl.BlockSpec((1, 1, 2 * tm), lambda i: (jnp.minimum(i + 1, nsteps - 1), 0, 0),
                               memory_space=pltpu.SMEM),
                  pl.BlockSpec(memory_space=pl.ANY),
                  pl.BlockSpec((tm, d), lambda i: (i, 0)),
                  pl.BlockSpec((tm, LANES), lambda i: (i, 0)),
                  pl.BlockSpec((1, d), lambda i: (0, 0)),
                  pl.BlockSpec((1, d), lambda i: (0, 0))],
        out_specs=[pl.BlockSpec((tm, d), lambda i: (jnp.minimum(i, n_first - 1), 0)),
                   pl.BlockSpec((tm, d), lambda i: (jnp.maximum(i - n_first, 0), 0))],
        out_shape=[jax.ShapeDtypeStruct((split, d), F32), jax.ShapeDtypeStruct((m - split, d), F32)],
        scratch_shapes=[pltpu.VMEM((2, 2 * tm, d), F32), pltpu.SemaphoreType.DMA((2,))],
        compiler_params=_cparams("arbitrary"),
        name=name,
    )(pos_rows, pos_rows, y_sorted, x, route, g.reshape(1, d), b.reshape(1, d))


def moe_routed_ln(x, w_router_pad, b_router_pad, w1, w3, w2, g, b, *, tm, split, name):
    m, d = x.shape
    ne = w1.shape[0]
    te_rows = EXPERT_TILE
    route, counts = router(x, w_router_pad, b_router_pad, tm=tm, name=name + "_router")
    counts = counts[0, :ne].astype(jnp.int32)
    padded = (counts + te_rows - 1) // te_rows * te_rows
    upto = jnp.arange(ne)[None, :] <= jnp.arange(ne)[:, None]
    ends = jnp.sum(jnp.where(upto, padded[None, :], 0), axis=1)
    base = ends - padded
    i1 = route[:, ROUTE_I1].astype(jnp.int32)
    i2 = route[:, ROUTE_I2].astype(jnp.int32)
    pos1 = base[i1] + route[:, ROUTE_R1].astype(jnp.int32)
    pos2 = base[i2] + route[:, ROUTE_R2].astype(jnp.int32)
    nt_max = (2 * m + ne * (te_rows - 1)) // te_rows + 1
    tok = jnp.arange(m, dtype=jnp.int32)
    src = jnp.zeros((nt_max * te_rows,), jnp.int32).at[jnp.concatenate([pos1, pos2])].set(
        jnp.concatenate([tok, tok]))
    n_tiles = (ends[-1] // te_rows).astype(jnp.int32).reshape(1)
    tile_start = jnp.arange(nt_max, dtype=jnp.int32) * te_rows
    tile_expert = jnp.minimum(jnp.sum(tile_start[:, None] >= ends[None, :], axis=1), ne - 1).astype(jnp.int32)
    y_sorted = expert_tiles(x, w1, w3, w2, tile_expert, n_tiles, src.reshape(nt_max, 1, te_rows),
                            tm=te_rows, nf=2, name=name + "_experts")
    pos_rows = jnp.concatenate([pos1.reshape(m // tm, 1, tm), pos2.reshape(m // tm, 1, tm)], axis=-1)
    return combine_ln(x, route, y_sorted, pos_rows, g, b, tm=tm, split=split, name=name + "_combine")


def _xattn_kernel(q_ref, mk_ref, mv_ref, o_ref):
    scale = X_HEAD_DIM ** -0.5
    for h in range(X_HEADS):
        sl = slice(h * X_HEAD_DIM, (h + 1) * X_HEAD_DIM)
        q = q_ref[:, sl]
        k = mk_ref[0, :, sl].astype(BF16)
        v = mv_ref[0, :, sl].astype(BF16)
        s = lax.dot_general(q, k, NT_DIMS, preferred_element_type=F32) * scale
        m = jnp.max(s, axis=-1, keepdims=True)
        p = jnp.exp(s - m)
        l = jnp.sum(p, axis=-1, keepdims=True)
        o = jnp.dot(p.astype(BF16), v, preferred_element_type=F32) / l
        o_ref[:, sl] = o.astype(o_ref.dtype)


def cross_attention(q, mem_k, mem_v, *, row0, rows, tq, name):
    d = q.shape[1]
    nb = mem_k.shape[0]
    per_b = rows // nb
    nq = per_b // tq
    base = row0 // tq
    return pl.pallas_call(
        _xattn_kernel,
        grid=(nb, nq),
        in_specs=[pl.BlockSpec((tq, d), lambda b, i: (base + b * nq + i, 0)),
                  pl.BlockSpec((1, N_MEM, d), lambda b, i: (b, 0, 0)),
                  pl.BlockSpec((1, N_MEM, d), lambda b, i: (b, 0, 0))],
        out_specs=pl.BlockSpec((tq, d), lambda b, i: (b * nq + i, 0)),
        out_shape=jax.ShapeDtypeStruct((rows, d), BF16),
        compiler_params=_cparams("arbitrary", "arbitrary"),
        name=name,
    )(q, mem_k, mem_v)


A_BLOCK = A_WINDOW
A_KEYS = A_WINDOW + A_BLOCK


def _head_lane_masks():
    lane = lax.broadcasted_iota(jnp.int32, (1, LANES), 1)
    return [(lane >= a * HEAD_DIM) & (lane < (a + 1) * HEAD_DIM) for a in range(PAIR)]


def _band_prompt_kernel(q_ref, kp_ref, kc_ref, vp_ref, vc_ref, bias_ref, o_ref):
    kk = jnp.concatenate([kp_ref[...].astype(BF16), kc_ref[...].astype(BF16)], axis=0)
    vv = jnp.concatenate([vp_ref[...].astype(BF16), vc_ref[...].astype(BF16)], axis=0)
    masks = _head_lane_masks()
    q = q_ref[...] * (HEAD_DIM ** -0.5 * LOG2E)
    out = None
    for a in range(PAIR):
        qm = jnp.where(masks[a], q, 0.0).astype(BF16)
        s = lax.dot_general(qm, kk, NT_DIMS, preferred_element_type=F32) + bias_ref[0, a]
        m = jnp.max(s, axis=-1, keepdims=True)
        p = jnp.exp2(s - m)
        l = jnp.sum(p, axis=-1, keepdims=True)
        o = jnp.dot(p.astype(BF16), vv, preferred_element_type=F32) / l
        out = o if out is None else jnp.where(masks[a], o, out)
    o_ref[...] = out.astype(o_ref.dtype)


def band_attention_prompt(h, bias, *, seq, name):
    nblk = seq // A_BLOCK
    npair = A_WIDTH // LANES
    qcol, kcol, vcol = 0, npair, 2 * npair
    return pl.pallas_call(
        _band_prompt_kernel,
        grid=(npair, nblk),
        in_specs=[pl.BlockSpec((A_BLOCK, LANES), lambda p, i: (i, qcol + p)),
                  pl.BlockSpec((A_BLOCK, LANES), lambda p, i: (jnp.maximum(i - 1, 0), kcol + p)),
                  pl.BlockSpec((A_BLOCK, LANES), lambda p, i: (i, kcol + p)),
                  pl.BlockSpec((A_BLOCK, LANES), lambda p, i: (jnp.maximum(i - 1, 0), vcol + p)),
                  pl.BlockSpec((A_BLOCK, LANES), lambda p, i: (i, vcol + p)),
                  pl.BlockSpec((1, PAIR, A_BLOCK, A_KEYS), lambda p, i: (jnp.minimum(i, 1), p, 0, 0))],
        out_specs=pl.BlockSpec((A_BLOCK, LANES), lambda p, i: (i, p)),
        out_shape=jax.ShapeDtypeStruct((seq, A_WIDTH), BF16),
        compiler_params=_cparams("arbitrary", "arbitrary"),
        name=name,
    )(h, h, h, h, h, bias)


def _head_rows(x, nheads):
    t, w = x.shape
    x3 = jnp.broadcast_to(x[None], (nheads, t, w))
    hh = lax.broadcasted_iota(jnp.int32, (nheads, 1, w), 0)
    cc = lax.broadcasted_iota(jnp.int32, (nheads, 1, w), 2) // HEAD_DIM
    return jnp.where(hh == cc, x3, jnp.zeros_like(x3)).reshape(nheads * t, w)


def _head_diag(r, nheads, t):
    w = r.shape[1]
    r3 = r.reshape(nheads, t, w)
    hh = lax.broadcasted_iota(jnp.int32, (nheads, 1, w), 0)
    cc = lax.broadcasted_iota(jnp.int32, (nheads, 1, w), 2) // HEAD_DIM
    return jnp.sum(jnp.where(hh == cc, r3, 0.0), axis=0)


def _band_sample_kernel(q_ref, kn_ref, vn_ref, kc_ref, vc_ref, bc_ref, bn_ref, o_ref):
    scale = HEAD_DIM ** -0.5
    t = q_ref.shape[0]
    qr = _head_rows(q_ref[...], A_HEADS).astype(BF16)
    w = kc_ref.shape[4]
    kct = kc_ref[0, 0].reshape(A_WIDTH, w).astype(BF16)
    vct = vc_ref[0, 0].reshape(A_WIDTH, w).astype(BF16)
    kn = kn_ref[...].astype(BF16)
    vn = vn_ref[...].astype(BF16)
    s_c = jnp.dot(qr, kct, preferred_element_type=F32) * scale + bc_ref[...]
    s_n = lax.dot_general(qr, kn, NT_DIMS, preferred_element_type=F32) * scale + bn_ref[...]
    m = jnp.maximum(jnp.max(s_c, axis=-1, keepdims=True), jnp.max(s_n, axis=-1, keepdims=True))
    p_c = jnp.exp(s_c - m)
    p_n = jnp.exp(s_n - m)
    l = jnp.sum(p_c, axis=-1, keepdims=True) + jnp.sum(p_n, axis=-1, keepdims=True)
    r = (lax.dot_general(p_c.astype(BF16), vct, NT_DIMS, preferred_element_type=F32)
         + jnp.dot(p_n.astype(BF16), vn, preferred_element_type=F32)) / l
    o_ref[...] = _head_diag(r, A_HEADS, t).astype(o_ref.dtype)


def band_attention_sample(h, k_cache_t, v_cache_t, bias_c, bias_n, *, pair, row0, nb, t, name):
    base = row0 // t
    w = k_cache_t.shape[4]
    cache_spec = pl.BlockSpec((1, 1, A_HEADS, HEAD_DIM, w), lambda b: (pair, b, 0, 0, 0))
    return pl.pallas_call(
        _band_sample_kernel,
        grid=(nb,),
        in_specs=[pl.BlockSpec((t, A_WIDTH), lambda b: (base + b, 0)),
                  pl.BlockSpec((t, A_WIDTH), lambda b: (base + b, 1)),
                  pl.BlockSpec((t, A_WIDTH), lambda b: (base + b, 2)),
                  cache_spec,
                  cache_spec,
                  pl.BlockSpec((A_HEADS * t, w), lambda b: (0, 0)),
                  pl.BlockSpec((A_HEADS * t, t), lambda b: (0, 0))],
        out_specs=pl.BlockSpec((t, A_WIDTH), lambda b: (b, 0)),
        out_shape=jax.ShapeDtypeStruct((nb * t, A_WIDTH), BF16),
        compiler_params=_cparams("arbitrary"),
        name=name,
    )(h, h, h, k_cache_t, v_cache_t, bias_c, bias_n)


POOL_HALO = POOL_STATE + 1


def _pool_core(ext_ref, rows, pos0, pw_ref, sc_ref):
    pos = (pos0 + lax.broadcasted_iota(jnp.int32, (rows, 1), 0)).astype(F32)
    outs = []
    for g, w in enumerate(POOL_WINDOWS):
        sl = slice(g * B_GROUP, (g + 1) * B_GROUP)
        cur = ext_ref[POOL_HALO:POOL_HALO + rows, sl]
        win = cur
        for r in range(1, w):
            win = win + ext_ref[POOL_HALO - r:POOL_HALO - r + rows, sl]
        cnt = jnp.minimum(float(w), pos + 1.0)
        d = (win / cnt - cur).astype(BF16)
        outs.append(jnp.dot(d, pw_ref[g], preferred_element_type=F32))
    return jnp.concatenate(outs, axis=-1) * sc_ref[...]


def _pool_prompt_kernel(up_ref, uc_ref, pw_ref, sc_ref, o_ref, ext_ref, *, tm):
    i = pl.program_id(0)
    ext_ref[0:POOL_HALO, :] = jnp.where(i > 0, up_ref[...], 0.0)
    ext_ref[POOL_HALO:POOL_HALO + tm, :] = uc_ref[...]
    o_ref[...] = _pool_core(ext_ref, tm, i * tm, pw_ref, sc_ref).astype(o_ref.dtype)


def pool_prompt(h, pool_w, pool_scale, *, seq, tm, name):
    ucol = 3 * A_WIDTH // B_WIDTH
    per = tm // POOL_HALO
    return pl.pallas_call(
        functools.partial(_pool_prompt_kernel, tm=tm),
        grid=(seq // tm,),
        in_specs=[pl.BlockSpec((POOL_HALO, B_WIDTH), lambda i: (jnp.maximum(i * per - 1, 0), ucol)),
                  pl.BlockSpec((tm, B_WIDTH), lambda i: (i, ucol)),
                  pl.BlockSpec((len(POOL_WINDOWS), B_GROUP, B_GROUP), lambda i: (0, 0, 0)),
                  pl.BlockSpec((1, B_WIDTH), lambda i: (0, 0))],
        out_specs=pl.BlockSpec((tm, B_WIDTH), lambda i: (i, 0)),
        out_shape=jax.ShapeDtypeStruct((seq, B_WIDTH), BF16),
        scratch_shapes=[pltpu.VMEM((POOL_HALO + tm, B_WIDTH), F32)],
        compiler_params=_cparams("arbitrary"),
        name=name,
    )(h, h, pool_w, pool_scale.reshape(1, B_WIDTH))


def _pool_sample_kernel(hist_ref, u_ref, pw_ref, sc_ref, o_ref, ext_ref, *, t, pos0):
    ext_ref[0:POOL_HALO, :] = hist_ref[0]
    ext_ref[POOL_HALO:POOL_HALO + t, :] = u_ref[...]
    o_ref[...] = _pool_core(ext_ref, t, pos0, pw_ref, sc_ref).astype(o_ref.dtype)


def pool_sample(h, hist, pool_w, pool_scale, *, row0, nb, t, pos0, name):
    ucol = 3 * A_WIDTH // B_WIDTH
    base = row0 // t
    return pl.pallas_call(
        functools.partial(_pool_sample_kernel, t=t, pos0=pos0),
        grid=(nb,),
        in_specs=[pl.BlockSpec((1, POOL_HALO, B_WIDTH), lambda b: (b, 0, 0)),
                  pl.BlockSpec((t, B_WIDTH), lambda b: (base + b, ucol)),
                  pl.BlockSpec((len(POOL_WINDOWS), B_GROUP, B_GROUP), lambda b: (0, 0, 0)),
                  pl.BlockSpec((1, B_WIDTH), lambda b: (0, 0))],
        out_specs=pl.BlockSpec((t, B_WIDTH), lambda b: (b, 0)),
        out_shape=jax.ShapeDtypeStruct((nb * t, B_WIDTH), BF16),
        scratch_shapes=[pltpu.VMEM((POOL_HALO + t, B_WIDTH), F32)],
        compiler_params=_cparams("arbitrary"),
        name=name,
    )(hist, h, pool_w, pool_scale.reshape(1, B_WIDTH))


CUMSUM_ALIGN = 8 * LANES


def _cumsum_kernel(x_ref, before_ref, o_ref):
    rows = x_ref.shape[1]
    r = lax.broadcasted_iota(jnp.int32, (LANES, LANES), 0)
    c = lax.broadcasted_iota(jnp.int32, (LANES, LANES), 1)
    upper = (r <= c).astype(F32)
    local = jnp.dot(x_ref[0], upper, preferred_element_type=F32, precision=lax.Precision.HIGHEST)
    tot = jnp.broadcast_to(local[:, LANES - 1:LANES], (rows, LANES))
    o_ref[0] = local + jnp.dot(before_ref[...], tot, preferred_element_type=F32, precision=lax.Precision.HIGHEST)


def cumsum_lanes(x, *, name):
    nb, nh, ln = x.shape
    nblk = ln // LANES
    rows = nh * nblk
    rid = jnp.arange(rows)
    before = ((rid[:, None] // nblk == rid[None, :] // nblk) & (rid[None, :] < rid[:, None])).astype(F32)
    out = pl.pallas_call(
        _cumsum_kernel,
        grid=(nb,),
        in_specs=[pl.BlockSpec((1, rows, LANES), lambda b: (b, 0, 0)),
                  pl.BlockSpec((rows, rows), lambda b: (0, 0))],
        out_specs=pl.BlockSpec((1, rows, LANES), lambda b: (b, 0, 0)),
        out_shape=jax.ShapeDtypeStruct((nb, rows, LANES), F32),
        compiler_params=_cparams("arbitrary"),
        name=name,
    )(x.reshape(nb, rows, LANES), before)
    return out.reshape(nb, nh, ln)


def _cumsum_rows_kernel(x_ref, o_ref, offs_ref, *, nblk):
    r = lax.broadcasted_iota(jnp.int32, (LANES, LANES), 0)
    c = lax.broadcasted_iota(jnp.int32, (LANES, LANES), 1)
    lower = (c <= r).astype(F32)

    def local(b, carry):
        off = pl.multiple_of(b * LANES, LANES)
        o_ref[pl.ds(off, LANES), :] = jnp.dot(lower, x_ref[pl.ds(off, LANES), :], preferred_element_type=F32,
                                              precision=lax.Precision.HIGHEST)
        return carry

    lax.fori_loop(0, nblk, local, 0)
    totals = o_ref[pl.ds(LANES - 1, nblk, stride=LANES), :]
    rb = lax.broadcasted_iota(jnp.int32, (nblk, nblk), 0)
    cb = lax.broadcasted_iota(jnp.int32, (nblk, nblk), 1)
    offs_ref[...] = jnp.dot((cb < rb).astype(F32), totals, preferred_element_type=F32,
                            precision=lax.Precision.HIGHEST)

    def shift(b, carry):
        off = pl.multiple_of(b * LANES, LANES)
        o_ref[pl.ds(off, LANES), :] = o_ref[pl.ds(off, LANES), :] + offs_ref[pl.ds(b, 1), :]
        return carry

    lax.fori_loop(0, nblk, shift, 0)


def cumsum_rows(x, *, name):
    ln, nh = x.shape
    nblk = ln // LANES
    return pl.pallas_call(
        functools.partial(_cumsum_rows_kernel, nblk=nblk),
        grid=(1,),
        in_specs=[pl.BlockSpec((ln, nh), lambda i: (0, 0))],
        out_specs=pl.BlockSpec((ln, nh), lambda i: (0, 0)),
        out_shape=jax.ShapeDtypeStruct((ln, nh), F32),
        scratch_shapes=[pltpu.VMEM((nblk, nh), F32)],
        compiler_params=_cparams("arbitrary"),
        name=name,
    )(x)


FA_CHUNK = 64
FA_QCOLS = 256


def _split3(x):
    hi = x.astype(BF16)
    r = x - hi.astype(F32)
    mid = r.astype(BF16)
    lo = (r - mid.astype(F32)).astype(BF16)
    return hi, mid, lo


def _spare_lane_columns(lane, base, first, second):
    out = jnp.zeros(jnp.broadcast_shapes(lane.shape, first[0].shape, second[0].shape), F32)
    for j, val in enumerate(tuple(first) + tuple(second)):
        out = jnp.where(lane == base + j, val.astype(F32), out)
    return out


def _fa_prep_kernel(k_ref, v_ref, d_ref, ka_ref, va_ref):
    lane = lax.broadcasted_iota(jnp.int32, (1, LANES), 1)
    masks = _head_lane_masks()
    d = d_ref[...] * (-LOG2E)
    one = jnp.ones((1, 1), BF16)
    for h in range(C_HEADS):
        p, a = divmod(h, PAIR)
        sl = slice(p * LANES, (p + 1) * LANES)
        ext = _spare_lane_columns(lane, HEAD_DIM * (1 - a), _split3(d[:, h:h + 1]), (one, one, one))
        ka_ref[h] = jnp.where(masks[a], k_ref[:, sl].astype(F32), ext).astype(BF16)
        if a == 0:
            va_ref[p] = v_ref[:, sl].T.astype(BF16)


def fa_prep(qkv16, v32, dcum, *, seq, tr, name):
    kcol = 1
    npair = C_HEADS // PAIR
    return pl.pallas_call(
        _fa_prep_kernel,
        grid=(seq // tr,),
        in_specs=[pl.BlockSpec((tr, C_WIDTH), lambda r: (r, kcol)),
                  pl.BlockSpec((tr, C_WIDTH), lambda r: (r, 0)),
                  pl.BlockSpec((tr, C_HEADS), lambda r: (r, 0))],
        out_specs=[pl.BlockSpec((C_HEADS, tr, LANES), lambda r: (0, r, 0)),
                   pl.BlockSpec((npair, LANES, tr), lambda r: (0, 0, r))],
        out_shape=[jax.ShapeDtypeStruct((C_HEADS, seq, LANES), BF16),
                   jax.ShapeDtypeStruct((npair, LANES, seq), BF16)],
        compiler_params=_cparams("arbitrary"),
        name=name,
    )(qkv16, v32, dcum)


def _fa_prompt_kernel(q_ref, ka_ref, va_ref, d_ref, o_ref, m_ref, l_ref, acc_ref, s0_ref, s1_ref, p_ref, *, tq, tk):
    p = pl.program_id(0)
    qi = pl.program_id(1)
    lane = lax.broadcasted_iota(jnp.int32, (1, LANES), 1)
    head_col = lax.broadcasted_iota(jnp.int32, (1, C_HEADS), 1)
    masks = _head_lane_masks()
    q = q_ref[...].astype(F32) * (HEAD_DIM ** -0.5 * LOG2E)
    d0 = d_ref[0:1, :] * LOG2E
    one = jnp.ones((1, 1), BF16)
    qa = []
    for a in range(PAIR):
        dref = jnp.sum(jnp.where(head_col == PAIR * p + a, d0, 0.0), axis=-1, keepdims=True)
        ext = _spare_lane_columns(lane, HEAD_DIM * (1 - a), (one, one, one), _split3(dref))
        qa.append(jnp.where(masks[a], q, ext).astype(BF16))

    m_ref[...] = jnp.full(m_ref.shape, NEG_INF, F32)
    l_ref[...] = jnp.zeros(l_ref.shape, F32)
    acc_ref[...] = jnp.zeros(acc_ref.shape, F32)

    jd = (qi * tq) // tk
    nchunk = tk // FA_CHUNK
    s_refs = (s0_ref, s1_ref)

    def scores(j, slot, diagonal):
        off = pl.multiple_of(j * tk, tk)
        s_ref = s_refs[slot]
        for a in range(PAIR):
            st = lax.dot_general(ka_ref[a, pl.ds(off, tk), :], qa[a], NT_DIMS, preferred_element_type=F32)
            if diagonal:
                krow = lax.broadcasted_iota(jnp.int32, (tk, tq), 0)
                qcol = lax.broadcasted_iota(jnp.int32, (tk, tq), 1)
                st = jnp.where(krow - qcol <= qi * tq - j * tk, st, NEG_INF)
            s_ref[a] = st

    def absorb(j, slot):
        off = pl.multiple_of(j * tk, tk)
        s_ref = s_refs[slot]
        for a in range(PAIR):
            vt = va_ref[0, a * HEAD_DIM:(a + 1) * HEAD_DIM, pl.ds(off, tk)]
            for g in range(tq // FA_QCOLS):
                cols = slice(g * FA_QCOLS, (g + 1) * FA_QCOLS)
                cm = None
                for c in range(nchunk):
                    x = s_ref[a, c * FA_CHUNK:(c + 1) * FA_CHUNK, cols].reshape(FA_CHUNK // 8, 8, FA_QCOLS)
                    part = jnp.max(x, axis=0)
                    cm = part if cm is None else jnp.maximum(cm, part)
                m_old = m_ref[a, :, cols]
                m_new = jnp.maximum(m_old, jnp.max(cm, axis=0, keepdims=True))
                alpha = jnp.exp2(m_old - m_new)
                cs = None
                for c in range(nchunk):
                    rows = slice(c * FA_CHUNK, (c + 1) * FA_CHUNK)
                    pc = jnp.exp2(s_ref[a, rows, cols] - m_new)
                    p_ref[a, rows, cols] = pc.astype(BF16)
                    part = jnp.sum(pc.reshape(FA_CHUNK // 8, 8, FA_QCOLS), axis=0)
                    cs = part if cs is None else cs + part
                pv = jnp.dot(vt, p_ref[a, :, cols], preferred_element_type=F32)
                acc_ref[a, :, cols] = alpha * acc_ref[a, :, cols] + pv
                l_ref[a, :, cols] = alpha * l_ref[a, :, cols] + jnp.sum(cs, axis=0, keepdims=True)
                m_ref[a, :, cols] = m_new

    scores(jd, 0, True)

    def body(u, carry):
        scores(2 * u, 1, False)
        absorb(jnp.where(u == 0, jd, 2 * u - 1), 0)
        scores(2 * u + 1, 0, False)
        absorb(2 * u, 1)
        return carry

    npairs = jd // 2
    lax.fori_loop(0, npairs, body, 0)
    last0 = jnp.where(npairs == 0, jd, 2 * npairs - 1)

    @pl.when(jd % 2 == 1)
    def _():
        scores(jd - 1, 1, False)
        absorb(last0, 0)
        absorb(jd - 1, 1)

    @pl.when(jd % 2 == 0)
    def _():
        absorb(last0, 0)

    out = jnp.concatenate([acc_ref[a] / l_ref[a] for a in range(PAIR)], axis=0)
    o_ref[...] = out.T.astype(o_ref.dtype)


def forgetting_attention_prompt(qkv16, ka, va, dcum, *, seq, tq, tk, name):
    npair = C_WIDTH // LANES
    return pl.pallas_call(
        functools.partial(_fa_prompt_kernel, tq=tq, tk=tk),
        grid=(npair, seq // tq),
        in_specs=[pl.BlockSpec((tq, LANES), lambda p, i: (i, p)),
                  pl.BlockSpec((PAIR, seq, LANES), lambda p, i: (p, 0, 0)),
                  pl.BlockSpec((1, LANES, seq), lambda p, i: (p, 0, 0)),
                  pl.BlockSpec((8, C_HEADS), lambda p, i: (i * (tq // 8), 0))],
        out_specs=pl.BlockSpec((tq, LANES), lambda p, i: (i, p)),
        out_shape=jax.ShapeDtypeStruct((seq, C_WIDTH), BF16),
        scratch_shapes=[pltpu.VMEM((PAIR, 1, tq), F32),
                        pltpu.VMEM((PAIR, 1, tq), F32),
                        pltpu.VMEM((PAIR, HEAD_DIM, tq), F32),
                        pltpu.VMEM((PAIR, tk, tq), F32),
                        pltpu.VMEM((PAIR, tk, tq), F32),
                        pltpu.VMEM((PAIR, tk, tq), BF16)],
        compiler_params=_cparams("arbitrary", "arbitrary"),
        name=name,
    )(qkv16, ka, va, dcum)


def _fa_sample_kernel(q_ref, kn_ref, vn_ref, kc_ref, vc_ref, dc_ref, dp_ref, dn_ref, o_ref,
                      qr_ref, m_ref, l_ref, acc_ref, *, t, nkc):
    kc_i = pl.program_id(1)
    scale = HEAD_DIM ** -0.5
    rows = C_HEADS * t

    @pl.when(kc_i == 0)
    def _():
        qr_ref[...] = _head_rows(q_ref[...], C_HEADS)
        m_ref[...] = jnp.full(m_ref.shape, NEG_INF, F32)
        l_ref[...] = jnp.zeros(l_ref.shape, F32)
        acc_ref[...] = jnp.zeros(acc_ref.shape, F32)

    qr = qr_ref[...]
    d_tot = dp_ref[0][:, :, LANES - 1:LANES]

    def update(s, v, dims=(((1,), (0,)), ((), ()))):
        m = m_ref[...]
        m_new = jnp.maximum(m, jnp.max(s, axis=-1, keepdims=True))
        alpha = jnp.exp(m - m_new)
        p = jnp.exp(s - m_new)
        l_ref[...] = alpha * l_ref[...] + jnp.sum(p, axis=-1, keepdims=True)
        acc_ref[...] = alpha * acc_ref[...] + lax.dot_general(p.astype(BF16), v, dims, preferred_element_type=F32)
        m_ref[...] = m_new

    tk = kc_ref.shape[4]
    kct = kc_ref[0, 0].reshape(C_WIDTH, tk).astype(BF16)
    vct = vc_ref[0, 0].reshape(C_WIDTH, tk).astype(BF16)
    bias_c = jnp.broadcast_to(d_tot - dc_ref[0], (C_HEADS, t, tk)).reshape(rows, tk)
    s_c = jnp.dot(qr, kct, preferred_element_type=F32) * scale + bias_c
    update(s_c, vct, NT_DIMS)

    @pl.when(kc_i == nkc - 1)
    def _():
        bias_n = jnp.broadcast_to(d_tot - dn_ref[0][:, :, 0:t], (C_HEADS, t, t)).reshape(rows, t)
        s_n = lax.dot_general(qr, kn_ref[...], NT_DIMS, preferred_element_type=F32) * scale + bias_n
        qpos = lax.broadcasted_iota(jnp.int32, (C_HEADS, t, t), 1).reshape(rows, t)
        kpos = lax.broadcasted_iota(jnp.int32, (C_HEADS, t, t), 2).reshape(rows, t)
        s_n = jnp.where(kpos <= qpos, s_n, NEG_INF)
        update(s_n, vn_ref[...])
        r = acc_ref[...] / l_ref[...]
        o_ref[...] = _head_diag(r, C_HEADS, t).astype(o_ref.dtype)


def forgetting_attention_sample(qkv, k_cache_t, v_cache_t, dall, *, pair, row0, nb, t, tk, name):
    base = row0 // t
    past = k_cache_t.shape[4]
    nkc = past // tk
    rows = C_HEADS * t
    return pl.pallas_call(
        functools.partial(_fa_sample_kernel, t=t, nkc=nkc),
        grid=(nb, nkc),
        in_specs=[pl.BlockSpec((t, C_WIDTH), lambda b, c: (base + b, 0)),
                  pl.BlockSpec((t, C_WIDTH), lambda b, c: (base + b, 1)),
                  pl.BlockSpec((t, C_WIDTH), lambda b, c: (base + b, 2)),
                  pl.BlockSpec((1, 1, C_HEADS, HEAD_DIM, tk), lambda b, c: (pair, b, 0, 0, c)),
                  pl.BlockSpec((1, 1, C_HEADS, HEAD_DIM, tk), lambda b, c: (pair, b, 0, 0, c)),
                  pl.BlockSpec((1, C_HEADS, 1, tk), lambda b, c: (b, 0, 0, c)),
                  pl.BlockSpec((1, C_HEADS, 1, LANES), lambda b, c: (b, 0, 0, past // LANES - 1)),
                  pl.BlockSpec((1, C_HEADS, 1, LANES), lambda b, c: (b, 0, 0, past // LANES))],
        out_specs=pl.BlockSpec((t, C_WIDTH), lambda b, c: (b, 0)),
        out_shape=jax.ShapeDtypeStruct((nb * t, C_WIDTH), BF16),
        scratch_shapes=[pltpu.VMEM((rows, C_WIDTH), BF16),
                        pltpu.VMEM((rows, 1), F32),
                        pltpu.VMEM((rows, 1), F32),
                        pltpu.VMEM((rows, C_WIDTH), F32)],
        compiler_params=_cparams("arbitrary", "arbitrary"),
        name=name,
    )(qkv, qkv, qkv, k_cache_t, v_cache_t, dall, dall, dall)


def _rel_bias_matrix(rel_table, nq, nk):
    span = nq + nk - 1
    diag = jnp.arange(span) - (nq - 1)
    vals = rel_table[:, jnp.clip(A_WINDOW - diag, -REL_CLIP, REL_CLIP) + REL_CLIP].astype(F32)
    nh = vals.shape[0]
    skew = jnp.tile(vals, (1, nq + 1))[:, :nq * (span + 1)].reshape(nh, nq, span + 1)
    return skew[:, ::-1, :nk]


def _band_block_bias(bias_chunk):
    nchunk = A_BLOCK // CHUNK
    parts = [jnp.pad(bias_chunk * LOG2E, ((0, 0), (0, 0), (CHUNK * c, A_KEYS - A_BAND - CHUNK * c)),
                     constant_values=NEG_INF) for c in range(nchunk)]
    later = jnp.stack(parts, axis=1).reshape(A_HEADS, A_BLOCK, A_KEYS)
    col = jnp.arange(A_KEYS)
    first = jnp.where(col >= A_WINDOW, later, NEG_INF)
    return jnp.stack([first, later])


def kernel(x_prompt, x_sample, cache_a_k, cache_a_v, state_pool, cache_c_k, cache_c_v, cache_c_logf,
           cache_mem_k, cache_mem_v, mem_prompt, w_in_ab, rel_bias_a, pool_w, pool_scale, w_out_ab,
           w_in_c, b_f, w_out_c, w_xq, w_xk, w_xv, w_xo, ln_g, ln_b, ffn_w1, ffn_w3, ffn_w2,
           w_router, b_router, moe_w1, moe_w3, moe_w2):
    bp, seq, d = x_prompt.shape
    nb, t, _ = x_sample.shape
    assert bp == 1 and d == D_MODEL
    ns = nb * t
    tot = seq + ns
    past = cache_c_k.shape[2]
    tm = 512
    assert tot % tm == 0 and seq % tm == 0

    x = jnp.concatenate([x_prompt.reshape(seq, d), x_sample.reshape(ns, d)], axis=0)

    mem = mem_prompt.reshape(N_MEM, d)
    p_mem_k, p_mem_v = [], []
    for layer in range(DEPTH):
        (mk,) = matmul(mem, w_xk[layer].astype(BF16), [F32], tm=N_MEM, tn=d, name=f"mem_k{layer}")
        (mv,) = matmul(mem, w_xv[layer].astype(BF16), [F32], tm=N_MEM, tn=d, name=f"mem_v{layer}")
        p_mem_k.append(mk)
        p_mem_v.append(mv)

    def cross_block(x, layer):
        (q,) = matmul(x, w_xq[layer].astype(BF16), [BF16], tm=tm, tn=d, name=f"xq{layer}")
        o_p = cross_attention(q, p_mem_k[layer].reshape(1, N_MEM, d), p_mem_v[layer].reshape(1, N_MEM, d),
                              row0=0, rows=seq, tq=tm, name=f"xattn_p{layer}")
        o_s = cross_attention(q, cache_mem_k[layer].reshape(nb, N_MEM, d), cache_mem_v[layer].reshape(nb, N_MEM, d),
                              row0=seq, rows=ns, tq=t, name=f"xattn_s{layer}")
        return matmul_ln([(o_p, o_s)], w_xo[layer].astype(BF16), x, ln_g[layer, 1], ln_b[layer, 1], tm=tm,
                         name=f"xo{layer}")

    pr = 0
    (h,) = matmul(x, w_in_ab[pr].astype(BF16), [F32], tm=tm, tn=w_in_ab.shape[-1], name="in_ab")
    bias = _rel_bias_matrix(rel_bias_a[pr], CHUNK, A_BAND)
    oa_p = band_attention_prompt(h, _band_block_bias(bias), seq=seq, name="band_p")
    bias_s = bias[:, :t, :A_WINDOW + t].reshape(A_HEADS * t, A_WINDOW + t)
    feature_major = (0, 1, 3, 4, 2)
    oa_s = band_attention_sample(h, jnp.transpose(cache_a_k, feature_major), jnp.transpose(cache_a_v, feature_major),
                                 bias_s[:, :A_WINDOW], bias_s[:, A_WINDOW:], pair=pr, row0=seq, nb=nb, t=t,
                                 name="band_s")
    pw = pool_w[pr].astype(BF16)
    ob_p = pool_prompt(h, pw, pool_scale[pr], seq=seq, tm=tm, name="pool_p")
    hist = jnp.concatenate([jnp.zeros((nb, 1, B_WIDTH), F32), state_pool[pr]], axis=1)
    ob_s = pool_sample(h, hist, pw, pool_scale[pr], row0=seq, nb=nb, t=t, pos0=past, name="pool_s")
    x = matmul_ln([(oa_p, oa_s), (ob_p, ob_s)], w_out_ab[pr].astype(BF16), x, ln_g[0, 0], ln_b[0, 0], tm=tm,
                  name="out_ab")
    x = cross_block(x, 0)
    ones = jnp.ones((1, tot, 1), F32)
    x = moe_ln(x, ones, ffn_w1.astype(BF16), ffn_w3.astype(BF16), ffn_w2.astype(BF16),
               ln_g[0, 2], ln_b[0, 2], tm=tm, tf=D_FF // 2, name="ffn")

    k_a = h[:, A_WIDTH:2 * A_WIDTH]
    v_a = h[:, 2 * A_WIDTH:3 * A_WIDTH]
    u_b = h[:, 3 * A_WIDTH:]
    keep = min(A_WINDOW, seq)
    p_a_k = k_a[seq - keep:seq].reshape(1, 1, keep, A_HEADS, HEAD_DIM)
    p_a_v = v_a[seq - keep:seq].reshape(1, 1, keep, A_HEADS, HEAD_DIM)
    p_pool = u_b[seq - POOL_STATE:seq].reshape(1, 1, POOL_STATE, B_WIDTH)
    s_a_k = k_a[seq:].reshape(1, nb, t, A_HEADS, HEAD_DIM)
    s_a_v = v_a[seq:].reshape(1, nb, t, A_HEADS, HEAD_DIM)
    s_pool = jnp.concatenate([state_pool[pr], u_b[seq:].reshape(nb, t, B_WIDTH)], axis=1)[:, -POOL_STATE:][None]

    w_c = w_in_c[pr]
    w_cf = jnp.pad(w_c, ((0, 0), (0, LANES - C_HEADS))).astype(BF16)
    bf_pad = jnp.pad(b_f[pr].astype(F32), (0, LANES - C_HEADS)).reshape(1, LANES)
    qkv16, logf, k_cp, v_cp, k_cs, v_cs = matmul_qkv(x, w_cf, bf_pad, seq=seq, tm=tm, name="in_c")
    logf = logf[:, :C_HEADS]
    dcum_p = cumsum_rows(logf[:seq], name="cumsum_p")
    ka, va = fa_prep(qkv16, v_cp, dcum_p, seq=seq, tr=tm, name="fa_prep")
    oc_p = forgetting_attention_prompt(qkv16, ka, va, dcum_p, seq=seq, tq=tm, tk=tm, name="fa_p")
    lf_s = logf[seq:].reshape(nb, t, C_HEADS)
    lf_all = jnp.concatenate([cache_c_logf[pr].astype(F32), lf_s,
                              jnp.zeros((nb, CUMSUM_ALIGN - t, C_HEADS), F32)], axis=1)
    dall = cumsum_lanes(jnp.swapaxes(lf_all, 1, 2), name="cumsum_s").reshape(nb, C_HEADS, 1, past + CUMSUM_ALIGN)
    oc_s = forgetting_attention_sample(qkv16, jnp.transpose(cache_c_k, feature_major),
                                       jnp.transpose(cache_c_v, feature_major), dall, pair=pr,
                                       row0=seq, nb=nb, t=t, tk=2048, name="fa_s")
    x = matmul_ln([(oc_p, oc_s)], w_out_c[pr].astype(BF16), x, ln_g[1, 0], ln_b[1, 0], tm=tm, name="out_c")
    x = cross_block(x, 1)
    w_r = jnp.pad(w_router[pr], ((0, 0), (0, LANES - N_EXPERTS)))
    b_r = jnp.concatenate([b_router[pr].astype(F32), jnp.full((LANES - N_EXPERTS,), NEG_INF, F32)]).reshape(1, LANES)
    y_p, y_s = moe_routed_ln(x, w_r, b_r, moe_w1[pr].astype(BF16), moe_w3[pr].astype(BF16),
                             moe_w2[pr].astype(BF16), ln_g[1, 2], ln_b[1, 2], tm=tm, split=seq, name="moe")

    p_c_k = k_cp.reshape(1, 1, seq, C_HEADS, HEAD_DIM)
    p_c_v = v_cp.reshape(1, 1, seq, C_HEADS, HEAD_DIM)
    p_c_logf = logf[:seq].reshape(1, 1, seq, C_HEADS)
    s_c_k = k_cs.reshape(1, nb, t, C_HEADS, HEAD_DIM)
    s_c_v = v_cs.reshape(1, nb, t, C_HEADS, HEAD_DIM)
    s_c_logf = lf_s[None]

    y_prompt = y_p.reshape(1, seq, d)
    y_sample = y_s.reshape(nb, t, d)
    pmk = jnp.stack(p_mem_k).reshape(DEPTH, 1, N_MEM, X_HEADS, X_HEAD_DIM)
    pmv = jnp.stack(p_mem_v).reshape(DEPTH, 1, N_MEM, X_HEADS, X_HEAD_DIM)
    return (y_prompt, y_sample, p_a_k, p_a_v, p_pool, p_c_k, p_c_v, p_c_logf, pmk, pmv,
            s_a_k, s_a_v, s_pool, s_c_k, s_c_v, s_c_logf)
```
